```python
import math
import jax, jax.numpy as jnp
from jax import lax
import numpy as np


D_MODEL = 2048
BATCH = 2
SEQ = 4096
DEPTH = 1

ATTN_WIDTH = D_MODEL // 2
ATTN_HEAD_DIM = 128
N_ATTN_HEADS = ATTN_WIDTH // ATTN_HEAD_DIM
SSM_WIDTH = D_MODEL - ATTN_WIDTH
SSM_HEAD_DIM = 64
N_SSM_HEADS = SSM_WIDTH // SSM_HEAD_DIM
N_SSM_GROUPS = 2
SSM_HEADS_PER_GROUP = N_SSM_HEADS // N_SSM_GROUPS
D_STATE = 128
SSM_CONV = 4
SSM_CHUNK = 256
SSM_CONV_CH = SSM_WIDTH + 2 * N_SSM_GROUPS * D_STATE
MOBA_BLOCK = 256
MOBA_TOPK = 3
MOBA_Q_CHUNK = 32
REL_BUCKETS = 32
REL_MAX_DIST = 128
FFN_DIM = 5632
FFN_CONV = 3
IN_COLS = 3 * ATTN_WIDTH + 2 * SSM_WIDTH + 2 * N_SSM_GROUPS * D_STATE + N_SSM_HEADS
EPS = 1e-6

kernel_name = 'hymba_moba_ssd_convffn_adaln'


def rms_norm(x, g):
    xf = x.astype(jnp.float32)
    y = xf * lax.rsqrt(jnp.mean(xf * xf, axis=-1, keepdims=True) + EPS)
    return (y * g.astype(jnp.float32)).astype(x.dtype)


def causal_dwconv(x, w, b):
    k = w.shape[0]
    y = lax.conv_general_dilated(x, w[:, None, :].astype(x.dtype), window_strides=(1,),
                                 padding=[(k - 1, 0)], dimension_numbers=('NWC', 'WIO', 'NWC'),
                                 feature_group_count=x.shape[-1])
    return y + b.astype(x.dtype)


def pad_seq(x, mult):
    p = (-x.shape[1]) % mult
    return jnp.pad(x, [(0, 0), (0, p)] + [(0, 0)] * (x.ndim - 2))


def rel_bucket(dist):
    n = jnp.maximum(dist, 0)
    max_exact = REL_BUCKETS // 2
    nf = jnp.maximum(n, max_exact).astype(jnp.float32)
    large = max_exact + (jnp.log(nf / max_exact) / math.log(REL_MAX_DIST / max_exact)
                         * (REL_BUCKETS - max_exact)).astype(jnp.int32)
    large = jnp.minimum(large, REL_BUCKETS - 1)
    return jnp.where(n < max_exact, n, large)


def gather_blocks(blocks, sel):
    return jax.vmap(jax.vmap(lambda kb, s: kb[s]))(blocks, sel)


def moba_attention(q, k, v, rel_bias):
    bsz, s = q.shape[:2]
    q, k, v = [pad_seq(t, MOBA_BLOCK).transpose(0, 2, 1, 3) for t in (q, k, v)]
    sp = q.shape[2]
    nb = sp // MOBA_BLOCK
    n_sel = min(MOBA_TOPK, nb - 1)
    k_blocks = k.reshape(bsz, N_ATTN_HEADS, nb, MOBA_BLOCK, ATTN_HEAD_DIM)
    v_blocks = v.reshape(bsz, N_ATTN_HEADS, nb, MOBA_BLOCK, ATTN_HEAD_DIM)
    k_mean = jnp.mean(k_blocks, axis=3)
    scale = ATTN_HEAD_DIM ** -0.5
    bias_t = rel_bias.T.astype(jnp.float32)
    head_idx = jnp.arange(N_ATTN_HEADS)[:, None, None, None]

    def chunk(ci):
        q0 = ci * MOBA_Q_CHUNK
        blk = q0 // MOBA_BLOCK
        qc = lax.dynamic_slice_in_dim(q, q0, MOBA_Q_CHUNK, axis=2)
        q_pos = q0 + jnp.arange(MOBA_Q_CHUNK)
        k_own = lax.dynamic_index_in_dim(k_blocks, blk, axis=2, keepdims=False)
        v_own = lax.dynamic_index_in_dim(v_blocks, blk, axis=2, keepdims=False)
        dist_own = q_pos[:, None] - (blk * MOBA_BLOCK + jnp.arange(MOBA_BLOCK))[None, :]
        logit_own = (jnp.einsum('bhqd,bhkd->bhqk', qc, k_own).astype(jnp.float32) * scale
                     + bias_t[:, rel_bucket(dist_own)])
        logit_own = jnp.where(dist_own >= 0, logit_own, -jnp.inf)
        if n_sel == 0:
            p_own = jax.nn.softmax(logit_own, axis=-1).astype(v.dtype)
            return jnp.einsum('bhqk,bhkd->bhqd', p_own, v_own)
        gate = jnp.einsum('bhqd,bhnd->bhqn', qc, k_mean).astype(jnp.float32)
        gate = jnp.where(jnp.arange(nb) < blk, gate, -jnp.inf)
        _, sel = lax.top_k(gate, n_sel)
        k_sel = gather_blocks(k_blocks, sel)
        v_sel = gather_blocks(v_blocks, sel)
        sel_pos = sel[..., None] * MOBA_BLOCK + jnp.arange(MOBA_BLOCK)
        dist_sel = q_pos[:, None, None] - sel_pos
        logit_sel = (jnp.einsum('bhqd,bhqtkd->bhqtk', qc, k_sel).astype(jnp.float32) * scale
                     + bias_t[head_idx, rel_bucket(dist_sel)])
        slot_ok = (jnp.arange(n_sel) < blk)[:, None]
        logit_sel = jnp.where(slot_ok, logit_sel, -jnp.inf)
        logits = jnp.concatenate(
            [logit_sel.reshape(bsz, N_ATTN_HEADS, MOBA_Q_CHUNK, n_sel * MOBA_BLOCK), logit_own], axis=-1)
        p = jax.nn.softmax(logits, axis=-1).astype(v.dtype)
        p_sel = p[..., :n_sel * MOBA_BLOCK].reshape(bsz, N_ATTN_HEADS, MOBA_Q_CHUNK, n_sel, MOBA_BLOCK)
        p_own = p[..., n_sel * MOBA_BLOCK:]
        return (jnp.einsum('bhqtk,bhqtkd->bhqd', p_sel, v_sel)
                + jnp.einsum('bhqk,bhkd->bhqd', p_own, v_own))

    out = lax.map(chunk, jnp.arange(sp // MOBA_Q_CHUNK))
    out = out.transpose(1, 0, 3, 2, 4).reshape(bsz, sp, N_ATTN_HEADS * ATTN_HEAD_DIM)
    return out[:, :s]


def segsum(x):
    t = x.shape[-1]
    xe = jnp.broadcast_to(x[..., :, None], x.shape + (t,))
    xe = jnp.where(jnp.tril(jnp.ones((t, t), bool), -1), xe, 0.0)
    xs = jnp.cumsum(xe, axis=-2)
    return jnp.where(jnp.tril(jnp.ones((t, t), bool)), xs, -jnp.inf)


def ssd_mixer(xs, z, bmat, cmat, dt_raw, conv_w, conv_b, dt_bias, a_log, d_skip, norm_g):
    bsz, s = xs.shape[:2]
    gn = N_SSM_GROUPS * D_STATE
    xbc = jax.nn.silu(causal_dwconv(jnp.concatenate([xs, bmat, cmat], axis=-1), conv_w, conv_b))
    x_h = xbc[..., :SSM_WIDTH].reshape(bsz, s, N_SSM_HEADS, SSM_HEAD_DIM).astype(jnp.float32)
    bm = xbc[..., SSM_WIDTH:SSM_WIDTH + gn].astype(jnp.float32)
    cm = xbc[..., SSM_WIDTH + gn:].astype(jnp.float32)
    dt = jax.nn.softplus(dt_raw.astype(jnp.float32) + dt_bias.astype(jnp.float32))
    a = -jnp.exp(a_log.astype(jnp.float32))
    xdt = pad_seq(x_h * dt[..., None], SSM_CHUNK)
    adt = pad_seq(dt * a, SSM_CHUNK)
    bm, cm = pad_seq(bm, SSM_CHUNK), pad_seq(cm, SSM_CHUNK)
    sp = xdt.shape[1]
    nc = sp // SSM_CHUNK
    X = xdt.reshape(bsz, nc, SSM_CHUNK, N_SSM_GROUPS, SSM_HEADS_PER_GROUP, SSM_HEAD_DIM)
    A = adt.reshape(bsz, nc, SSM_CHUNK, N_SSM_GROUPS, SSM_HEADS_PER_GROUP).transpose(0, 3, 4, 1, 2)
    Bm = bm.reshape(bsz, nc, SSM_CHUNK, N_SSM_GROUPS, D_STATE)
    Cm = cm.reshape(bsz, nc, SSM_CHUNK, N_SSM_GROUPS, D_STATE)
    A_cs = jnp.cumsum(A, axis=-1)
    Lm = jnp.exp(segsum(A))
    y_diag = jnp.einsum('bclgn,bcsgn,bgrcls,bcsgrp->bclgrp', Cm, Bm, Lm, X)
    decay_states = jnp.exp(A_cs[..., -1:] - A_cs)
    states = jnp.einsum('bclgn,bgrcl,bclgrp->bcgrpn', Bm, decay_states, X)
    states = jnp.concatenate([jnp.zeros_like(states[:, :1]), states], axis=1)
    chunk_tot = jnp.pad(A_cs[..., -1], ((0, 0), (0, 0), (0, 0), (1, 0)))
    decay_chunk = jnp.exp(segsum(chunk_tot))
    new_states = jnp.einsum('bgrzc,bcgrpn->bzgrpn', decay_chunk, states)
    prev_states = new_states[:, :-1]
    y_off = jnp.einsum('bclgn,bcgrpn,bgrcl->bclgrp', Cm, prev_states, jnp.exp(A_cs))
    y = (y_diag + y_off).reshape(bsz, sp, N_SSM_HEADS, SSM_HEAD_DIM)[:, :s]
    y = y + d_skip.astype(jnp.float32)[:, None] * x_h
    y = y.reshape(bsz, s, SSM_WIDTH) * jax.nn.silu(z.astype(jnp.float32))
    yg = y.reshape(bsz, s, N_SSM_GROUPS, SSM_WIDTH // N_SSM_GROUPS)
    yg = yg * lax.rsqrt(jnp.mean(yg * yg, axis=-1, keepdims=True) + EPS)
    return (yg.reshape(bsz, s, SSM_WIDTH) * norm_g.astype(jnp.float32)).astype(xs.dtype)


def conv_ffn(h, w_up, conv_w, conv_b, w_down):
    u = causal_dwconv(h @ w_up, conv_w, conv_b)
    g, val = jnp.split(u, 2, axis=-1)
    return (jax.nn.silu(g) * val) @ w_down


def setup_inputs(seed: int = 0) -> dict:
    key = jax.random.key(seed)
    ks = jax.random.split(key, 24)
    f32 = jnp.float32
    nrm = lambda k, shape, sc: jax.random.normal(k, shape, f32) * sc
    dt0 = jnp.exp(jax.random.uniform(ks[10], (DEPTH, N_SSM_HEADS), f32)
                  * (math.log(0.1) - math.log(0.001)) + math.log(0.001))
    return {
        'x': nrm(ks[0], (BATCH, SEQ, D_MODEL), 1.0),
        'c': nrm(ks[1], (BATCH, D_MODEL), 1.0),
        'w_ada': nrm(ks[2], (DEPTH, D_MODEL, 6 * D_MODEL), 0.5 * D_MODEL ** -0.5),
        'b_ada': nrm(ks[3], (DEPTH, 6 * D_MODEL), 0.01),
        'norm_mix_g': 1.0 + nrm(ks[4], (DEPTH, D_MODEL), 0.05),
        'w_in': nrm(ks[5], (DEPTH, D_MODEL, IN_COLS), D_MODEL ** -0.5),
        'rel_bias': nrm(ks[6], (REL_BUCKETS, N_ATTN_HEADS), 0.2),
        'attn_norm_g': 1.0 + nrm(ks[7], (DEPTH, ATTN_WIDTH), 0.05),
        'conv_ssm_w': nrm(ks[8], (DEPTH, SSM_CONV, SSM_CONV_CH), SSM_CONV ** -0.5),
        'conv_ssm_b': nrm(ks[9], (DEPTH, SSM_CONV_CH), 0.01),
        'dt_bias': dt0 + jnp.log(-jnp.expm1(-dt0)),
        'a_log': jnp.log(jax.random.uniform(ks[11], (DEPTH, N_SSM_HEADS), f32, 1.0, 16.0)),
        'd_skip': 1.0 + nrm(ks[12], (DEPTH, N_SSM_HEADS), 0.1),
        'ssm_norm_g': 1.0 + nrm(ks[13], (DEPTH, SSM_WIDTH), 0.05),
        'w_out': nrm(ks[14], (DEPTH, ATTN_WIDTH + SSM_WIDTH, D_MODEL), (ATTN_WIDTH + SSM_WIDTH) ** -0.5),
        'norm_ffn_g': 1.0 + nrm(ks[15], (DEPTH, D_MODEL), 0.05),
        'w_up': nrm(ks[16], (DEPTH, D_MODEL, 2 * FFN_DIM), D_MODEL ** -0.5),
        'conv_ffn_w': nrm(ks[17], (DEPTH, FFN_CONV, 2 * FFN_DIM), FFN_CONV ** -0.5),
        'conv_ffn_b': nrm(ks[18], (DEPTH, 2 * FFN_DIM), 0.01),
        'w_down': nrm(ks[19], (DEPTH, FFN_DIM, D_MODEL), FFN_DIM ** -0.5),
        'final_norm_g': 1.0 + nrm(ks[20], (D_MODEL,), 0.05),
    }


def reference(x, c, w_ada, b_ada, norm_mix_g, w_in, rel_bias, attn_norm_g, conv_ssm_w, conv_ssm_b,
              dt_bias, a_log, d_skip, ssm_norm_g, w_out, norm_ffn_g, w_up, conv_ffn_w, conv_ffn_b,
              w_down, final_norm_g):
    bsz, s, _ = x.shape
    gn = N_SSM_GROUPS * D_STATE
    sizes = [ATTN_WIDTH, ATTN_WIDTH, ATTN_WIDTH, SSM_WIDTH, SSM_WIDTH, gn, gn, N_SSM_HEADS]
    split_at = [int(v) for v in np.cumsum(sizes)[:-1]]
    for l in range(DEPTH):
        mod = jax.nn.silu(c) @ w_ada[l] + b_ada[l]
        shift_m, scale_m, gate_m, shift_f, scale_f, gate_f = jnp.split(mod[:, None, :], 6, axis=-1)
        h = rms_norm(x, norm_mix_g[l]) * (1.0 + scale_m) + shift_m
        proj = h @ w_in[l]
        q, k, v, xs, z, bmat, cmat, dt_raw = jnp.split(proj, split_at, axis=-1)
        hd = (bsz, s, N_ATTN_HEADS, ATTN_HEAD_DIM)
        attn = moba_attention(q.reshape(hd), k.reshape(hd), v.reshape(hd), rel_bias)
        attn = rms_norm(attn, attn_norm_g[l])
        ssm = ssd_mixer(xs, z, bmat, cmat, dt_raw, conv_ssm_w[l], conv_ssm_b[l], dt_bias[l],
                        a_log[l], d_skip[l], ssm_norm_g[l])
        mix = jnp.concatenate([attn, ssm], axis=-1) @ w_out[l]
        x = x + gate_m * mix
        h = rms_norm(x, norm_ffn_g[l]) * (1.0 + scale_f) + shift_f
        x = x + gate_f * conv_ffn(h, w_up[l], conv_ffn_w[l], conv_ffn_b[l], w_down[l])
    return rms_norm(x, final_norm_g)
```

```python
import functools
import math

import jax
import jax.numpy as jnp
from jax import lax
from jax.experimental import pallas as pl
from jax.experimental.pallas import tpu as pltpu

F32 = jnp.float32
BF16 = jnp.bfloat16

D_MODEL = 2048
ATTN_WIDTH = 1024
HEAD_DIM = 128
N_HEADS = 8
SSM_WIDTH = 1024
SSM_HEAD_DIM = 64
N_SSM_HEADS = 16
N_GROUPS = 2
GROUP_WIDTH = SSM_WIDTH // N_GROUPS
D_STATE = 128
SSM_CONV = 4
CHUNK = 256
MOBA_BLOCK = 256
MOBA_TOPK = 3
REL_BUCKETS = 32
REL_MAX_DIST = 128
FFN_DIM = 5632
FFN_CONV = 3
EPS = 1e-6
NEG = -1e30

LANES = 128
SUBLANES = 8
VMEM_LIMIT = 56 * 1024 * 1024

REST_COLS = 3840
QK_COLS = 2 * ATTN_WIDTH


def _params(sem):
    return pltpu.CompilerParams(dimension_semantics=sem, vmem_limit_bytes=VMEM_LIMIT)


def _split3(x):
    hi = x.astype(BF16)
    r = x - hi.astype(F32)
    mid = r.astype(BF16)
    lo = (r - mid.astype(F32)).astype(BF16)
    return hi, mid, lo


def _dot(a, b, dims=(((1,), (0,)), ((), ()))):
    return lax.dot_general(a, b, dims, preferred_element_type=F32)


def _dot_exact_lhs(a_bf, x, dims=(((1,), (0,)), ((), ()))):
    hi, mid, lo = _split3(x)
    return _dot(a_bf, hi, dims) + _dot(a_bf, mid, dims) + _dot(a_bf, lo, dims)


def _dot_exact_rhs(x, b_bf, dims=(((1,), (0,)), ((), ()))):
    hi, mid, lo = _split3(x)
    return _dot(hi, b_bf, dims) + _dot(mid, b_bf, dims) + _dot(lo, b_bf, dims)


def _silu(x):
    return x * jax.nn.sigmoid(x)


def _rms(x):
    return x * lax.rsqrt(jnp.mean(x * x, axis=-1, keepdims=True) + EPS)


def _ada_kernel(ct_ref, w_ref, b_ref, o_ref, sb_ref):
    nb = sb_ref.shape[0]
    d = w_ref.shape[0]
    tn = o_ref.shape[-1]

    @pl.when(pl.program_id(0) == 0)
    def _():
        ct = ct_ref[...]
        st = _silu(ct)
        for b in range(nb):
            sb_ref[b] = jnp.broadcast_to(st[:, b:b + 1], (d, LANES))

    def body(kc, accs):
        r = pl.multiple_of(kc * SUBLANES, SUBLANES)
        w8 = w_ref[pl.ds(r, SUBLANES), :]
        out = []
        for b in range(nb):
            s8 = sb_ref[b, pl.ds(r, SUBLANES), :]
            out.append(accs[b] + w8 * jnp.tile(s8, (1, tn // LANES)))
        return tuple(out)

    accs = lax.fori_loop(0, d // SUBLANES, body,
                         tuple(jnp.zeros((SUBLANES, tn), F32) for _ in range(nb)), unroll=8)
    for b in range(nb):
        o_ref[b:b + 1, :] = jnp.sum(accs[b], axis=0, keepdims=True) + b_ref[...]


def _ada(c, w_ada, b_ada, tn=1024):
    nb, d = c.shape
    n = w_ada.shape[1]
    return pl.pallas_call(
        _ada_kernel,
        grid=(n // tn,),
        in_specs=[pl.BlockSpec((d, nb), lambda j: (0, 0)),
                  pl.BlockSpec((d, tn), lambda j: (0, j)),
                  pl.BlockSpec((1, tn), lambda j: (0, j))],
        out_specs=pl.BlockSpec((nb, tn), lambda j: (0, j)),
        out_shape=jax.ShapeDtypeStruct((nb, n), F32),
        scratch_shapes=[pltpu.VMEM((nb, d, LANES), F32)],
        compiler_params=_params(("arbitrary",)),
    )(c.T, w_ada, b_ada.reshape(1, n))


def _normproj_kernel(x_ref, g_ref, sc_ref, sh_ref, *rest, split):
    if split:
        whi_ref, wlo_ref, o_ref, hhi_ref, hlo_ref = rest
    else:
        whi_ref, o_ref, hhi_ref = rest

    @pl.when(pl.program_id(1) == 0)
    def _():
        y = _rms(x_ref[...]) * g_ref[...]
        h = y * (1.0 + sc_ref[0]) + sh_ref[0]
        hi = h.astype(BF16)
        hhi_ref[...] = hi
        if split:
            hlo_ref[...] = (h - hi.astype(F32)).astype(BF16)

    acc = _dot(hhi_ref[...], whi_ref[...])
    if split:
        acc = acc + _dot(hlo_ref[...], whi_ref[...]) + _dot(hhi_ref[...], wlo_ref[...])
    o_ref[...] = acc.astype(o_ref.dtype)


def _normproj(x2, g, scale, shift, ws, seq, tm, tn, out_dtype):
    m, d = x2.shape
    n = ws[0].shape[1]
    split = len(ws) == 2
    per_b = seq // tm
    vec = pl.BlockSpec((1, 1, d), lambda i, j: (i // per_b, 0, 0))
    wspec = pl.BlockSpec((d, tn), lambda i, j: (0, j))
    return pl.pallas_call(
        functools.partial(_normproj_kernel, split=split),
        grid=(m // tm, n // tn),
        in_specs=[pl.BlockSpec((tm, d), lambda i, j: (i, 0)),
                  pl.BlockSpec((1, d), lambda i, j: (0, 0)),
                  vec, vec] + [wspec] * len(ws),
        out_specs=pl.BlockSpec((tm, tn), lambda i, j: (i, j)),
        out_shape=jax.ShapeDtypeStruct((m, n), out_dtype),
        scratch_shapes=[pltpu.VMEM((tm, d), BF16)] * len(ws),
        compiler_params=_params(("parallel", "arbitrary")),
    )(x2, g.reshape(1, d), scale, shift, *ws)


def _rel_bucket(dist):
    n = jnp.maximum(dist, 0)
    max_exact = REL_BUCKETS // 2
    nf = jnp.maximum(n, max_exact).astype(F32)
    large = max_exact + (jnp.log(nf / max_exact) / math.log(REL_MAX_DIST / max_exact)
                         * (REL_BUCKETS - max_exact)).astype(jnp.int32)
    large = jnp.minimum(large, REL_BUCKETS - 1)
    return jnp.where(n < max_exact, n, large)


def _moba_kernel(rb_ref, q_ref, k_ref, v_ref, o_ref,
                 kbf_ref, vt_ref, kmean_ref, town_ref, tprev_ref, mask_ref, m_ref, l_ref, acc_ref):
    h = pl.program_id(1)
    i = pl.program_id(2)
    nblk = vt_ref.shape[0]
    blk = MOBA_BLOCK
    scale = HEAD_DIM ** -0.5
    nt = (((1,), (1,)), ((), ()))
    b_far = rb_ref[REL_BUCKETS - 1, h]

    @pl.when(i == 0)
    def _init():
        kf = k_ref[0]
        kbf_ref[...] = kf.astype(BF16)
        kmean_ref[...] = jnp.mean(kf.reshape(nblk, blk, HEAD_DIM), axis=1)
        for j in range(nblk):
            vt_ref[j] = v_ref[0, j * blk:(j + 1) * blk, :].T.astype(BF16)
        kk = lax.broadcasted_iota(jnp.int32, (blk, blk), 0)
        qq = lax.broadcasted_iota(jnp.int32, (blk, blk), 1)
        for ref, dist in ((town_ref, qq - kk), (tprev_ref, qq - kk + blk)):
            bucket = _rel_bucket(dist)
            tab = jnp.zeros((blk, blk), F32)
            for b in range(REL_BUCKETS):
                tab = jnp.where(bucket == b, rb_ref[b, h], tab)
            ref[...] = jnp.where(dist >= 0, tab, NEG)

    qf = q_ref[0]
    qb = qf.astype(BF16)

    q3 = _split3(qf)
    k3 = _split3(kmean_ref[...])
    gate = jnp.zeros((nblk, blk), F32)
    for a, b in ((0, 0), (0, 1), (1, 0), (1, 1), (0, 2), (2, 0)):
        gate = gate + _dot(k3[a], q3[b], nt)
    nidx = lax.broadcasted_iota(jnp.int32, (nblk, blk), 0)
    past = nidx < i
    for j in range(nblk):
        gj = gate[j:j + 1, :]
        beats = jnp.where(past & ((gate > gj) | ((gate == gj) & (nidx < j))), 1.0, 0.0)
        cnt = jnp.sum(beats, axis=0, keepdims=True)
        chosen = (cnt < MOBA_TOPK) & (j < i)
        mask_ref[j] = jnp.broadcast_to(jnp.where(chosen, 0.0, NEG), (SUBLANES, blk))

    def scores(j):
        start = pl.multiple_of(j * blk, blk)
        kj = kbf_ref[pl.ds(start, blk), :]
        return _dot(kj, qb, nt) * scale

    s = scores(i) + town_ref[...]
    m0 = jnp.max(s, axis=0, keepdims=True)
    p = jnp.exp(s - m0)
    m_ref[...] = m0
    l_ref[...] = jnp.sum(p, axis=0, keepdims=True)
    acc_ref[...] = _dot(vt_ref[i], p.astype(BF16))

    def update(j, s):
        m_old = m_ref[...]
        m_new = jnp.maximum(m_old, jnp.max(s, axis=0, keepdims=True))
        alpha = jnp.exp(m_old - m_new)
        p = jnp.exp(s - m_new)
        m_ref[...] = m_new
        l_ref[...] = alpha * l_ref[...] + jnp.sum(p, axis=0, keepdims=True)
        acc_ref[...] = alpha * acc_ref[...] + _dot(vt_ref[j], p.astype(BF16))

    @pl.when(i >= 1)
    def _prev():
        j = i - 1
        update(j, scores(j) + tprev_ref[...] + mask_ref[j][0:1, :])

    def far(j, carry):
        update(j, scores(j) + (mask_ref[j][0:1, :] + b_far))
        return carry

    lax.fori_loop(0, i - 1, far, 0)

    o_ref[0] = (acc_ref[...] / l_ref[...]).T


def _moba(rel_bias, qk3, rest3):
    bsz, seq, _ = qk3.shape
    nblk = seq // MOBA_BLOCK
    blk = MOBA_BLOCK
    return pl.pallas_call(
        _moba_kernel,
        grid=(bsz, N_HEADS, nblk),
        in_specs=[pl.BlockSpec(memory_space=pltpu.SMEM),
                  pl.BlockSpec((1, blk, HEAD_DIM), lambda b, h, i: (b, i, h)),
                  pl.BlockSpec((1, seq, HEAD_DIM), lambda b, h, i: (b, 0, N_HEADS + h)),
                  pl.BlockSpec((1, seq, HEAD_DIM), lambda b, h, i: (b, 0, h))],
        out_specs=pl.BlockSpec((1, blk, HEAD_DIM), lambda b, h, i: (b, i, h)),
        out_shape=jax.ShapeDtypeStruct((bsz, seq, ATTN_WIDTH), F32),
        scratch_shapes=[pltpu.VMEM((seq, HEAD_DIM), BF16),
                        pltpu.VMEM((nblk, HEAD_DIM, blk), BF16),
                        pltpu.VMEM((nblk, HEAD_DIM), F32),
                        pltpu.VMEM((blk, blk), F32),
                        pltpu.VMEM((blk, blk), F32),
                        pltpu.VMEM((nblk, SUBLANES, blk), F32),
                        pltpu.VMEM((1, blk), F32),
                        pltpu.VMEM((1, blk), F32),
                        pltpu.VMEM((HEAD_DIM, blk), F32)],
        compiler_params=_params(("arbitrary", "arbitrary", "arbitrary")),
    )(rel_bias, qk3, qk3, rest3)


def _ssd_kernel(xs_ref, z_ref, bc_ref, dt_ref, cwx_ref, cwbc_ref, cbx_ref, cbbc_ref,
                dtb_ref, alog_ref, dsk_ref, ng_ref, tri_ref, trit_ref, exp_ref, o_ref,
                xext_ref, bcext_ref, state_ref):
    c = pl.program_id(1)
    t = CHUNK
    pad = SUBLANES

    @pl.when(c == 0)
    def _():
        xext_ref[0:pad, :] = jnp.zeros((pad, SSM_WIDTH), F32)
        bcext_ref[0:pad, :] = jnp.zeros((pad, 2 * N_GROUPS * D_STATE), F32)
        state_ref[...] = jnp.zeros_like(state_ref)

    def conv_silu(ext_ref, src_ref, w_ref, b_ref):
        ext_ref[pad:pad + t, :] = src_ref[0]
        acc = b_ref[...]
        for k in range(SSM_CONV):
            off = pad - (SSM_CONV - 1) + k
            acc = acc + w_ref[k:k + 1, :] * ext_ref[off:off + t, :]
        ext_ref[0:pad, :] = ext_ref[t:t + pad, :]
        return _silu(acc)

    xh = conv_silu(xext_ref, xs_ref, cwx_ref, cbx_ref)
    bc = conv_silu(bcext_ref, bc_ref, cwbc_ref, cbbc_ref)

    dtr = dt_ref[0] + dtb_ref[...]
    dt = jnp.maximum(dtr, 0.0) + jnp.log1p(jnp.exp(-jnp.abs(dtr)))
    adt = dt * (-jnp.exp(alog_ref[...]))
    acs = _dot_exact_lhs(tri_ref[...], adt)
    acs_t = _dot_exact_rhs(adt.T, trit_ref[...])

    stack = jnp.concatenate([dt, jnp.exp(acs), jnp.exp(acs[t - 1:t, :] - acs)], axis=0)
    wide = _dot_exact_rhs(stack, exp_ref[...])
    dt_x, eacs_x, dst_x = wide[0:t], wide[t:2 * t], wide[2 * t:3 * t]

    xdt = xh * dt_x
    xdt_bf = xdt.astype(BF16)
    xdec_bf = (xdt * dst_x).astype(BF16)

    row = lax.broadcasted_iota(jnp.int32, (t, t), 0)
    col = lax.broadcasted_iota(jnp.int32, (t, t), 1)
    tril = row >= col
    lane = lax.broadcasted_iota(jnp.int32, (t, LANES), 1)
    heads_per_group = N_SSM_HEADS // N_GROUPS
    nt = (((1,), (1,)), ((), ()))
    tn = (((0,), (0,)), ((), ()))

    y_parts = []
    for g in range(N_GROUPS):
        bg = bc[:, g * D_STATE:(g + 1) * D_STATE].astype(BF16)
        cg = bc[:, (N_GROUPS + g) * D_STATE:(N_GROUPS + g + 1) * D_STATE].astype(BF16)
        cb = _dot(cg, bg, nt)
        for pair in range(heads_per_group // 2):
            ms = []
            for r in (g * heads_per_group + 2 * pair, g * heads_per_group + 2 * pair + 1):
                seg = acs[:, r:r + 1] - acs_t[r:r + 1, :]
                ms.append((cb * jnp.exp(jnp.where(tril, seg, NEG))).astype(BF16))
            q = g * (heads_per_group // 2) + pair
            y2 = _dot(jnp.concatenate(ms, axis=0), xdt_bf[:, q * LANES:(q + 1) * LANES])
            y_parts.append(jnp.where(lane < SSM_HEAD_DIM, y2[0:t], y2[t:2 * t]))
    y = jnp.concatenate(y_parts, axis=1)

    off_parts = []
    for g in range(N_GROUPS):
        sl = slice(g * GROUP_WIDTH, (g + 1) * GROUP_WIDTH)
        bg = bc[:, g * D_STATE:(g + 1) * D_STATE].astype(BF16)
        cg = bc[:, (N_GROUPS + g) * D_STATE:(N_GROUPS + g + 1) * D_STATE].astype(BF16)
        st = state_ref[:, sl]
        off_parts.append(_dot(cg, st.astype(BF16)))
        state_ref[:, sl] = eacs_x[t - 1:t, sl] * st + _dot(bg, xdec_bf[:, sl], tn)
    y = y + jnp.concatenate(off_parts, axis=1) * eacs_x + dsk_ref[...] * xh
    y = y * _silu(z_ref[0])

    outs = []
    for g in range(N_GROUPS):
        sl = slice(g * GROUP_WIDTH, (g + 1) * GROUP_WIDTH)
        outs.append(_rms(y[:, sl]) * ng_ref[:, sl])
    o_ref[0] = jnp.concatenate(outs, axis=1).astype(o_ref.dtype)


def _ssd(rest3, conv_w, conv_b, dt_bias, a_log, d_skip, norm_g):
    bsz, seq, _ = rest3.shape
    t = CHUNK
    gn2 = 2 * N_GROUPS * D_STATE
    pad_h = LANES - N_SSM_HEADS
    tri = jnp.tril(jnp.ones((t, t), F32)).astype(BF16)
    expand = jnp.pad(jnp.repeat(jnp.eye(N_SSM_HEADS, dtype=F32), SSM_HEAD_DIM, axis=1),
                     ((0, pad_h), (0, 0))).astype(BF16)
    const = lambda shape: pl.BlockSpec(shape, lambda b, c: (0,) * len(shape))
    return pl.pallas_call(
        _ssd_kernel,
        grid=(bsz, seq // t),
        in_specs=[pl.BlockSpec((1, t, SSM_WIDTH), lambda b, c: (b, c, 1)),
                  pl.BlockSpec((1, t, SSM_WIDTH), lambda b, c: (b, c, 2)),
                  pl.BlockSpec((1, t, gn2), lambda b, c: (b, c, 3 * SSM_WIDTH // gn2)),
                  pl.BlockSpec((1, t, LANES), lambda b, c: (b, c, (3 * SSM_WIDTH + gn2) // LANES)),
                  const((SSM_CONV, SSM_WIDTH)), const((SSM_CONV, gn2)),
                  const((1, SSM_WIDTH)), const((1, gn2)),
                  const((1, LANES)), const((1, LANES)),
                  const((1, SSM_WIDTH)), const((1, SSM_WIDTH)),
                  const((t, t)), const((t, t)), const((LANES, SSM_WIDTH))],
        out_specs=pl.BlockSpec((1, t, SSM_WIDTH), lambda b, c: (b, c, 0)),
        out_shape=jax.ShapeDtypeStruct((bsz, seq, SSM_WIDTH), BF16),
        scratch_shapes=[pltpu.VMEM((t + 2 * SUBLANES, SSM_WIDTH), F32),
                        pltpu.VMEM((t + 2 * SUBLANES, gn2), F32),
                        pltpu.VMEM((D_STATE, SSM_WIDTH), F32)],
        compiler_params=_params(("arbitrary", "arbitrary")),
    )(rest3, rest3, rest3, rest3,
      conv_w[:, :SSM_WIDTH], conv_w[:, SSM_WIDTH:],
      conv_b[:SSM_WIDTH].reshape(1, -1), conv_b[SSM_WIDTH:].reshape(1, -1),
      jnp.pad(dt_bias, (0, pad_h)).reshape(1, LANES), jnp.pad(a_log, (0, pad_h)).reshape(1, LANES),
      jnp.repeat(d_skip, SSM_HEAD_DIM).reshape(1, SSM_WIDTH), norm_g.reshape(1, SSM_WIDTH),
      tri, tri.T, expand)


def _outproj_kernel(attn_ref, ssm_ref, x_ref, ag_ref, w_ref, gm_ref, g2_ref, sc_ref, sh_ref,
                    x1_ref, h2_ref):
    a = _rms(attn_ref[...]) * ag_ref[...]
    lhs = jnp.concatenate([a.astype(BF16), ssm_ref[...]], axis=-1)
    x1 = x_ref[...] + gm_ref[0] * _dot(lhs, w_ref[...])
    x1_ref[...] = x1
    y = _rms(x1) * g2_ref[...]
    h2_ref[...] = (y * (1.0 + sc_ref[0]) + sh_ref[0]).astype(BF16)


def _outproj(attn2, ssm2, x2, attn_g, w_out_bf, gate_m, g2, scale_f, shift_f, seq, tm):
    m, d = x2.shape
    per_b = seq // tm
    vec = pl.BlockSpec((1, 1, d), lambda i: (i // per_b, 0, 0))
    half = pl.BlockSpec((tm, ATTN_WIDTH), lambda i: (i, 0))
    full = pl.BlockSpec((tm, d), lambda i: (i, 0))
    return pl.pallas_call(
        _outproj_kernel,
        grid=(m // tm,),
        in_specs=[half, half, full,
                  pl.BlockSpec((1, ATTN_WIDTH), lambda i: (0, 0)),
                  pl.BlockSpec((d, d), lambda i: (0, 0), pipeline_mode=pl.Buffered(1)),
                  vec, pl.BlockSpec((1, d), lambda i: (0, 0)), vec, vec],
        out_specs=[full, full],
        out_shape=[jax.ShapeDtypeStruct((m, d), F32), jax.ShapeDtypeStruct((m, d), BF16)],
        compiler_params=_params(("parallel",)),
    )(attn2, ssm2, x2, attn_g.reshape(1, -1), w_out_bf, gate_m, g2.reshape(1, d), scale_f, shift_f)


def _ffn_up_kernel(h_ref, wg_ref, wv_ref, cwg_ref, cwv_ref, cbg_ref, cbv_ref, o_ref, eg_ref, ev_ref,
                   *, tm, seq):
    i = pl.program_id(1)
    pad = SUBLANES

    @pl.when((i * tm) % seq == 0)
    def _():
        eg_ref[0:pad, :] = jnp.zeros((pad, eg_ref.shape[1]), F32)
        ev_ref[0:pad, :] = jnp.zeros((pad, ev_ref.shape[1]), F32)

    h = h_ref[...]

    def conv(e_ref, w_ref, cw_ref, cb_ref):
        e_ref[pad:pad + tm, :] = _dot(h, w_ref[...])
        acc = cb_ref[...]
        for k in range(FFN_CONV):
            off = pad - (FFN_CONV - 1) + k
            acc = acc + cw_ref[k:k + 1, :] * e_ref[off:off + tm, :]
        e_ref[0:pad, :] = e_ref[tm:tm + pad, :]
        return acc

    g = conv(eg_ref, wg_ref, cwg_ref, cbg_ref)
    v = conv(ev_ref, wv_ref, cwv_ref, cbv_ref)
    o_ref[...] = (_silu(g) * v).astype(o_ref.dtype)


def _ffn_up(h2, w_up_bf, conv_w, conv_b, seq, tm, tn):
    m, d = h2.shape
    nj = FFN_DIM // tn
    cb = conv_b.reshape(1, -1)
    return pl.pallas_call(
        functools.partial(_ffn_up_kernel, tm=tm, seq=seq),
        grid=(nj, m // tm),
        in_specs=[pl.BlockSpec((tm, d), lambda j, i: (i, 0)),
                  pl.BlockSpec((d, tn), lambda j, i: (0, j)),
                  pl.BlockSpec((d, tn), lambda j, i: (0, j + nj)),
                  pl.BlockSpec((FFN_CONV, tn), lambda j, i: (0, j)),
                  pl.BlockSpec((FFN_CONV, tn), lambda j, i: (0, j + nj)),
                  pl.BlockSpec((1, tn), lambda j, i: (0, j)),
                  pl.BlockSpec((1, tn), lambda j, i: (0, j + nj))],
        out_specs=pl.BlockSpec((tm, tn), lambda j, i: (i, j)),
        out_shape=jax.ShapeDtypeStruct((m, FFN_DIM), BF16),
        scratch_shapes=[pltpu.VMEM((tm + 2 * SUBLANES, tn), F32)] * 2,
        compiler_params=_params(("parallel", "arbitrary")),
    )(h2, w_up_bf, w_up_bf, conv_w, conv_w, cb, cb)


def _ffn_down_kernel(a_ref, w_ref, x1_ref, gf_ref, fg_ref, o_ref, acc_ref):
    k = pl.program_id(1)

    @pl.when(k == 0)
    def _():
        acc_ref[...] = jnp.zeros_like(acc_ref)

    acc_ref[...] += _dot(a_ref[...], w_ref[...])

    @pl.when(k == pl.num_programs(1) - 1)
    def _():
        x2 = x1_ref[...] + gf_ref[0] * acc_ref[...]
        o_ref[...] = _rms(x2) * fg_ref[...]


def _ffn_down(act, w_down_bf, x1, gate_f, final_g, seq, tm, tk):
    m, d = x1.shape
    per_b = seq // tm
    return pl.pallas_call(
        _ffn_down_kernel,
        grid=(m // tm, FFN_DIM // tk),
        in_specs=[pl.BlockSpec((tm, tk), lambda i, k: (i, k)),
                  pl.BlockSpec((tk, d), lambda i, k: (k, 0)),
                  pl.BlockSpec((tm, d), lambda i, k: (i, 0)),
                  pl.BlockSpec((1, 1, d), lambda i, k: (i // per_b, 0, 0)),
                  pl.BlockSpec((1, d), lambda i, k: (0, 0))],
        out_specs=pl.BlockSpec((tm, d), lambda i, k: (i, 0)),
        out_shape=jax.ShapeDtypeStruct((m, d), F32),
        scratch_shapes=[pltpu.VMEM((tm, d), F32)],
        compiler_params=_params(("parallel", "arbitrary")),
    )(act, w_down_bf, x1, gate_f, final_g.reshape(1, d))


def kernel(x, c, w_ada, b_ada, norm_mix_g, w_in, rel_bias, attn_norm_g, conv_ssm_w, conv_ssm_b, dt_bias,
           a_log, d_skip, ssm_norm_g, w_out, norm_ffn_g, w_up, conv_ffn_w, conv_ffn_b, w_down, final_norm_g):
    bsz, seq, d = x.shape
    m = bsz * seq
    depth = w_ada.shape[0]
    x2 = x.reshape(m, d)
    for l in range(depth):
        mod = _ada(c, w_ada[l], b_ada[l])
        shift_m, scale_m, gate_m, shift_f, scale_f, gate_f = [
            mod[:, k * d:(k + 1) * d].reshape(bsz, 1, d) for k in range(6)]

        w_qk = w_in[l][:, :QK_COLS]
        w_qk_hi = w_qk.astype(BF16)
        w_qk_lo = (w_qk - w_qk_hi.astype(F32)).astype(BF16)
        w_rest = jnp.pad(w_in[l][:, QK_COLS:].astype(BF16),
                         ((0, 0), (0, REST_COLS - (w_in.shape[2] - QK_COLS))))
        qk = _normproj(x2, norm_mix_g[l], scale_m, shift_m, (w_qk_hi, w_qk_lo), seq, 512, 1024, F32)
        rest = _normproj(x2, norm_mix_g[l], scale_m, shift_m, (w_rest,), seq, 512, 768, F32)
        qk3 = qk.reshape(bsz, seq, QK_COLS)
        rest3 = rest.reshape(bsz, seq, REST_COLS)

        attn = _moba(rel_bias, qk3, rest3)
        ssm = _ssd(rest3, conv_ssm_w[l], conv_ssm_b[l], dt_bias[l], a_log[l], d_skip[l], ssm_norm_g[l])

        x1, h2 = _outproj(attn.reshape(m, ATTN_WIDTH), ssm.reshape(m, SSM_WIDTH), x2, attn_norm_g[l],
                          w_out[l].astype(BF16), gate_m, norm_ffn_g[l], scale_f, shift_f, seq, 512)
        act = _ffn_up(h2, w_up[l].astype(BF16), conv_ffn_w[l], conv_ffn_b[l], seq, 512, 512)
        x2 = _ffn_down(act, w_down[l].astype(BF16), x1, gate_f, final_norm_g, seq, 512, 512)
    return x2.reshape(bsz, seq, d)
```

```python
import functools
import math

import jax
import jax.numpy as jnp
from jax import lax
from jax.experimental import pallas as pl
from jax.experimental.pallas import tpu as pltpu

F32 = jnp.float32
BF16 = jnp.bfloat16

D_MODEL = 2048
ATTN_WIDTH = 1024
HEAD_DIM = 128
N_HEADS = 8
SSM_WIDTH = 1024
SSM_HEAD_DIM = 64
N_SSM_HEADS = 16
N_GROUPS = 2
GROUP_WIDTH = SSM_WIDTH // N_GROUPS
D_STATE = 128
SSM_CONV = 4
CHUNK = 256
MOBA_BLOCK = 256
MOBA_TOPK = 3
MOBA_FAR_GROUP = 4
REL_BUCKETS = 32
REL_MAX_DIST = 128
FFN_DIM = 5632
FFN_CONV = 3
EPS = 1e-6
NEG = -1e30

LANES = 128
SUBLANES = 8
VMEM_LIMIT = 56 * 1024 * 1024

REST_COLS = 3712
QK_COLS = 2 * ATTN_WIDTH


def _params(sem):
    return pltpu.CompilerParams(dimension_semantics=sem, vmem_limit_bytes=VMEM_LIMIT)


def _split3(x):
    hi = x.astype(BF16)
    r = x - hi.astype(F32)
    mid = r.astype(BF16)
    lo = (r - mid.astype(F32)).astype(BF16)
    return hi, mid, lo


def _dot(a, b, dims=(((1,), (0,)), ((), ()))):
    return lax.dot_general(a, b, dims, preferred_element_type=F32)


def _dot_exact_lhs(a_bf, x, dims=(((1,), (0,)), ((), ()))):
    hi, mid, lo = _split3(x)
    return _dot(a_bf, hi, dims) + _dot(a_bf, mid, dims) + _dot(a_bf, lo, dims)


def _dot_exact_rhs(x, b_bf, dims=(((1,), (0,)), ((), ()))):
    hi, mid, lo = _split3(x)
    return _dot(hi, b_bf, dims) + _dot(mid, b_bf, dims) + _dot(lo, b_bf, dims)


def _silu(x):
    return x * jax.nn.sigmoid(x)


def _rms(x):
    return x * lax.rsqrt(jnp.mean(x * x, axis=-1, keepdims=True) + EPS)


def _ada_kernel(ct_ref, w_ref, b_ref, o_ref, sb_ref):
    nb = sb_ref.shape[0]
    d = w_ref.shape[0]
    tn = o_ref.shape[-1]

    @pl.when(pl.program_id(0) == 0)
    def _():
        ct = ct_ref[...]
        st = _silu(ct)
        for b in range(nb):
            sb_ref[b] = jnp.broadcast_to(st[:, b:b + 1], (d, LANES))

    def body(kc, accs):
        r = pl.multiple_of(kc * SUBLANES, SUBLANES)
        w8 = w_ref[pl.ds(r, SUBLANES), :]
        out = []
        for b in range(nb):
            s8 = sb_ref[b, pl.ds(r, SUBLANES), :]
            out.append(accs[b] + w8 * jnp.tile(s8, (1, tn // LANES)))
        return tuple(out)

    accs = lax.fori_loop(0, d // SUBLANES, body,
                         tuple(jnp.zeros((SUBLANES, tn), F32) for _ in range(nb)), unroll=8)
    for b in range(nb):
        o_ref[b:b + 1, :] = jnp.sum(accs[b], axis=0, keepdims=True) + b_ref[...]


def _ada(c, w_ada, b_ada, tn=1024):
    nb, d = c.shape
    n = w_ada.shape[1]
    return pl.pallas_call(
        _ada_kernel,
        grid=(n // tn,),
        in_specs=[pl.BlockSpec((d, nb), lambda j: (0, 0)),
                  pl.BlockSpec((d, tn), lambda j: (0, j)),
                  pl.BlockSpec((1, tn), lambda j: (0, j))],
        out_specs=pl.BlockSpec((nb, tn), lambda j: (0, j)),
        out_shape=jax.ShapeDtypeStruct((nb, n), F32),
        scratch_shapes=[pltpu.VMEM((nb, d, LANES), F32)],
        compiler_params=_params(("arbitrary",)),
    )(c.T, w_ada, b_ada.reshape(1, n))


def _normproj_kernel(x_ref, g_ref, sc_ref, sh_ref, *rest, split, sub):
    if split:
        whi_ref, wlo_ref, o_ref = rest
    else:
        whi_ref, o_ref = rest
    for r0 in range(0, x_ref.shape[0], sub):
        rows = slice(r0, r0 + sub)
        y = _rms(x_ref[rows, :]) * g_ref[...]
        h = y * (1.0 + sc_ref[0]) + sh_ref[0]
        hi = h.astype(BF16)
        acc = _dot(hi, whi_ref[...])
        if split:
            lo = (h - hi.astype(F32)).astype(BF16)
            acc = acc + _dot(lo, whi_ref[...]) + _dot(hi, wlo_ref[...])
        o_ref[rows, :] = acc.astype(o_ref.dtype)


def _resident(shape):
    return pl.BlockSpec(shape, lambda *_: (0,) * len(shape), pipeline_mode=pl.Buffered(1))


def _normproj(x2, g, scale, shift, ws, seq, tm, sub, out_dtype):
    m, d = x2.shape
    n = ws[0].shape[1]
    per_b = seq // tm
    vec = pl.BlockSpec((1, 1, d), lambda i: (i // per_b, 0, 0))
    return pl.pallas_call(
        functools.partial(_normproj_kernel, split=len(ws) == 2, sub=sub),
        grid=(m // tm,),
        in_specs=[pl.BlockSpec((tm, d), lambda i: (i, 0)),
                  pl.BlockSpec((1, d), lambda i: (0, 0)),
                  vec, vec] + [_resident((d, n))] * len(ws),
        out_specs=pl.BlockSpec((tm, n), lambda i: (i, 0)),
        out_shape=jax.ShapeDtypeStruct((m, n), out_dtype),
        compiler_params=_params(("parallel",)),
    )(x2, g.reshape(1, d), scale, shift, *ws)


def _rel_bucket(dist):
    n = jnp.maximum(dist, 0)
    max_exact = REL_BUCKETS // 2
    nf = jnp.maximum(n, max_exact).astype(F32)
    large = max_exact + (jnp.log(nf / max_exact) / math.log(REL_MAX_DIST / max_exact)
                         * (REL_BUCKETS - max_exact)).astype(jnp.int32)
    large = jnp.minimum(large, REL_BUCKETS - 1)
    return jnp.where(n < max_exact, n, large)


def _moba_kernel(rb_ref, q_ref, k_ref, v_ref, o_ref,
                 kbf_ref, vt_ref, kmean_ref, tcat_ref, mask_ref, farmask_ref, m_ref, l_ref, acc_ref):
    h = pl.program_id(1)
    i = pl.program_id(2)
    nblk = vt_ref.shape[0]
    blk = MOBA_BLOCK
    grp = MOBA_FAR_GROUP
    scale = HEAD_DIM ** -0.5
    nt = (((1,), (1,)), ((), ()))

    @pl.when(i == 0)
    def _init():
        kf = k_ref[0]
        kbf_ref[...] = kf.astype(BF16)
        kmean_ref[...] = jnp.mean(kf.reshape(nblk, blk, HEAD_DIM), axis=1)
        for j in range(nblk):
            vt_ref[j] = v_ref[0, j * blk:(j + 1) * blk, :].T.astype(BF16)
        b_far = rb_ref[REL_BUCKETS - 1, h]
        kk = lax.broadcasted_iota(jnp.int32, (blk, blk), 0)
        qq = lax.broadcasted_iota(jnp.int32, (blk, blk), 1)
        for half, dist in ((0, qq - kk + blk), (1, qq - kk)):
            bucket = _rel_bucket(dist)
            tab = jnp.zeros((blk, blk), F32)
            for b in range(REL_BUCKETS):
                tab = jnp.where(bucket == b, rb_ref[b, h] - b_far, tab)
            tcat_ref[half * blk:(half + 1) * blk, :] = jnp.where(dist >= 0, tab, NEG)

    qf = q_ref[0]
    qb = qf.astype(BF16)

    q3 = _split3(qf)
    k3 = _split3(kmean_ref[...])
    gate = jnp.zeros((nblk, blk), F32)
    for a, b in ((0, 0), (0, 1), (1, 0), (1, 1), (0, 2), (2, 0)):
        gate = gate + _dot(k3[a], q3[b], nt)
    nidx = lax.broadcasted_iota(jnp.int32, (nblk, blk), 0)
    nidx_f = nidx.astype(F32)
    avail = jnp.where(nidx < i, 1.0, 0.0)
    chosen = jnp.zeros((nblk, blk), F32)
    for _ in range(MOBA_TOPK):
        gm = jnp.where(avail > 0.0, gate, -jnp.inf)
        best = jnp.max(gm, axis=0, keepdims=True)
        first = jnp.min(jnp.where((gm == best) & (avail > 0.0), nidx_f, float(nblk)), axis=0, keepdims=True)
        hit = nidx_f == first
        chosen = jnp.where(hit, 1.0, chosen)
        avail = jnp.where(hit, 0.0, avail)
    add_all = jnp.where(chosen > 0.0, 0.0, NEG)
    add_far = jnp.where(nidx < i - 1, add_all, NEG)
    for j in range(nblk):
        mask_ref[j] = jnp.broadcast_to(add_all[j:j + 1, :], (SUBLANES, blk))
        farmask_ref[j] = jnp.broadcast_to(add_far[j:j + 1, :], (SUBLANES, blk))

    def pv(p, blocks):
        pb = p.astype(BF16)
        out = None
        for n, j in enumerate(blocks):
            d = _dot(vt_ref[j], pb[n * blk:(n + 1) * blk, :])
            out = d if out is None else out + d
        return out

    def first_unit(s, blocks):
        m0 = jnp.max(s, axis=0, keepdims=True)
        p = jnp.exp(s - m0)
        m_ref[...] = m0
        l_ref[...] = jnp.sum(p, axis=0, keepdims=True)
        acc_ref[...] = pv(p, blocks)

    def update(s, blocks):
        m_old = m_ref[...]
        m_new = jnp.maximum(m_old, jnp.max(s, axis=0, keepdims=True))
        alpha = jnp.exp(m_old - m_new)
        p = jnp.exp(s - m_new)
        m_ref[...] = m_new
        l_ref[...] = alpha * l_ref[...] + jnp.sum(p, axis=0, keepdims=True)
        acc_ref[...] = alpha * acc_ref[...] + pv(p, blocks)

    @pl.when(i == 0)
    def _():
        first_unit(_dot(kbf_ref[0:blk, :], qb, nt) * scale + tcat_ref[blk:2 * blk, :], [0])

    @pl.when(i > 0)
    def _():
        start = pl.multiple_of((i - 1) * blk, blk)
        s = _dot(kbf_ref[pl.ds(start, 2 * blk), :], qb, nt) * scale + tcat_ref[...]
        s = jnp.concatenate([s[0:blk] + mask_ref[i - 1][0:1, :], s[blk:2 * blk]], axis=0)
        first_unit(s, [i - 1, i])

    def far(t, carry):
        start = pl.multiple_of(t * (grp * blk), grp * blk)
        s = _dot(kbf_ref[pl.ds(start, grp * blk), :], qb, nt) * scale
        s = jnp.concatenate([s[n * blk:(n + 1) * blk] + farmask_ref[t * grp + n][0:1, :] for n in range(grp)],
                            axis=0)
        update(s, [t * grp + n for n in range(grp)])
        return carry

    lax.fori_loop(0, (i + grp - 2) // grp, far, 0)

    o_ref[0] = (acc_ref[...] / l_ref[...]).T


def _moba(rel_bias, qk3, rest3):
    bsz, seq, _ = qk3.shape
    nblk = seq // MOBA_BLOCK
    blk = MOBA_BLOCK
    assert nblk % MOBA_FAR_GROUP == 0
    return pl.pallas_call(
        _moba_kernel,
        grid=(bsz, N_HEADS, nblk),
        in_specs=[pl.BlockSpec(memory_space=pltpu.SMEM),
                  pl.BlockSpec((1, blk, HEAD_DIM), lambda b, h, i: (b, i, h)),
                  pl.BlockSpec((1, seq, HEAD_DIM), lambda b, h, i: (b, 0, N_HEADS + h)),
                  pl.BlockSpec((1, seq, HEAD_DIM), lambda b, h, i: (b, 0, h))],
        out_specs=pl.BlockSpec((1, blk, HEAD_DIM), lambda b, h, i: (b, i, h)),
        out_shape=jax.ShapeDtypeStruct((bsz, seq, ATTN_WIDTH), F32),
        scratch_shapes=[pltpu.VMEM((seq, HEAD_DIM), BF16),
                        pltpu.VMEM((nblk, HEAD_DIM, blk), BF16),
                        pltpu.VMEM((nblk, HEAD_DIM), F32),
                        pltpu.VMEM((2 * blk, blk), F32),
                        pltpu.VMEM((nblk, SUBLANES, blk), F32),
                        pltpu.VMEM((nblk, SUBLANES, blk), F32),
                        pltpu.VMEM((1, blk), F32),
                        pltpu.VMEM((1, blk), F32),
                        pltpu.VMEM((HEAD_DIM, blk), F32)],
        compiler_params=_params(("arbitrary", "arbitrary", "arbitrary")),
    )(rel_bias, qk3, qk3, rest3)


def _ssd_kernel(xs_ref, z_ref, bc_ref, dt_ref, cwx_ref, cwbc_ref, cbx_ref, cbbc_ref,
                dtb_ref, alog_ref, dsk_ref, ng_ref, tri_ref, trit_ref, exp_ref, o_ref,
                xext_ref, bcext_ref, state_ref):
    c = pl.program_id(1)
    t = CHUNK
    pad = SUBLANES

    @pl.when(c == 0)
    def _():
        xext_ref[0:pad, :] = jnp.zeros((pad, SSM_WIDTH), F32)
        bcext_ref[0:pad, :] = jnp.zeros((pad, 2 * N_GROUPS * D_STATE), F32)
        state_ref[...] = jnp.zeros_like(state_ref)

    def conv_silu(ext_ref, src_ref, w_ref, b_ref):
        ext_ref[pad:pad + t, :] = src_ref[0]
        acc = b_ref[...]
        for k in range(SSM_CONV):
            off = pad - (SSM_CONV - 1) + k
            acc = acc + w_ref[k:k + 1, :] * ext_ref[off:off + t, :]
        ext_ref[0:pad, :] = ext_ref[t:t + pad, :]
        return _silu(acc)

    xh = conv_silu(xext_ref, xs_ref, cwx_ref, cbx_ref)
    bc = conv_silu(bcext_ref, bc_ref, cwbc_ref, cbbc_ref)

    dtr = dt_ref[0] + dtb_ref[...]
    dt = jnp.maximum(dtr, 0.0) + jnp.log1p(jnp.exp(-jnp.abs(dtr)))
    adt = dt * (-jnp.exp(alog_ref[...]))
    acs = _dot_exact_lhs(tri_ref[...], adt)
    acs_t = _dot_exact_rhs(adt.T, trit_ref[...])

    stack = jnp.concatenate([dt, jnp.exp(acs), jnp.exp(acs[t - 1:t, :] - acs)], axis=0)
    wide = _dot_exact_rhs(stack, exp_ref[...])
    dt_x, eacs_x, dst_x = wide[0:t], wide[t:2 * t], wide[2 * t:3 * t]

    xdt = xh * dt_x
    xdt_bf = xdt.astype(BF16)
    xdec_bf = (xdt * dst_x).astype(BF16)

    row = lax.broadcasted_iota(jnp.int32, (t, t), 0)
    col = lax.broadcasted_iota(jnp.int32, (t, t), 1)
    tril = row >= col
    lane = lax.broadcasted_iota(jnp.int32, (t, LANES), 1)
    heads_per_group = N_SSM_HEADS // N_GROUPS
    nt = (((1,), (1,)), ((), ()))
    tn = (((0,), (0,)), ((), ()))

    y_parts = []
    for g in range(N_GROUPS):
        bg = bc[:, g * D_STATE:(g + 1) * D_STATE].astype(BF16)
        cg = bc[:, (N_GROUPS + g) * D_STATE:(N_GROUPS + g + 1) * D_STATE].astype(BF16)
        cb = _dot(cg, bg, nt)
        for pair in range(heads_per_group // 2):
            ms = []
            for r in (g * heads_per_group + 2 * pair, g * heads_per_group + 2 * pair + 1):
                seg = acs[:, r:r + 1] - acs_t[r:r + 1, :]
                ms.append((cb * jnp.exp(jnp.where(tril, seg, NEG))).astype(BF16))
            q = g * (heads_per_group // 2) + pair
            y2 = _dot(jnp.concatenate(ms, axis=0), xdt_bf[:, q * LANES:(q + 1) * LANES])
            y_parts.append(jnp.where(lane < SSM_HEAD_DIM, y2[0:t], y2[t:2 * t]))
    y = jnp.concatenate(y_parts, axis=1)

    off_parts = []
    for g in range(N_GROUPS):
        sl = slice(g * GROUP_WIDTH, (g + 1) * GROUP_WIDTH)
        bg = bc[:, g * D_STATE:(g + 1) * D_STATE].astype(BF16)
        cg = bc[:, (N_GROUPS + g) * D_STATE:(N_GROUPS + g + 1) * D_STATE].astype(BF16)
        st = state_ref[:, sl]
        off_parts.append(_dot(cg, st.astype(BF16)))
        state_ref[:, sl] = eacs_x[t - 1:t, sl] * st + _dot(bg, xdec_bf[:, sl], tn)
    y = y + jnp.concatenate(off_parts, axis=1) * eacs_x + dsk_ref[...] * xh
    y = y * _silu(z_ref[0])

    outs = []
    for g in range(N_GROUPS):
        sl = slice(g * GROUP_WIDTH, (g + 1) * GROUP_WIDTH)
        outs.append(_rms(y[:, sl]) * ng_ref[:, sl])
    o_ref[0] = jnp.concatenate(outs, axis=1).astype(o_ref.dtype)


def _ssd(rest3, conv_w, conv_b, dt_bias, a_log, d_skip, norm_g):
    bsz, seq, _ = rest3.shape
    t = CHUNK
    gn2 = 2 * N_GROUPS * D_STATE
    pad_h = LANES - N_SSM_HEADS
    tri = jnp.tril(jnp.ones((t, t), F32)).astype(BF16)
    expand = jnp.pad(jnp.repeat(jnp.eye(N_SSM_HEADS, dtype=F32), SSM_HEAD_DIM, axis=1),
                     ((0, pad_h), (0, 0))).astype(BF16)
    const = lambda shape: pl.BlockSpec(shape, lambda b, c: (0,) * len(shape))
    return pl.pallas_call(
        _ssd_kernel,
        grid=(bsz, seq // t),
        in_specs=[pl.BlockSpec((1, t, SSM_WIDTH), lambda b, c: (b, c, 1)),
                  pl.BlockSpec((1, t, SSM_WIDTH), lambda b, c: (b, c, 2)),
                  pl.BlockSpec((1, t, gn2), lambda b, c: (b, c, 3 * SSM_WIDTH // gn2)),
                  pl.BlockSpec((1, t, LANES), lambda b, c: (b, c, (3 * SSM_WIDTH + gn2) // LANES)),
                  const((SSM_CONV, SSM_WIDTH)), const((SSM_CONV, gn2)),
                  const((1, SSM_WIDTH)), const((1, gn2)),
                  const((1, LANES)), const((1, LANES)),
                  const((1, SSM_WIDTH)), const((1, SSM_WIDTH)),
                  const((t, t)), const((t, t)), const((LANES, SSM_WIDTH))],
        out_specs=pl.BlockSpec((1, t, SSM_WIDTH), lambda b, c: (b, c, 0)),
        out_shape=jax.ShapeDtypeStruct((bsz, seq, SSM_WIDTH), BF16),
        scratch_shapes=[pltpu.VMEM((t + 2 * SUBLANES, SSM_WIDTH), F32),
                        pltpu.VMEM((t + 2 * SUBLANES, gn2), F32),
                        pltpu.VMEM((D_STATE, SSM_WIDTH), F32)],
        compiler_params=_params(("arbitrary", "arbitrary")),
    )(rest3, rest3, rest3, rest3,
      conv_w[:, :SSM_WIDTH], conv_w[:, SSM_WIDTH:],
      conv_b[:SSM_WIDTH].reshape(1, -1), conv_b[SSM_WIDTH:].reshape(1, -1),
      jnp.pad(dt_bias, (0, pad_h)).reshape(1, LANES), jnp.pad(a_log, (0, pad_h)).reshape(1, LANES),
      jnp.repeat(d_skip, SSM_HEAD_DIM).reshape(1, SSM_WIDTH), norm_g.reshape(1, SSM_WIDTH),
      tri, tri.T, expand)


def _outproj_kernel(attn_ref, ssm_ref, x_ref, ag_ref, w_ref, gm_ref, g2_ref, sc_ref, sh_ref,
                    x1_ref, h2_ref, *, sub):
    for r0 in range(0, x_ref.shape[0], sub):
        rows = slice(r0, r0 + sub)
        a = _rms(attn_ref[rows, :]) * ag_ref[...]
        lhs = jnp.concatenate([a.astype(BF16), ssm_ref[rows, :]], axis=-1)
        x1 = x_ref[rows, :] + gm_ref[0] * _dot(lhs, w_ref[...])
        x1_ref[rows, :] = x1
        y = _rms(x1) * g2_ref[...]
        h2_ref[rows, :] = (y * (1.0 + sc_ref[0]) + sh_ref[0]).astype(BF16)


def _outproj(attn2, ssm2, x2, attn_g, w_out_bf, gate_m, g2, scale_f, shift_f, seq, tm, sub):
    m, d = x2.shape
    per_b = seq // tm
    vec = pl.BlockSpec((1, 1, d), lambda i: (i // per_b, 0, 0))
    half = pl.BlockSpec((tm, ATTN_WIDTH), lambda i: (i, 0))
    full = pl.BlockSpec((tm, d), lambda i: (i, 0))
    return pl.pallas_call(
        functools.partial(_outproj_kernel, sub=sub),
        grid=(m // tm,),
        in_specs=[half, half, full,
                  pl.BlockSpec((1, ATTN_WIDTH), lambda i: (0, 0)),
                  _resident((d, d)),
                  vec, pl.BlockSpec((1, d), lambda i: (0, 0)), vec, vec],
        out_specs=[full, full],
        out_shape=[jax.ShapeDtypeStruct((m, d), F32), jax.ShapeDtypeStruct((m, d), BF16)],
        compiler_params=_params(("parallel",)),
    )(attn2, ssm2, x2, attn_g.reshape(1, -1), w_out_bf, gate_m, g2.reshape(1, d), scale_f, shift_f)


def _ffn_up_kernel(h_ref, wg_ref, wv_ref, cwg_ref, cwv_ref, cbg_ref, cbv_ref, o_ref, eg_ref, ev_ref,
                   *, tm, sub, seq):
    i = pl.program_id(1)
    pad = SUBLANES

    @pl.when((i * tm) % seq == 0)
    def _():
        eg_ref[0:pad, :] = jnp.zeros((pad, eg_ref.shape[1]), F32)
        ev_ref[0:pad, :] = jnp.zeros((pad, ev_ref.shape[1]), F32)

    def conv(e_ref, h, r0, w_ref, cw_ref, cb_ref):
        e_ref[pad + r0:pad + r0 + sub, :] = _dot(h, w_ref[...])
        acc = cb_ref[...]
        for k in range(FFN_CONV):
            off = pad + r0 - (FFN_CONV - 1) + k
            acc = acc + cw_ref[k:k + 1, :] * e_ref[off:off + sub, :]
        return acc

    for r0 in range(0, tm, sub):
        h = h_ref[r0:r0 + sub, :]
        g = conv(eg_ref, h, r0, wg_ref, cwg_ref, cbg_ref)
        v = conv(ev_ref, h, r0, wv_ref, cwv_ref, cbv_ref)
        o_ref[r0:r0 + sub, :] = (_silu(g) * v).astype(o_ref.dtype)
    eg_ref[0:pad, :] = eg_ref[tm:tm + pad, :]
    ev_ref[0:pad, :] = ev_ref[tm:tm + pad, :]


def _ffn_up(h2, w_up_bf, conv_w, conv_b, seq, tm, sub, tn):
    m, d = h2.shape
    nj = FFN_DIM // tn
    cb = conv_b.reshape(1, -1)
    return pl.pallas_call(
        functools.partial(_ffn_up_kernel, tm=tm, sub=sub, seq=seq),
        grid=(nj, m // tm),
        in_specs=[pl.BlockSpec((tm, d), lambda j, i: (i, 0)),
                  pl.BlockSpec((d, tn), lambda j, i: (0, j)),
                  pl.BlockSpec((d, tn), lambda j, i: (0, j + nj)),
                  pl.BlockSpec((FFN_CONV, tn), lambda j, i: (0, j)),
                  pl.BlockSpec((FFN_CONV, tn), lambda j, i: (0, j + nj)),
                  pl.BlockSpec((1, tn), lambda j, i: (0, j)),
                  pl.BlockSpec((1, tn), lambda j, i: (0, j + nj))],
        out_specs=pl.BlockSpec((tm, tn), lambda j, i: (i, j)),
        out_shape=jax.ShapeDtypeStruct((m, FFN_DIM), BF16),
        scratch_shapes=[pltpu.VMEM((tm + 2 * SUBLANES, tn), F32)] * 2,
        compiler_params=_params(("parallel", "arbitrary")),
    )(h2, w_up_bf, w_up_bf, conv_w, conv_w, cb, cb)


def _ffn_down_kernel(a_ref, w_ref, x1_ref, gf_ref, fg_ref, o_ref):
    x2 = x1_ref[...] + gf_ref[0] * _dot(a_ref[...], w_ref[...])
    o_ref[...] = _rms(x2) * fg_ref[...]


def _ffn_down(act, w_down_bf, x1, gate_f, final_g, seq, tm):
    m, d = x1.shape
    per_b = seq // tm
    row = lambda width: pl.BlockSpec((tm, width), lambda i: (i, 0))
    return pl.pallas_call(
        _ffn_down_kernel,
        grid=(m // tm,),
        in_specs=[row(FFN_DIM), _resident((FFN_DIM, d)), row(d),
                  pl.BlockSpec((1, 1, d), lambda i: (i // per_b, 0, 0)),
                  pl.BlockSpec((1, d), lambda i: (0, 0))],
        out_specs=row(d),
        out_shape=jax.ShapeDtypeStruct((m, d), F32),
        compiler_params=_params(("parallel",)),
    )(act, w_down_bf, x1, gate_f, final_g.reshape(1, d))


def kernel(x, c, w_ada, b_ada, norm_mix_g, w_in, rel_bias, attn_norm_g, conv_ssm_w, conv_ssm_b, dt_bias,
           a_log, d_skip, ssm_norm_g, w_out, norm_ffn_g, w_up, conv_ffn_w, conv_ffn_b, w_down, final_norm_g):
    bsz, seq, d = x.shape
    m = bsz * seq
    assert w_ada.shape[0] == 1, "the final RMSNorm is fused into the (single) layer's ffn_down kernel"
    l = 0
    x2 = x.reshape(m, d)
    mod = _ada(c, w_ada[l], b_ada[l])
    shift_m, scale_m, gate_m, shift_f, scale_f, gate_f = [
        mod[:, k * d:(k + 1) * d].reshape(bsz, 1, d) for k in range(6)]

    w_qk = w_in[l][:, :QK_COLS]
    w_qk_hi = w_qk.astype(BF16)
    w_qk_lo = (w_qk - w_qk_hi.astype(F32)).astype(BF16)
    w_rest = jnp.pad(w_in[l][:, QK_COLS:].astype(BF16), ((0, 0), (0, REST_COLS - (w_in.shape[2] - QK_COLS))))
    qk = _normproj(x2, norm_mix_g[l], scale_m, shift_m, (w_qk_hi, w_qk_lo), seq, 512, 256, F32)
    rest = _normproj(x2, norm_mix_g[l], scale_m, shift_m, (w_rest,), seq, 512, 256, F32)
    qk3 = qk.reshape(bsz, seq, QK_COLS)
    rest3 = rest.reshape(bsz, seq, REST_COLS)

    attn = _moba(rel_bias, qk3, rest3)
    ssm = _ssd(rest3, conv_ssm_w[l], conv_ssm_b[l], dt_bias[l], a_log[l], d_skip[l], ssm_norm_g[l])

    x1, h2 = _outproj(attn.reshape(m, ATTN_WIDTH), ssm.reshape(m, SSM_WIDTH), x2, attn_norm_g[l],
                      w_out[l].astype(BF16), gate_m, norm_ffn_g[l], scale_f, shift_f, seq, 512, 256)
    act = _ffn_up(h2, w_up[l].astype(BF16), conv_ffn_w[l], conv_ffn_b[l], seq, 1024, 256, 512)
    out = _ffn_down(act, w_down[l].astype(BF16), x1, gate_f, final_norm_g, seq, 256)
    return out.reshape(bsz, seq, d)
```

```python
import functools
import math

import jax
import jax.numpy as jnp
from jax import lax
from jax.experimental import pallas as pl
from jax.experimental.pallas import tpu as pltpu

F32 = jnp.float32
BF16 = jnp.bfloat16

D_MODEL = 2048
ATTN_WIDTH = 1024
HEAD_DIM = 128
N_HEADS = 8
SSM_WIDTH = 1024
SSM_HEAD_DIM = 64
N_SSM_HEADS = 16
N_GROUPS = 2
GROUP_WIDTH = SSM_WIDTH // N_GROUPS
D_STATE = 128
SSM_CONV = 4
CHUNK = 256
MOBA_BLOCK = 256
MOBA_TOPK = 3
MOBA_FAR_GROUP = 4
REL_BUCKETS = 32
REL_MAX_DIST = 128
FFN_DIM = 5632
FFN_CONV = 3
EPS = 1e-6
NEG = -1e30

LANES = 128
SUBLANES = 8
VMEM_LIMIT = 56 * 1024 * 1024

REST_COLS = 3712
QK_COLS = 2 * ATTN_WIDTH


def _params(sem):
    return pltpu.CompilerParams(dimension_semantics=sem, vmem_limit_bytes=VMEM_LIMIT)


def _split3(x):
    hi = x.astype(BF16)
    r = x - hi.astype(F32)
    mid = r.astype(BF16)
    lo = (r - mid.astype(F32)).astype(BF16)
    return hi, mid, lo


def _dot(a, b, dims=(((1,), (0,)), ((), ()))):
    return lax.dot_general(a, b, dims, preferred_element_type=F32)


def _dot_exact_lhs(a_bf, x, dims=(((1,), (0,)), ((), ()))):
    hi, mid, lo = _split3(x)
    return _dot(a_bf, hi, dims) + _dot(a_bf, mid, dims) + _dot(a_bf, lo, dims)


def _dot_exact_rhs(x, b_bf, dims=(((1,), (0,)), ((), ()))):
    hi, mid, lo = _split3(x)
    return _dot(hi, b_bf, dims) + _dot(mid, b_bf, dims) + _dot(lo, b_bf, dims)


def _silu(x):
    return x * jax.nn.sigmoid(x)


def _rms(x):
    return x * lax.rsqrt(jnp.mean(x * x, axis=-1, keepdims=True) + EPS)


def _ada_kernel(ct_ref, w_ref, b_ref, o_ref, sb_ref):
    nb = sb_ref.shape[0]
    d = w_ref.shape[0]
    tn = o_ref.shape[-1]

    @pl.when(pl.program_id(0) == 0)
    def _():
        ct = ct_ref[...]
        st = _silu(ct)
        for b in range(nb):
            sb_ref[b] = jnp.broadcast_to(st[:, b:b + 1], (d, LANES))

    def body(kc, accs):
        r = pl.multiple_of(kc * SUBLANES, SUBLANES)
        w8 = w_ref[pl.ds(r, SUBLANES), :]
        out = []
        for b in range(nb):
            s8 = sb_ref[b, pl.ds(r, SUBLANES), :]
            out.append(accs[b] + w8 * jnp.tile(s8, (1, tn // LANES)))
        return tuple(out)

    accs = lax.fori_loop(0, d // SUBLANES, body,
                         tuple(jnp.zeros((SUBLANES, tn), F32) for _ in range(nb)), unroll=8)
    for b in range(nb):
        o_ref[b:b + 1, :] = jnp.sum(accs[b], axis=0, keepdims=True) + b_ref[...]


def _ada(c, w_ada, b_ada, tn=1024):
    nb, d = c.shape
    n = w_ada.shape[1]
    return pl.pallas_call(
        _ada_kernel,
        grid=(n // tn,),
        in_specs=[pl.BlockSpec((d, nb), lambda j: (0, 0)),
                  pl.BlockSpec((d, tn), lambda j: (0, j)),
                  pl.BlockSpec((1, tn), lambda j: (0, j))],
        out_specs=pl.BlockSpec((nb, tn), lambda j: (0, j)),
        out_shape=jax.ShapeDtypeStruct((nb, n), F32),
        scratch_shapes=[pltpu.VMEM((nb, d, LANES), F32)],
        compiler_params=_params(("arbitrary",)),
    )(c.T, w_ada, b_ada.reshape(1, n))


def _normproj_kernel(x_ref, g_ref, sc_ref, sh_ref, *rest, split, sub):
    if split:
        whi_ref, wlo_ref, o_ref = rest
    else:
        whi_ref, o_ref = rest
    for r0 in range(0, x_ref.shape[0], sub):
        rows = slice(r0, r0 + sub)
        y = _rms(x_ref[rows, :]) * g_ref[...]
        h = y * (1.0 + sc_ref[0]) + sh_ref[0]
        hi = h.astype(BF16)
        acc = _dot(hi, whi_ref[...])
        if split:
            lo = (h - hi.astype(F32)).astype(BF16)
            acc = acc + _dot(lo, whi_ref[...]) + _dot(hi, wlo_ref[...])
        o_ref[rows, :] = acc.astype(o_ref.dtype)


def _resident(shape):
    return pl.BlockSpec(shape, lambda *_: (0,) * len(shape), pipeline_mode=pl.Buffered(1))


def _normproj(x2, g, scale, shift, ws, seq, tm, sub, out_dtype):
    m, d = x2.shape
    n = ws[0].shape[1]
    per_b = seq // tm
    vec = pl.BlockSpec((1, 1, d), lambda i: (i // per_b, 0, 0))
    return pl.pallas_call(
        functools.partial(_normproj_kernel, split=len(ws) == 2, sub=sub),
        grid=(m // tm,),
        in_specs=[pl.BlockSpec((tm, d), lambda i: (i, 0)),
                  pl.BlockSpec((1, d), lambda i: (0, 0)),
                  vec, vec] + [_resident((d, n))] * len(ws),
        out_specs=pl.BlockSpec((tm, n), lambda i: (i, 0)),
        out_shape=jax.ShapeDtypeStruct((m, n), out_dtype),
        compiler_params=_params(("parallel",)),
    )(x2, g.reshape(1, d), scale, shift, *ws)


def _rel_bucket(dist):
    n = jnp.maximum(dist, 0)
    max_exact = REL_BUCKETS // 2
    nf = jnp.maximum(n, max_exact).astype(F32)
    large = max_exact + (jnp.log(nf / max_exact) / math.log(REL_MAX_DIST / max_exact)
                         * (REL_BUCKETS - max_exact)).astype(jnp.int32)
    large = jnp.minimum(large, REL_BUCKETS - 1)
    return jnp.where(n < max_exact, n, large)


def _moba_kernel(rb_ref, q_ref, k_ref, v_ref, o_ref,
                 kbf_ref, vt_ref, tcat_ref, mask_ref, farmask_ref, s_buf, p_buf, alpha_buf,
                 m_ref, l_ref, acc_ref):
    h = pl.program_id(1)
    i = pl.program_id(2)
    nblk = vt_ref.shape[0]
    seq = k_ref.shape[1]
    blk = MOBA_BLOCK
    grp = MOBA_FAR_GROUP
    scale = HEAD_DIM ** -0.5
    nt = (((1,), (1,)), ((), ()))

    @pl.when(i == 0)
    def _init():
        kf = k_ref[0]
        kbf_ref[...] = kf.astype(BF16)
        kmean = jnp.mean(kf.reshape(nblk, blk, HEAD_DIM), axis=1)
        for j in range(nblk):
            vt_ref[j] = v_ref[0, j * blk:(j + 1) * blk, :].T.astype(BF16)
        b_far = rb_ref[REL_BUCKETS - 1, h]
        kk = lax.broadcasted_iota(jnp.int32, (blk, blk), 0)
        qq = lax.broadcasted_iota(jnp.int32, (blk, blk), 1)
        for half, dist in ((0, qq - kk + blk), (1, qq - kk)):
            bucket = _rel_bucket(dist)
            tab = jnp.zeros((blk, blk), F32)
            for b in range(REL_BUCKETS):
                tab = jnp.where(bucket == b, rb_ref[b, h] - b_far, tab)
            tcat_ref[half * blk:(half + 1) * blk, :] = jnp.where(dist >= 0, tab, NEG)

        q3 = _split3(q_ref[0])
        k3 = _split3(kmean)
        gate = jnp.zeros((nblk, seq), F32)
        for a, b in ((0, 0), (0, 1), (1, 0), (1, 1), (0, 2), (2, 0)):
            gate = gate + _dot(k3[a], q3[b], nt)
        nidx = lax.broadcasted_iota(jnp.int32, (nblk, seq), 0)
        qblk = lax.broadcasted_iota(jnp.int32, (nblk, seq), 1) // blk
        nidx_f = nidx.astype(F32)
        avail = jnp.where(nidx < qblk, 1.0, 0.0)
        chosen = jnp.zeros((nblk, seq), F32)
        for _ in range(MOBA_TOPK):
            gm = jnp.where(avail > 0.0, gate, -jnp.inf)
            best = jnp.max(gm, axis=0, keepdims=True)
            first = jnp.min(jnp.where((gm == best) & (avail > 0.0), nidx_f, float(nblk)), axis=0, keepdims=True)
            hit = nidx_f == first
            chosen = jnp.where(hit, 1.0, chosen)
            avail = jnp.where(hit, 0.0, avail)
        add_all = jnp.where(chosen > 0.0, 0.0, NEG)
        add_far = jnp.where(nidx < qblk - 1, add_all, NEG)
        for qi in range(nblk):
            cols = slice(qi * blk, (qi + 1) * blk)
            for j in range(nblk):
                mask_ref[qi * nblk + j] = jnp.broadcast_to(add_all[j:j + 1, cols], (SUBLANES, blk))
                farmask_ref[qi * nblk + j] = jnp.broadcast_to(add_far[j:j + 1, cols], (SUBLANES, blk))

    qb = q_ref[0, pl.ds(pl.multiple_of(i * blk, blk), blk), :].astype(BF16)

    def pv(pb, blocks):
        out = None
        for n, j in enumerate(blocks):
            d = _dot(vt_ref[j], pb[n * blk:(n + 1) * blk, :])
            out = d if out is None else out + d
        return out

    def first_unit(s, blocks):
        m0 = jnp.max(s, axis=0, keepdims=True)
        p = jnp.exp(s - m0)
        m_ref[...] = m0
        l_ref[...] = jnp.sum(p, axis=0, keepdims=True)
        acc_ref[...] = pv(p.astype(BF16), blocks)

    def far_scores(slot, t):
        start = pl.multiple_of(t * (grp * blk), grp * blk)
        s = _dot(kbf_ref[pl.ds(start, grp * blk), :], qb, nt) * scale
        s_buf[slot] = jnp.concatenate(
            [s[n * blk:(n + 1) * blk] + farmask_ref[i * nblk + t * grp + n][0:1, :] for n in range(grp)], axis=0)

    def far_softmax(slot):
        s = s_buf[slot]
        m_old = m_ref[...]
        m_new = jnp.maximum(m_old, jnp.max(s, axis=0, keepdims=True))
        alpha = jnp.exp(m_old - m_new)
        p = jnp.exp(s - m_new)
        m_ref[...] = m_new
        l_ref[...] = alpha * l_ref[...] + jnp.sum(p, axis=0, keepdims=True)
        alpha_buf[slot] = alpha
        p_buf[slot] = p.astype(BF16)

    def far_values(slot, t):
        acc_ref[...] = alpha_buf[slot] * acc_ref[...] + pv(p_buf[slot], [t * grp + n for n in range(grp)])

    @pl.when(i == 0)
    def _():
        first_unit(_dot(kbf_ref[0:blk, :], qb, nt) * scale + tcat_ref[blk:2 * blk, :], [0])

    @pl.when(i > 0)
    def _():
        start = pl.multiple_of((i - 1) * blk, blk)
        s = _dot(kbf_ref[pl.ds(start, 2 * blk), :], qb, nt) * scale + tcat_ref[...]
        s = jnp.concatenate([s[0:blk] + mask_ref[i * nblk + i - 1][0:1, :], s[blk:2 * blk]], axis=0)
        first_unit(s, [i - 1, i])
        far_scores(0, 0)
        alpha_buf[1] = jnp.ones((1, blk), F32)
        p_buf[1] = jnp.zeros((grp * blk, blk), BF16)

    ngroups = (i + grp - 2) // grp

    def trip(t, carry):
        slot = t % 2
        far_values(1 - slot, jnp.maximum(t - 1, 0))
        far_softmax(slot)
        far_scores(1 - slot, jnp.minimum(t + 1, nblk // grp - 1))
        return carry

    lax.fori_loop(0, ngroups, trip, 0)

    @pl.when(ngroups > 0)
    def _():
        far_values((ngroups - 1) % 2, ngroups - 1)

    o_ref[0] = (acc_ref[...] / l_ref[...]).T


def _moba(rel_bias, qk3, rest3):
    bsz, seq, _ = qk3.shape
    nblk = seq // MOBA_BLOCK
    blk = MOBA_BLOCK
    assert nblk % MOBA_FAR_GROUP == 0
    return pl.pallas_call(
        _moba_kernel,
        grid=(bsz, N_HEADS, nblk),
        in_specs=[pl.BlockSpec(memory_space=pltpu.SMEM),
                  pl.BlockSpec((1, seq, HEAD_DIM), lambda b, h, i: (b, 0, h)),
                  pl.BlockSpec((1, seq, HEAD_DIM), lambda b, h, i: (b, 0, N_HEADS + h)),
                  pl.BlockSpec((1, seq, HEAD_DIM), lambda b, h, i: (b, 0, h))],
        out_specs=pl.BlockSpec((1, blk, HEAD_DIM), lambda b, h, i: (b, i, h)),
        out_shape=jax.ShapeDtypeStruct((bsz, seq, ATTN_WIDTH), F32),
        scratch_shapes=[pltpu.VMEM((seq, HEAD_DIM), BF16),
                        pltpu.VMEM((nblk, HEAD_DIM, blk), BF16),
                        pltpu.VMEM((2 * blk, blk), F32),
                        pltpu.VMEM((nblk * nblk, SUBLANES, blk), F32),
                        pltpu.VMEM((nblk * nblk, SUBLANES, blk), F32),
                        pltpu.VMEM((2, MOBA_FAR_GROUP * blk, blk), F32),
                        pltpu.VMEM((2, MOBA_FAR_GROUP * blk, blk), BF16),
                        pltpu.VMEM((2, 1, blk), F32),
                        pltpu.VMEM((1, blk), F32),
                        pltpu.VMEM((1, blk), F32),
                        pltpu.VMEM((HEAD_DIM, blk), F32)],
        compiler_params=_params(("arbitrary", "arbitrary", "arbitrary")),
    )(rel_bias, qk3, qk3, rest3)


def _ssd_kernel(xs_ref, z_ref, bc_ref, dt_ref, cwx_ref, cwbc_ref, cbx_ref, cbbc_ref,
                dtb_ref, alog_ref, dsk_ref, ng_ref, tri_ref, trit_ref, exp_ref, o_ref,
                xext_ref, bcext_ref, state_ref):
    c = pl.program_id(1)
    t = CHUNK
    pad = SUBLANES

    @pl.when(c == 0)
    def _():
        xext_ref[0:pad, :] = jnp.zeros((pad, SSM_WIDTH), F32)
        bcext_ref[0:pad, :] = jnp.zeros((pad, 2 * N_GROUPS * D_STATE), F32)
        state_ref[...] = jnp.zeros_like(state_ref)

    def conv_silu(ext_ref, src_ref, w_ref, b_ref):
        ext_ref[pad:pad + t, :] = src_ref[0]
        acc = b_ref[...]
        for k in range(SSM_CONV):
            off = pad - (SSM_CONV - 1) + k
            acc = acc + w_ref[k:k + 1, :] * ext_ref[off:off + t, :]
        ext_ref[0:pad, :] = ext_ref[t:t + pad, :]
        return _silu(acc)

    xh = conv_silu(xext_ref, xs_ref, cwx_ref, cbx_ref)
    bc = conv_silu(bcext_ref, bc_ref, cwbc_ref, cbbc_ref)

    dtr = dt_ref[0] + dtb_ref[...]
    dt = jnp.maximum(dtr, 0.0) + jnp.log1p(jnp.exp(-jnp.abs(dtr)))
    adt = dt * (-jnp.exp(alog_ref[...]))
    acs = _dot_exact_lhs(tri_ref[...], adt)
    acs_t = _dot_exact_rhs(adt.T, trit_ref[...])

    stack = jnp.concatenate([dt, jnp.exp(acs), jnp.exp(acs[t - 1:t, :] - acs)], axis=0)
    wide = _dot_exact_rhs(stack, exp_ref[...])
    dt_x, eacs_x, dst_x = wide[0:t], wide[t:2 * t], wide[2 * t:3 * t]

    xdt = xh * dt_x
    xdt_bf = xdt.astype(BF16)
    xdec_bf = (xdt * dst_x).astype(BF16)

    row = lax.broadcasted_iota(jnp.int32, (t, t), 0)
    col = lax.broadcasted_iota(jnp.int32, (t, t), 1)
    tril = row >= col
    lane = lax.broadcasted_iota(jnp.int32, (t, LANES), 1)
    heads_per_group = N_SSM_HEADS // N_GROUPS
    nt = (((1,), (1,)), ((), ()))
    tn = (((0,), (0,)), ((), ()))

    y_parts = []
    for g in range(N_GROUPS):
        bg = bc[:, g * D_STATE:(g + 1) * D_STATE].astype(BF16)
        cg = bc[:, (N_GROUPS + g) * D_STATE:(N_GROUPS + g + 1) * D_STATE].astype(BF16)
        cb = _dot(cg, bg, nt)
        for pair in range(heads_per_group // 2):
            ms = []
            for r in (g * heads_per_group + 2 * pair, g * heads_per_group + 2 * pair + 1):
                seg = acs[:, r:r + 1] - acs_t[r:r + 1, :]
                ms.append((cb * jnp.exp(jnp.where(tril, seg, NEG))).astype(BF16))
            q = g * (heads_per_group // 2) + pair
            y2 = _dot(jnp.concatenate(ms, axis=0), xdt_bf[:, q * LANES:(q + 1) * LANES])
            y_parts.append(jnp.where(lane < SSM_HEAD_DIM, y2[0:t], y2[t:2 * t]))
    y = jnp.concatenate(y_parts, axis=1)

    off_parts = []
    for g in range(N_GROUPS):
        sl = slice(g * GROUP_WIDTH, (g + 1) * GROUP_WIDTH)
        bg = bc[:, g * D_STATE:(g + 1) * D_STATE].astype(BF16)
        cg = bc[:, (N_GROUPS + g) * D_STATE:(N_GROUPS + g + 1) * D_STATE].astype(BF16)
        st = state_ref[:, sl]
        off_parts.append(_dot(cg, st.astype(BF16)))
        state_ref[:, sl] = eacs_x[t - 1:t, sl] * st + _dot(bg, xdec_bf[:, sl], tn)
    y = y + jnp.concatenate(off_parts, axis=1) * eacs_x + dsk_ref[...] * xh
    y = y * _silu(z_ref[0])

    outs = []
    for g in range(N_GROUPS):
        sl = slice(g * GROUP_WIDTH, (g + 1) * GROUP_WIDTH)
        outs.append(_rms(y[:, sl]) * ng_ref[:, sl])
    o_ref[0] = jnp.concatenate(outs, axis=1).astype(o_ref.dtype)


def _ssd(rest3, conv_w, conv_b, dt_bias, a_log, d_skip, norm_g):
    bsz, seq, _ = rest3.shape
    t = CHUNK
    gn2 = 2 * N_GROUPS * D_STATE
    pad_h = LANES - N_SSM_HEADS
    tri = jnp.tril(jnp.ones((t, t), F32)).astype(BF16)
    expand = jnp.pad(jnp.repeat(jnp.eye(N_SSM_HEADS, dtype=F32), SSM_HEAD_DIM, axis=1),
                     ((0, pad_h), (0, 0))).astype(BF16)
    const = lambda shape: pl.BlockSpec(shape, lambda b, c: (0,) * len(shape))
    return pl.pallas_call(
        _ssd_kernel,
        grid=(bsz, seq // t),
        in_specs=[pl.BlockSpec((1, t, SSM_WIDTH), lambda b, c: (b, c, 1)),
                  pl.BlockSpec((1, t, SSM_WIDTH), lambda b, c: (b, c, 2)),
                  pl.BlockSpec((1, t, gn2), lambda b, c: (b, c, 3 * SSM_WIDTH // gn2)),
                  pl.BlockSpec((1, t, LANES), lambda b, c: (b, c, (3 * SSM_WIDTH + gn2) // LANES)),
                  const((SSM_CONV, SSM_WIDTH)), const((SSM_CONV, gn2)),
                  const((1, SSM_WIDTH)), const((1, gn2)),
                  const((1, LANES)), const((1, LANES)),
                  const((1, SSM_WIDTH)), const((1, SSM_WIDTH)),
                  const((t, t)), const((t, t)), const((LANES, SSM_WIDTH))],
        out_specs=pl.BlockSpec((1, t, SSM_WIDTH), lambda b, c: (b, c, 0)),
        out_shape=jax.ShapeDtypeStruct((bsz, seq, SSM_WIDTH), BF16),
        scratch_shapes=[pltpu.VMEM((t + 2 * SUBLANES, SSM_WIDTH), F32),
                        pltpu.VMEM((t + 2 * SUBLANES, gn2), F32),
                        pltpu.VMEM((D_STATE, SSM_WIDTH), F32)],
        compiler_params=_params(("arbitrary", "arbitrary")),
    )(rest3, rest3, rest3, rest3,
      conv_w[:, :SSM_WIDTH], conv_w[:, SSM_WIDTH:],
      conv_b[:SSM_WIDTH].reshape(1, -1), conv_b[SSM_WIDTH:].reshape(1, -1),
      jnp.pad(dt_bias, (0, pad_h)).reshape(1, LANES), jnp.pad(a_log, (0, pad_h)).reshape(1, LANES),
      jnp.repeat(d_skip, SSM_HEAD_DIM).reshape(1, SSM_WIDTH), norm_g.reshape(1, SSM_WIDTH),
      tri, tri.T, expand)


def _outproj_kernel(attn_ref, ssm_ref, x_ref, ag_ref, w_ref, gm_ref, g2_ref, sc_ref, sh_ref,
                    x1_ref, h2_ref, *, sub):
    for r0 in range(0, x_ref.shape[0], sub):
        rows = slice(r0, r0 + sub)
        a = _rms(attn_ref[rows, :]) * ag_ref[...]
        lhs = jnp.concatenate([a.astype(BF16), ssm_ref[rows, :]], axis=-1)
        x1 = x_ref[rows, :] + gm_ref[0] * _dot(lhs, w_ref[...])
        x1_ref[rows, :] = x1
        y = _rms(x1) * g2_ref[...]
        h2_ref[rows, :] = (y * (1.0 + sc_ref[0]) + sh_ref[0]).astype(BF16)


def _outproj(attn2, ssm2, x2, attn_g, w_out_bf, gate_m, g2, scale_f, shift_f, seq, tm, sub):
    m, d = x2.shape
    per_b = seq // tm
    vec = pl.BlockSpec((1, 1, d), lambda i: (i // per_b, 0, 0))
    half = pl.BlockSpec((tm, ATTN_WIDTH), lambda i: (i, 0))
    full = pl.BlockSpec((tm, d), lambda i: (i, 0))
    return pl.pallas_call(
        functools.partial(_outproj_kernel, sub=sub),
        grid=(m // tm,),
        in_specs=[half, half, full,
                  pl.BlockSpec((1, ATTN_WIDTH), lambda i: (0, 0)),
                  _resident((d, d)),
                  vec, pl.BlockSpec((1, d), lambda i: (0, 0)), vec, vec],
        out_specs=[full, full],
        out_shape=[jax.ShapeDtypeStruct((m, d), F32), jax.ShapeDtypeStruct((m, d), BF16)],
        compiler_params=_params(("parallel",)),
    )(attn2, ssm2, x2, attn_g.reshape(1, -1), w_out_bf, gate_m, g2.reshape(1, d), scale_f, shift_f)


def _ffn_up_kernel(h_ref, wg_ref, wv_ref, cwg_ref, cwv_ref, cbg_ref, cbv_ref, o_ref, eg_ref, ev_ref,
                   *, tm, sub, seq):
    i = pl.program_id(1)
    pad = SUBLANES

    @pl.when((i * tm) % seq == 0)
    def _():
        eg_ref[0:pad, :] = jnp.zeros((pad, eg_ref.shape[1]), F32)
        ev_ref[0:pad, :] = jnp.zeros((pad, ev_ref.shape[1]), F32)

    def conv(e_ref, h, r0, w_ref, cw_ref, cb_ref):
        e_ref[pad + r0:pad + r0 + sub, :] = _dot(h, w_ref[...])
        acc = cb_ref[...]
        for k in range(FFN_CONV):
            off = pad + r0 - (FFN_CONV - 1) + k
            acc = acc + cw_ref[k:k + 1, :] * e_ref[off:off + sub, :]
        return acc

    for r0 in range(0, tm, sub):
        h = h_ref[r0:r0 + sub, :]
        g = conv(eg_ref, h, r0, wg_ref, cwg_ref, cbg_ref)
        v = conv(ev_ref, h, r0, wv_ref, cwv_ref, cbv_ref)
        o_ref[r0:r0 + sub, :] = (_silu(g) * v).astype(o_ref.dtype)
    eg_ref[0:pad, :] = eg_ref[tm:tm + pad, :]
    ev_ref[0:pad, :] = ev_ref[tm:tm + pad, :]


def _ffn_up(h2, w_up_bf, conv_w, conv_b, seq, tm, sub, tn):
    m, d = h2.shape
    nj = FFN_DIM // tn
    cb = conv_b.reshape(1, -1)
    return pl.pallas_call(
        functools.partial(_ffn_up_kernel, tm=tm, sub=sub, seq=seq),
        grid=(nj, m // tm),
        in_specs=[pl.BlockSpec((tm, d), lambda j, i: (i, 0)),
                  pl.BlockSpec((d, tn), lambda j, i: (0, j)),
                  pl.BlockSpec((d, tn), lambda j, i: (0, j + nj)),
                  pl.BlockSpec((FFN_CONV, tn), lambda j, i: (0, j)),
                  pl.BlockSpec((FFN_CONV, tn), lambda j, i: (0, j + nj)),
                  pl.BlockSpec((1, tn), lambda j, i: (0, j)),
                  pl.BlockSpec((1, tn), lambda j, i: (0, j + nj))],
        out_specs=pl.BlockSpec((tm, tn), lambda j, i: (i, j)),
        out_shape=jax.ShapeDtypeStruct((m, FFN_DIM), BF16),
        scratch_shapes=[pltpu.VMEM((tm + 2 * SUBLANES, tn), F32)] * 2,
        compiler_params=_params(("parallel", "arbitrary")),
    )(h2, w_up_bf, w_up_bf, conv_w, conv_w, cb, cb)


def _ffn_down_kernel(a_ref, w_ref, x1_ref, gf_ref, fg_ref, o_ref):
    x2 = x1_ref[...] + gf_ref[0] * _dot(a_ref[...], w_ref[...])
    o_ref[...] = _rms(x2) * fg_ref[...]


def _ffn_down(act, w_down_bf, x1, gate_f, final_g, seq, tm):
    m, d = x1.shape
    per_b = seq // tm
    row = lambda width: pl.BlockSpec((tm, width), lambda i: (i, 0))
    return pl.pallas_call(
        _ffn_down_kernel,
        grid=(m // tm,),
        in_specs=[row(FFN_DIM), _resident((FFN_DIM, d)), row(d),
                  pl.BlockSpec((1, 1, d), lambda i: (i // per_b, 0, 0)),
                  pl.BlockSpec((1, d), lambda i: (0, 0))],
        out_specs=row(d),
        out_shape=jax.ShapeDtypeStruct((m, d), F32),
        compiler_params=_params(("parallel",)),
    )(act, w_down_bf, x1, gate_f, final_g.reshape(1, d))


def kernel(x, c, w_ada, b_ada, norm_mix_g, w_in, rel_bias, attn_norm_g, conv_ssm_w, conv_ssm_b, dt_bias,
           a_log, d_skip, ssm_norm_g, w_out, norm_ffn_g, w_up, conv_ffn_w, conv_ffn_b, w_down, final_norm_g):
    bsz, seq, d = x.shape
    m = bsz * seq
    assert w_ada.shape[0] == 1, "the final RMSNorm is fused into the (single) layer's ffn_down kernel"
    l = 0
    x2 = x.reshape(m, d)
    mod = _ada(c, w_ada[l], b_ada[l])
    shift_m, scale_m, gate_m, shift_f, scale_f, gate_f = [
        mod[:, k * d:(k + 1) * d].reshape(bsz, 1, d) for k in range(6)]

    w_qk = w_in[l][:, :QK_COLS]
    w_qk_hi = w_qk.astype(BF16)
    w_qk_lo = (w_qk - w_qk_hi.astype(F32)).astype(BF16)
    w_rest = jnp.pad(w_in[l][:, QK_COLS:].astype(BF16), ((0, 0), (0, REST_COLS - (w_in.shape[2] - QK_COLS))))
    qk = _normproj(x2, norm_mix_g[l], scale_m, shift_m, (w_qk_hi, w_qk_lo), seq, 512, 256, F32)
    rest = _normproj(x2, norm_mix_g[l], scale_m, shift_m, (w_rest,), seq, 512, 256, F32)
    qk3 = qk.reshape(bsz, seq, QK_COLS)
    rest3 = rest.reshape(bsz, seq, REST_COLS)

    attn = _moba(rel_bias, qk3, rest3)
    ssm = _ssd(rest3, conv_ssm_w[l], conv_ssm_b[l], dt_bias[l], a_log[l], d_skip[l], ssm_norm_g[l])

    x1, h2 = _outproj(attn.reshape(m, ATTN_WIDTH), ssm.reshape(m, SSM_WIDTH), x2, attn_norm_g[l],
                      w_out[l].astype(BF16), gate_m, norm_ffn_g[l], scale_f, shift_f, seq, 512, 256)
    act = _ffn_up(h2, w_up[l].astype(BF16), conv_ffn_w[l], conv_ffn_b[l], seq, 1024, 256, 512)
    out = _ffn_down(act, w_down[l].astype(BF16), x1, gate_f, final_norm_g, seq, 256)
    return out.reshape(bsz, seq, d)
```

```python
import functools
import math

import jax
import jax.numpy as jnp
from jax import lax
from jax.experimental import pallas as pl
from jax.experimental.pallas import tpu as pltpu

F32 = jnp.float32
BF16 = jnp.bfloat16

D_MODEL = 2048
ATTN_WIDTH = 1024
HEAD_DIM = 128
N_HEADS = 8
SSM_WIDTH = 1024
SSM_HEAD_DIM = 64
N_SSM_HEADS = 16
N_GROUPS = 2
GROUP_WIDTH = SSM_WIDTH // N_GROUPS
D_STATE = 128
SSM_CONV = 4
CHUNK = 256
MOBA_BLOCK = 256
MOBA_TOPK = 3
MOBA_FAR_GROUP = 4
REL_BUCKETS = 32
REL_MAX_DIST = 128
FFN_DIM = 5632
FFN_CONV = 3
EPS = 1e-6
NEG = -1e30

LANES = 128
SUBLANES = 8
VMEM_LIMIT = 56 * 1024 * 1024

REST_COLS = 3712
QK_COLS = 2 * ATTN_WIDTH


def _params(sem):
    return pltpu.CompilerParams(dimension_semantics=sem, vmem_limit_bytes=VMEM_LIMIT)


def _split3(x):
    hi = x.astype(BF16)
    r = x - hi.astype(F32)
    mid = r.astype(BF16)
    lo = (r - mid.astype(F32)).astype(BF16)
    return hi, mid, lo


NT = (((1,), (1,)), ((), ()))


def _dot(a, b, dims=(((1,), (0,)), ((), ()))):
    return lax.dot_general(a, b, dims, preferred_element_type=F32)


def _dot_exact_lhs(a_bf, x, dims=(((1,), (0,)), ((), ()))):
    hi, mid, lo = _split3(x)
    return _dot(a_bf, hi, dims) + _dot(a_bf, mid, dims) + _dot(a_bf, lo, dims)


def _dot_exact_rhs(x, b_bf, dims=(((1,), (0,)), ((), ()))):
    hi, mid, lo = _split3(x)
    return _dot(hi, b_bf, dims) + _dot(mid, b_bf, dims) + _dot(lo, b_bf, dims)


def _silu(x):
    return x * jax.nn.sigmoid(x)


def _rms(x):
    return x * lax.rsqrt(jnp.mean(x * x, axis=-1, keepdims=True) + EPS)


def _ada_kernel(ct_ref, w_ref, b_ref, o_ref, sb_ref):
    nb = sb_ref.shape[0]
    d = w_ref.shape[0]
    tn = o_ref.shape[-1]

    @pl.when(pl.program_id(0) == 0)
    def _():
        ct = ct_ref[...]
        st = _silu(ct)
        for b in range(nb):
            sb_ref[b] = jnp.broadcast_to(st[:, b:b + 1], (d, LANES))

    def body(kc, accs):
        r = pl.multiple_of(kc * SUBLANES, SUBLANES)
        w8 = w_ref[pl.ds(r, SUBLANES), :]
        out = []
        for b in range(nb):
            s8 = sb_ref[b, pl.ds(r, SUBLANES), :]
            out.append(accs[b] + w8 * jnp.tile(s8, (1, tn // LANES)))
        return tuple(out)

    accs = lax.fori_loop(0, d // SUBLANES, body,
                         tuple(jnp.zeros((SUBLANES, tn), F32) for _ in range(nb)), unroll=8)
    for b in range(nb):
        o_ref[b:b + 1, :] = jnp.sum(accs[b], axis=0, keepdims=True) + b_ref[...]


def _ada(c, w_ada, b_ada, tn=1024):
    nb, d = c.shape
    n = w_ada.shape[1]
    return pl.pallas_call(
        _ada_kernel,
        grid=(n // tn,),
        in_specs=[pl.BlockSpec((d, nb), lambda j: (0, 0)),
                  pl.BlockSpec((d, tn), lambda j: (0, j)),
                  pl.BlockSpec((1, tn), lambda j: (0, j))],
        out_specs=pl.BlockSpec((nb, tn), lambda j: (0, j)),
        out_shape=jax.ShapeDtypeStruct((nb, n), F32),
        scratch_shapes=[pltpu.VMEM((nb, d, LANES), F32)],
        compiler_params=_params(("arbitrary",)),
    )(c.T, w_ada, b_ada.reshape(1, n))


def _normproj_kernel(x_ref, g_ref, sc_ref, sh_ref, *rest, split, sub):
    if split:
        whi_ref, wlo_ref, o_ref = rest
    else:
        whi_ref, o_ref = rest
    for r0 in range(0, x_ref.shape[0], sub):
        rows = slice(r0, r0 + sub)
        y = _rms(x_ref[rows, :]) * g_ref[...]
        h = y * (1.0 + sc_ref[0]) + sh_ref[0]
        hi = h.astype(BF16)
        acc = _dot(hi, whi_ref[...], NT)
        if split:
            lo = (h - hi.astype(F32)).astype(BF16)
            acc = acc + _dot(lo, whi_ref[...], NT) + _dot(hi, wlo_ref[...], NT)
        o_ref[rows, :] = acc.astype(o_ref.dtype)


def _resident(shape):
    return pl.BlockSpec(shape, lambda *_: (0,) * len(shape), pipeline_mode=pl.Buffered(1))


def _normproj(x2, g, scale, shift, ws, seq, tm, sub, out_dtype):
    m, d = x2.shape
    n = ws[0].shape[0]
    per_b = seq // tm
    vec = pl.BlockSpec((1, 1, d), lambda i: (i // per_b, 0, 0))
    return pl.pallas_call(
        functools.partial(_normproj_kernel, split=len(ws) == 2, sub=sub),
        grid=(m // tm,),
        in_specs=[pl.BlockSpec((tm, d), lambda i: (i, 0)),
                  pl.BlockSpec((1, d), lambda i: (0, 0)),
                  vec, vec] + [_resident((n, d))] * len(ws),
        out_specs=pl.BlockSpec((tm, n), lambda i: (i, 0)),
        out_shape=jax.ShapeDtypeStruct((m, n), out_dtype),
        compiler_params=_params(("parallel",)),
    )(x2, g.reshape(1, d), scale, shift, *ws)


def _rel_bucket(dist):
    n = jnp.maximum(dist, 0)
    max_exact = REL_BUCKETS // 2
    nf = jnp.maximum(n, max_exact).astype(F32)
    large = max_exact + (jnp.log(nf / max_exact) / math.log(REL_MAX_DIST / max_exact)
                         * (REL_BUCKETS - max_exact)).astype(jnp.int32)
    large = jnp.minimum(large, REL_BUCKETS - 1)
    return jnp.where(n < max_exact, n, large)


def _moba_kernel(rb_ref, q_ref, k_ref, v_ref, o_ref,
                 kbf_ref, vt_ref, tcat_ref, mask_ref, farmask_ref, s_buf, p_buf, alpha_buf,
                 m_ref, l_ref, acc_ref):
    h = pl.program_id(1)
    i = pl.program_id(2)
    nblk = vt_ref.shape[0]
    seq = k_ref.shape[1]
    blk = MOBA_BLOCK
    grp = MOBA_FAR_GROUP
    scale = HEAD_DIM ** -0.5
    nt = (((1,), (1,)), ((), ()))

    @pl.when(i == 0)
    def _init():
        kf = k_ref[0]
        kbf_ref[...] = kf.astype(BF16)
        kmean = jnp.mean(kf.reshape(nblk, blk, HEAD_DIM), axis=1)
        for j in range(nblk):
            vt_ref[j] = v_ref[0, j * blk:(j + 1) * blk, :].T.astype(BF16)
        b_far = rb_ref[REL_BUCKETS - 1, h]
        kk = lax.broadcasted_iota(jnp.int32, (blk, blk), 0)
        qq = lax.broadcasted_iota(jnp.int32, (blk, blk), 1)
        for half, dist in ((0, qq - kk + blk), (1, qq - kk)):
            bucket = _rel_bucket(dist)
            tab = jnp.zeros((blk, blk), F32)
            for b in range(REL_BUCKETS):
                tab = jnp.where(bucket == b, rb_ref[b, h] - b_far, tab)
            tcat_ref[half * blk:(half + 1) * blk, :] = jnp.where(dist >= 0, tab, NEG)

        q3 = _split3(q_ref[0])
        k3 = _split3(kmean)
        gate = jnp.zeros((nblk, seq), F32)
        for a, b in ((0, 0), (0, 1), (1, 0), (1, 1), (0, 2), (2, 0)):
            gate = gate + _dot(k3[a], q3[b], nt)
        nidx = lax.broadcasted_iota(jnp.int32, (nblk, seq), 0)
        qblk = lax.broadcasted_iota(jnp.int32, (nblk, seq), 1) // blk
        nidx_f = nidx.astype(F32)
        avail = jnp.where(nidx < qblk, 1.0, 0.0)
        chosen = jnp.zeros((nblk, seq), F32)
        for _ in range(MOBA_TOPK):
            gm = jnp.where(avail > 0.0, gate, -jnp.inf)
            best = jnp.max(gm, axis=0, keepdims=True)
            first = jnp.min(jnp.where((gm == best) & (avail > 0.0), nidx_f, float(nblk)), axis=0, keepdims=True)
            hit = nidx_f == first
            chosen = jnp.where(hit, 1.0, chosen)
            avail = jnp.where(hit, 0.0, avail)
        add_all = jnp.where(chosen > 0.0, 0.0, NEG)
        add_far = jnp.where(nidx < qblk - 1, add_all, NEG)
        for qi in range(nblk):
            cols = slice(qi * blk, (qi + 1) * blk)
            for j in range(nblk):
                mask_ref[qi * nblk + j] = jnp.broadcast_to(add_all[j:j + 1, cols], (SUBLANES, blk))
                farmask_ref[qi * nblk + j] = jnp.broadcast_to(add_far[j:j + 1, cols], (SUBLANES, blk))

    qb = q_ref[0, pl.ds(pl.multiple_of(i * blk, blk), blk), :].astype(BF16)

    def pv(pb, blocks):
        out = None
        for n, j in enumerate(blocks):
            d = _dot(vt_ref[j], pb[n * blk:(n + 1) * blk, :])
            out = d if out is None else out + d
        return out

    def first_unit(s, blocks):
        m0 = jnp.max(s, axis=0, keepdims=True)
        p = jnp.exp(s - m0)
        m_ref[...] = m0
        l_ref[...] = jnp.sum(p, axis=0, keepdims=True)
        acc_ref[...] = pv(p.astype(BF16), blocks)

    def far_scores(slot, t):
        start = pl.multiple_of(t * (grp * blk), grp * blk)
        s = _dot(kbf_ref[pl.ds(start, grp * blk), :], qb, nt) * scale
        s_buf[slot] = jnp.concatenate(
            [s[n * blk:(n + 1) * blk] + farmask_ref[i * nblk + t * grp + n][0:1, :] for n in range(grp)], axis=0)

    def far_softmax(slot):
        s = s_buf[slot]
        m_old = m_ref[...]
        m_new = jnp.maximum(m_old, jnp.max(s, axis=0, keepdims=True))
        alpha = jnp.exp(m_old - m_new)
        p = jnp.exp(s - m_new)
        m_ref[...] = m_new
        l_ref[...] = alpha * l_ref[...] + jnp.sum(p, axis=0, keepdims=True)
        alpha_buf[slot] = alpha
        p_buf[slot] = p.astype(BF16)

    def far_values(slot, t):
        acc_ref[...] = alpha_buf[slot] * acc_ref[...] + pv(p_buf[slot], [t * grp + n for n in range(grp)])

    @pl.when(i == 0)
    def _():
        first_unit(_dot(kbf_ref[0:blk, :], qb, nt) * scale + tcat_ref[blk:2 * blk, :], [0])

    @pl.when(i > 0)
    def _():
        start = pl.multiple_of((i - 1) * blk, blk)
        s = _dot(kbf_ref[pl.ds(start, 2 * blk), :], qb, nt) * scale + tcat_ref[...]
        s = jnp.concatenate([s[0:blk] + mask_ref[i * nblk + i - 1][0:1, :], s[blk:2 * blk]], axis=0)
        first_unit(s, [i - 1, i])
        far_scores(0, 0)
        alpha_buf[1] = jnp.ones((1, blk), F32)
        p_buf[1] = jnp.zeros((grp * blk, blk), BF16)

    ngroups = (i + grp - 2) // grp

    def trip(t, carry):
        slot = t % 2
        far_values(1 - slot, jnp.maximum(t - 1, 0))
        far_softmax(slot)
        far_scores(1 - slot, jnp.minimum(t + 1, nblk // grp - 1))
        return carry

    lax.fori_loop(0, ngroups, trip, 0)

    @pl.when(ngroups > 0)
    def _():
        far_values((ngroups - 1) % 2, ngroups - 1)

    o_ref[0] = (acc_ref[...] / l_ref[...]).T


def _moba(rel_bias, qk3, rest3):
    bsz, seq, _ = qk3.shape
    nblk = seq // MOBA_BLOCK
    blk = MOBA_BLOCK
    assert nblk % MOBA_FAR_GROUP == 0
    return pl.pallas_call(
        _moba_kernel,
        grid=(bsz, N_HEADS, nblk),
        in_specs=[pl.BlockSpec(memory_space=pltpu.SMEM),
                  pl.BlockSpec((1, seq, HEAD_DIM), lambda b, h, i: (b, 0, h)),
                  pl.BlockSpec((1, seq, HEAD_DIM), lambda b, h, i: (b, 0, N_HEADS + h)),
                  pl.BlockSpec((1, seq, HEAD_DIM), lambda b, h, i: (b, 0, h))],
        out_specs=pl.BlockSpec((1, blk, HEAD_DIM), lambda b, h, i: (b, i, h)),
        out_shape=jax.ShapeDtypeStruct((bsz, seq, ATTN_WIDTH), F32),
        scratch_shapes=[pltpu.VMEM((seq, HEAD_DIM), BF16),
                        pltpu.VMEM((nblk, HEAD_DIM, blk), BF16),
                        pltpu.VMEM((2 * blk, blk), F32),
                        pltpu.VMEM((nblk * nblk, SUBLANES, blk), F32),
                        pltpu.VMEM((nblk * nblk, SUBLANES, blk), F32),
                        pltpu.VMEM((2, MOBA_FAR_GROUP * blk, blk), F32),
                        pltpu.VMEM((2, MOBA_FAR_GROUP * blk, blk), BF16),
                        pltpu.VMEM((2, 1, blk), F32),
                        pltpu.VMEM((1, blk), F32),
                        pltpu.VMEM((1, blk), F32),
                        pltpu.VMEM((HEAD_DIM, blk), F32)],
        compiler_params=_params(("arbitrary", "arbitrary", "arbitrary")),
    )(rel_bias, qk3, qk3, rest3)


def _ssd_kernel(xs_ref, z_ref, bc_ref, dt_ref, cwx_ref, cwbc_ref, cbx_ref, cbbc_ref,
                dtb_ref, alog_ref, dsk_ref, ng_ref, tri_ref, trit_ref, exp_ref, o_ref,
                xext_ref, bcext_ref, state_ref):
    c = pl.program_id(1)
    t = CHUNK
    pad = SUBLANES

    @pl.when(c == 0)
    def _():
        xext_ref[0:pad, :] = jnp.zeros((pad, SSM_WIDTH), F32)
        bcext_ref[0:pad, :] = jnp.zeros((pad, 2 * N_GROUPS * D_STATE), F32)
        state_ref[...] = jnp.zeros_like(state_ref)

    def conv_silu(ext_ref, src_ref, w_ref, b_ref):
        ext_ref[pad:pad + t, :] = src_ref[0]
        acc = b_ref[...]
        for k in range(SSM_CONV):
            off = pad - (SSM_CONV - 1) + k
            acc = acc + w_ref[k:k + 1, :] * ext_ref[off:off + t, :]
        ext_ref[0:pad, :] = ext_ref[t:t + pad, :]
        return _silu(acc)

    xh = conv_silu(xext_ref, xs_ref, cwx_ref, cbx_ref)
    bc = conv_silu(bcext_ref, bc_ref, cwbc_ref, cbbc_ref)

    dtr = dt_ref[0] + dtb_ref[...]
    dt = jnp.maximum(dtr, 0.0) + jnp.log1p(jnp.exp(-jnp.abs(dtr)))
    adt = dt * (-jnp.exp(alog_ref[...]))
    acs = _dot_exact_lhs(tri_ref[...], adt)
    acs_t = _dot_exact_rhs(adt.T, trit_ref[...])

    stack = jnp.concatenate([dt, jnp.exp(acs), jnp.exp(acs[t - 1:t, :] - acs)], axis=0)
    wide = _dot_exact_rhs(stack, exp_ref[...])
    dt_x, eacs_x, dst_x = wide[0:t], wide[t:2 * t], wide[2 * t:3 * t]

    xdt = xh * dt_x
    xdt_bf = xdt.astype(BF16)
    xdec_bf = (xdt * dst_x).astype(BF16)

    row = lax.broadcasted_iota(jnp.int32, (t, t), 0)
    col = lax.broadcasted_iota(jnp.int32, (t, t), 1)
    tril = row >= col
    lane = lax.broadcasted_iota(jnp.int32, (t, LANES), 1)
    heads_per_group = N_SSM_HEADS // N_GROUPS
    nt = (((1,), (1,)), ((), ()))
    tn = (((0,), (0,)), ((), ()))

    y_parts = []
    for g in range(N_GROUPS):
        bg = bc[:, g * D_STATE:(g + 1) * D_STATE].astype(BF16)
        cg = bc[:, (N_GROUPS + g) * D_STATE:(N_GROUPS + g + 1) * D_STATE].astype(BF16)
        cb = _dot(cg, bg, nt)
        for pair in range(heads_per_group // 2):
            ms = []
            for r in (g * heads_per_group + 2 * pair, g * heads_per_group + 2 * pair + 1):
                seg = acs[:, r:r + 1] - acs_t[r:r + 1, :]
                ms.append((cb * jnp.exp(jnp.where(tril, seg, NEG))).astype(BF16))
            q = g * (heads_per_group // 2) + pair
            y2 = _dot(jnp.concatenate(ms, axis=0), xdt_bf[:, q * LANES:(q + 1) * LANES])
            y_parts.append(jnp.where(lane < SSM_HEAD_DIM, y2[0:t], y2[t:2 * t]))
    y = jnp.concatenate(y_parts, axis=1)

    off_parts = []
    for g in range(N_GROUPS):
        sl = slice(g * GROUP_WIDTH, (g + 1) * GROUP_WIDTH)
        bg = bc[:, g * D_STATE:(g + 1) * D_STATE].astype(BF16)
        cg = bc[:, (N_GROUPS + g) * D_STATE:(N_GROUPS + g + 1) * D_STATE].astype(BF16)
        st = state_ref[:, sl]
        off_parts.append(_dot(cg, st.astype(BF16)))
        state_ref[:, sl] = eacs_x[t - 1:t, sl] * st + _dot(bg, xdec_bf[:, sl], tn)
    y = y + jnp.concatenate(off_parts, axis=1) * eacs_x + dsk_ref[...] * xh
    y = y * _silu(z_ref[0])

    outs = []
    for g in range(N_GROUPS):
        sl = slice(g * GROUP_WIDTH, (g + 1) * GROUP_WIDTH)
        outs.append(_rms(y[:, sl]) * ng_ref[:, sl])
    o_ref[0] = jnp.concatenate(outs, axis=1).astype(o_ref.dtype)


def _ssd(rest3, conv_w, conv_b, dt_bias, a_log, d_skip, norm_g):
    bsz, seq, _ = rest3.shape
    t = CHUNK
    gn2 = 2 * N_GROUPS * D_STATE
    pad_h = LANES - N_SSM_HEADS
    tri = jnp.tril(jnp.ones((t, t), F32)).astype(BF16)
    expand = jnp.pad(jnp.repeat(jnp.eye(N_SSM_HEADS, dtype=F32), SSM_HEAD_DIM, axis=1),
                     ((0, pad_h), (0, 0))).astype(BF16)
    const = lambda shape: pl.BlockSpec(shape, lambda b, c: (0,) * len(shape))
    return pl.pallas_call(
        _ssd_kernel,
        grid=(bsz, seq // t),
        in_specs=[pl.BlockSpec((1, t, SSM_WIDTH), lambda b, c: (b, c, 1)),
                  pl.BlockSpec((1, t, SSM_WIDTH), lambda b, c: (b, c, 2)),
                  pl.BlockSpec((1, t, gn2), lambda b, c: (b, c, 3 * SSM_WIDTH // gn2)),
                  pl.BlockSpec((1, t, LANES), lambda b, c: (b, c, (3 * SSM_WIDTH + gn2) // LANES)),
                  const((SSM_CONV, SSM_WIDTH)), const((SSM_CONV, gn2)),
                  const((1, SSM_WIDTH)), const((1, gn2)),
                  const((1, LANES)), const((1, LANES)),
                  const((1, SSM_WIDTH)), const((1, SSM_WIDTH)),
                  const((t, t)), const((t, t)), const((LANES, SSM_WIDTH))],
        out_specs=pl.BlockSpec((1, t, SSM_WIDTH), lambda b, c: (b, c, 0)),
        out_shape=jax.ShapeDtypeStruct((bsz, seq, SSM_WIDTH), BF16),
        scratch_shapes=[pltpu.VMEM((t + 2 * SUBLANES, SSM_WIDTH), F32),
                        pltpu.VMEM((t + 2 * SUBLANES, gn2), F32),
                        pltpu.VMEM((D_STATE, SSM_WIDTH), F32)],
        compiler_params=_params(("arbitrary", "arbitrary")),
    )(rest3, rest3, rest3, rest3,
      conv_w[:, :SSM_WIDTH], conv_w[:, SSM_WIDTH:],
      conv_b[:SSM_WIDTH].reshape(1, -1), conv_b[SSM_WIDTH:].reshape(1, -1),
      jnp.pad(dt_bias, (0, pad_h)).reshape(1, LANES), jnp.pad(a_log, (0, pad_h)).reshape(1, LANES),
      jnp.repeat(d_skip, SSM_HEAD_DIM).reshape(1, SSM_WIDTH), norm_g.reshape(1, SSM_WIDTH),
      tri, tri.T, expand)


def _outproj_kernel(attn_ref, ssm_ref, x_ref, ag_ref, w_ref, gm_ref, g2_ref, sc_ref, sh_ref,
                    x1_ref, h2_ref, *, sub):
    for r0 in range(0, x_ref.shape[0], sub):
        rows = slice(r0, r0 + sub)
        a = _rms(attn_ref[rows, :]) * ag_ref[...]
        lhs = jnp.concatenate([a.astype(BF16), ssm_ref[rows, :]], axis=-1)
        x1 = x_ref[rows, :] + gm_ref[0] * _dot(lhs, w_ref[...])
        x1_ref[rows, :] = x1
        y = _rms(x1) * g2_ref[...]
        h2_ref[rows, :] = (y * (1.0 + sc_ref[0]) + sh_ref[0]).astype(BF16)


def _outproj(attn2, ssm2, x2, attn_g, w_out_bf, gate_m, g2, scale_f, shift_f, seq, tm, sub):
    m, d = x2.shape
    per_b = seq // tm
    vec = pl.BlockSpec((1, 1, d), lambda i: (i // per_b, 0, 0))
    half = pl.BlockSpec((tm, ATTN_WIDTH), lambda i: (i, 0))
    full = pl.BlockSpec((tm, d), lambda i: (i, 0))
    return pl.pallas_call(
        functools.partial(_outproj_kernel, sub=sub),
        grid=(m // tm,),
        in_specs=[half, half, full,
                  pl.BlockSpec((1, ATTN_WIDTH), lambda i: (0, 0)),
                  _resident((d, d)),
                  vec, pl.BlockSpec((1, d), lambda i: (0, 0)), vec, vec],
        out_specs=[full, full],
        out_shape=[jax.ShapeDtypeStruct((m, d), F32), jax.ShapeDtypeStruct((m, d), BF16)],
        compiler_params=_params(("parallel",)),
    )(attn2, ssm2, x2, attn_g.reshape(1, -1), w_out_bf, gate_m, g2.reshape(1, d), scale_f, shift_f)


def _ffn_up_kernel(h_ref, wg_ref, wv_ref, cw_ref, cb_ref, o_ref, wb_ref, e_ref, *, tm, sub, seq):
    i = pl.program_id(1)
    pad = SUBLANES
    tn = o_ref.shape[1]

    @pl.when(i == 0)
    def _():
        wb_ref[:, 0:tn] = wg_ref[...].astype(BF16)
        wb_ref[:, tn:2 * tn] = wv_ref[...].astype(BF16)

    @pl.when((i * tm) % seq == 0)
    def _():
        e_ref[0:pad, :] = jnp.zeros((pad, 2 * tn), F32)

    for r0 in range(0, tm, sub):
        u = _dot(h_ref[r0:r0 + sub, :], wb_ref[...])
        e_ref[pad + r0:pad + r0 + sub, :] = u
        ext = e_ref[r0:r0 + pad + sub, :]
        acc = cb_ref[...] + cw_ref[FFN_CONV - 1:FFN_CONV, :] * u
        for back in range(1, FFN_CONV):
            shifted = pltpu.roll(ext, back, axis=0)[pad:pad + sub, :]
            acc = acc + cw_ref[FFN_CONV - 1 - back:FFN_CONV - back, :] * shifted
        o_ref[r0:r0 + sub, :] = (_silu(acc[:, 0:tn]) * acc[:, tn:2 * tn]).astype(o_ref.dtype)
    e_ref[0:pad, :] = e_ref[tm:tm + pad, :]


def _ffn_up(h2, w_up, conv_w, conv_b, seq, tm, sub, tn):
    m, d = h2.shape
    nj = FFN_DIM // tn
    pair = lambda a: jnp.concatenate([a[:, :FFN_DIM].reshape(-1, nj, tn), a[:, FFN_DIM:].reshape(-1, nj, tn)],
                                     axis=2).reshape(-1, 2 * FFN_DIM)
    return pl.pallas_call(
        functools.partial(_ffn_up_kernel, tm=tm, sub=sub, seq=seq),
        grid=(nj, m // tm),
        in_specs=[pl.BlockSpec((tm, d), lambda j, i: (i, 0)),
                  pl.BlockSpec((d, tn), lambda j, i: (0, j)),
                  pl.BlockSpec((d, tn), lambda j, i: (0, j + nj)),
                  pl.BlockSpec((FFN_CONV, 2 * tn), lambda j, i: (0, j)),
                  pl.BlockSpec((1, 2 * tn), lambda j, i: (0, j))],
        out_specs=pl.BlockSpec((tm, tn), lambda j, i: (i, j)),
        out_shape=jax.ShapeDtypeStruct((m, FFN_DIM), BF16),
        scratch_shapes=[pltpu.VMEM((d, 2 * tn), BF16), pltpu.VMEM((tm + 2 * SUBLANES, 2 * tn), F32)],
        compiler_params=_params(("arbitrary", "arbitrary")),
    )(h2, w_up, w_up, pair(conv_w), pair(conv_b.reshape(1, -1)))


def _ffn_down_kernel(a_ref, w_ref, x1_ref, gf_ref, fg_ref, o_ref):
    x2 = x1_ref[...] + gf_ref[0] * _dot(a_ref[...], w_ref[...])
    o_ref[...] = _rms(x2) * fg_ref[...]


def _ffn_down(act, w_down_bf, x1, gate_f, final_g, seq, tm):
    m, d = x1.shape
    per_b = seq // tm
    row = lambda width: pl.BlockSpec((tm, width), lambda i: (i, 0))
    return pl.pallas_call(
        _ffn_down_kernel,
        grid=(m // tm,),
        in_specs=[row(FFN_DIM), _resident((FFN_DIM, d)), row(d),
                  pl.BlockSpec((1, 1, d), lambda i: (i // per_b, 0, 0)),
                  pl.BlockSpec((1, d), lambda i: (0, 0))],
        out_specs=row(d),
        out_shape=jax.ShapeDtypeStruct((m, d), F32),
        compiler_params=_params(("parallel",)),
    )(act, w_down_bf, x1, gate_f, final_g.reshape(1, d))


def kernel(x, c, w_ada, b_ada, norm_mix_g, w_in, rel_bias, attn_norm_g, conv_ssm_w, conv_ssm_b, dt_bias,
           a_log, d_skip, ssm_norm_g, w_out, norm_ffn_g, w_up, conv_ffn_w, conv_ffn_b, w_down, final_norm_g):
    bsz, seq, d = x.shape
    m = bsz * seq
    assert w_ada.shape[0] == 1, "the final RMSNorm is fused into the (single) layer's ffn_down kernel"
    l = 0
    x2 = x.reshape(m, d)
    mod = _ada(c, w_ada[l], b_ada[l])
    shift_m, scale_m, gate_m, shift_f, scale_f, gate_f = [
        mod[:, k * d:(k + 1) * d].reshape(bsz, 1, d) for k in range(6)]

    w_in_t = w_in[l].T
    w_qk = w_in_t[:QK_COLS]
    w_qk_hi = w_qk.astype(BF16)
    w_qk_lo = (w_qk - w_qk_hi.astype(F32)).astype(BF16)
    w_rest = jnp.pad(w_in_t[QK_COLS:].astype(BF16), ((0, REST_COLS - (w_in.shape[2] - QK_COLS)), (0, 0)))
    qk = _normproj(x2, norm_mix_g[l], scale_m, shift_m, (w_qk_hi, w_qk_lo), seq, 512, 256, F32)
    rest = _normproj(x2, norm_mix_g[l], scale_m, shift_m, (w_rest,), seq, 512, 256, F32)
    qk3 = qk.reshape(bsz, seq, QK_COLS)
    rest3 = rest.reshape(bsz, seq, REST_COLS)

    attn = _moba(rel_bias, qk3, rest3)
    ssm = _ssd(rest3, conv_ssm_w[l], conv_ssm_b[l], dt_bias[l], a_log[l], d_skip[l], ssm_norm_g[l])

    x1, h2 = _outproj(attn.reshape(m, ATTN_WIDTH), ssm.reshape(m, SSM_WIDTH), x2, attn_norm_g[l],
                      w_out[l].astype(BF16), gate_m, norm_ffn_g[l], scale_f, shift_f, seq, 512, 256)
    act = _ffn_up(h2, w_up[l], conv_ffn_w[l], conv_ffn_b[l], seq, 1024, 256, 512)
    out = _ffn_down(act, w_down[l].astype(BF16), x1, gate_f, final_norm_g, seq, 256)
    return out.reshape(bsz, seq, d)
```

```python
import functools
import math

import jax
import jax.numpy as jnp
from jax import lax
from jax.experimental import pallas as pl
from jax.experimental.pallas import tpu as pltpu

F32 = jnp.float32
BF16 = jnp.bfloat16

D_MODEL = 2048
ATTN_WIDTH = 1024
HEAD_DIM = 128
N_HEADS = 8
SSM_WIDTH = 1024
SSM_HEAD_DIM = 64
N_SSM_HEADS = 16
N_GROUPS = 2
GROUP_WIDTH = SSM_WIDTH // N_GROUPS
D_STATE = 128
SSM_CONV = 4
CHUNK = 256
MOBA_BLOCK = 256
MOBA_TOPK = 3
MOBA_FAR_GROUP = 4
REL_BUCKETS = 32
REL_MAX_DIST = 128
FFN_DIM = 5632
FFN_CONV = 3
EPS = 1e-6
NEG = -1e30

LANES = 128
SUBLANES = 8
VMEM_LIMIT = 56 * 1024 * 1024

REST_COLS = 3712
QK_COLS = 2 * ATTN_WIDTH


def _params(sem):
    return pltpu.CompilerParams(dimension_semantics=sem, vmem_limit_bytes=VMEM_LIMIT)


def _split3(x):
    hi = x.astype(BF16)
    r = x - hi.astype(F32)
    mid = r.astype(BF16)
    lo = (r - mid.astype(F32)).astype(BF16)
    return hi, mid, lo


NT = (((1,), (1,)), ((), ()))


def _dot(a, b, dims=(((1,), (0,)), ((), ()))):
    return lax.dot_general(a, b, dims, preferred_element_type=F32)


def _dot_exact_lhs(a_bf, x, dims=(((1,), (0,)), ((), ()))):
    hi, mid, lo = _split3(x)
    return _dot(a_bf, hi, dims) + _dot(a_bf, mid, dims) + _dot(a_bf, lo, dims)


def _dot_exact_rhs(x, b_bf, dims=(((1,), (0,)), ((), ()))):
    hi, mid, lo = _split3(x)
    return _dot(hi, b_bf, dims) + _dot(mid, b_bf, dims) + _dot(lo, b_bf, dims)


def _silu(x):
    return x * jax.nn.sigmoid(x)


def _rms(x):
    return x * lax.rsqrt(jnp.mean(x * x, axis=-1, keepdims=True) + EPS)


def _ada_kernel(ct_ref, w_ref, b_ref, o_ref, sb_ref):
    nb = sb_ref.shape[0]
    d = w_ref.shape[0]
    tn = o_ref.shape[-1]

    @pl.when(pl.program_id(0) == 0)
    def _():
        ct = ct_ref[...]
        st = _silu(ct)
        for b in range(nb):
            sb_ref[b] = jnp.broadcast_to(st[:, b:b + 1], (d, LANES))

    def body(kc, accs):
        r = pl.multiple_of(kc * SUBLANES, SUBLANES)
        w8 = w_ref[pl.ds(r, SUBLANES), :]
        out = []
        for b in range(nb):
            s8 = sb_ref[b, pl.ds(r, SUBLANES), :]
            out.append(accs[b] + w8 * jnp.tile(s8, (1, tn // LANES)))
        return tuple(out)

    accs = lax.fori_loop(0, d // SUBLANES, body,
                         tuple(jnp.zeros((SUBLANES, tn), F32) for _ in range(nb)), unroll=8)
    for b in range(nb):
        o_ref[b:b + 1, :] = jnp.sum(accs[b], axis=0, keepdims=True) + b_ref[...]


def _ada(c, w_ada, b_ada, tn=1024):
    nb, d = c.shape
    n = w_ada.shape[1]
    return pl.pallas_call(
        _ada_kernel,
        grid=(n // tn,),
        in_specs=[pl.BlockSpec((d, nb), lambda j: (0, 0)),
                  pl.BlockSpec((d, tn), lambda j: (0, j)),
                  pl.BlockSpec((1, tn), lambda j: (0, j))],
        out_specs=pl.BlockSpec((nb, tn), lambda j: (0, j)),
        out_shape=jax.ShapeDtypeStruct((nb, n), F32),
        scratch_shapes=[pltpu.VMEM((nb, d, LANES), F32)],
        compiler_params=_params(("arbitrary",)),
    )(c.T, w_ada, b_ada.reshape(1, n))


def _normproj_kernel(x_ref, g_ref, sc_ref, sh_ref, *rest, split, sub):
    if split:
        whi_ref, wlo_ref, o_ref = rest
    else:
        whi_ref, o_ref = rest
    for r0 in range(0, x_ref.shape[0], sub):
        rows = slice(r0, r0 + sub)
        y = _rms(x_ref[rows, :]) * g_ref[...]
        h = y * (1.0 + sc_ref[0]) + sh_ref[0]
        hi = h.astype(BF16)
        acc = _dot(hi, whi_ref[...], NT)
        if split:
            lo = (h - hi.astype(F32)).astype(BF16)
            acc = acc + _dot(lo, whi_ref[...], NT) + _dot(hi, wlo_ref[...], NT)
        o_ref[rows, :] = acc.astype(o_ref.dtype)


def _resident(shape):
    return pl.BlockSpec(shape, lambda *_: (0,) * len(shape), pipeline_mode=pl.Buffered(1))


def _normproj(x2, g, scale, shift, ws, seq, tm, sub, out_dtype):
    m, d = x2.shape
    n = ws[0].shape[0]
    per_b = seq // tm
    vec = pl.BlockSpec((1, 1, d), lambda i: (i // per_b, 0, 0))
    return pl.pallas_call(
        functools.partial(_normproj_kernel, split=len(ws) == 2, sub=sub),
        grid=(m // tm,),
        in_specs=[pl.BlockSpec((tm, d), lambda i: (i, 0)),
                  pl.BlockSpec((1, d), lambda i: (0, 0)),
                  vec, vec] + [_resident((n, d))] * len(ws),
        out_specs=pl.BlockSpec((tm, n), lambda i: (i, 0)),
        out_shape=jax.ShapeDtypeStruct((m, n), out_dtype),
        compiler_params=_params(("parallel",)),
    )(x2, g.reshape(1, d), scale, shift, *ws)


def _rel_bucket(dist):
    n = jnp.maximum(dist, 0)
    max_exact = REL_BUCKETS // 2
    nf = jnp.maximum(n, max_exact).astype(F32)
    large = max_exact + (jnp.log(nf / max_exact) / math.log(REL_MAX_DIST / max_exact)
                         * (REL_BUCKETS - max_exact)).astype(jnp.int32)
    large = jnp.minimum(large, REL_BUCKETS - 1)
    return jnp.where(n < max_exact, n, large)


def _moba_units(nblk, grp):
    lead = grp - 1
    units = []
    for qi in range(nblk):
        units.append((qi, qi, 1))
        for t in range(-(-max(qi - lead, 0) // grp)):
            units.append((qi, lead + t * grp, 0))
    return units


def _moba_kernel(uq_ref, uk_ref, uf_ref, rb_ref, q_ref, k_ref, v_ref, o_ref,
                 qbf_ref, kbf_ref, vt_ref, tab_ref, mask_ref, s_buf, p_buf, alpha_buf, m_ref, l_ref, acc_ref,
                 *, units):
    h = pl.program_id(1)
    seq = k_ref.shape[1]
    blk = MOBA_BLOCK
    grp = MOBA_FAR_GROUP
    lead = grp - 1
    nblk = seq // blk
    nunits = len(units)
    scale = HEAD_DIM ** -0.5

    kf = k_ref[0]
    qf = q_ref[0]
    qbf_ref[...] = qf.astype(BF16)
    kbf_ref[0:lead * blk, :] = jnp.zeros((lead * blk, HEAD_DIM), BF16)
    kbf_ref[lead * blk:, :] = kf.astype(BF16)
    kmean = jnp.mean(kf.reshape(nblk, blk, HEAD_DIM), axis=1)
    for j in range(lead):
        vt_ref[j] = jnp.zeros((HEAD_DIM, blk), BF16)
    for j in range(nblk):
        vt_ref[lead + j] = v_ref[0, j * blk:(j + 1) * blk, :].T.astype(BF16)

    b_far = rb_ref[REL_BUCKETS - 1, h]
    kk = lax.broadcasted_iota(jnp.int32, (blk, blk), 0)
    qq = lax.broadcasted_iota(jnp.int32, (blk, blk), 1)
    tab_ref[0] = jnp.zeros((grp * blk, blk), F32)
    tab_ref[1, 0:(grp - 2) * blk, :] = jnp.zeros(((grp - 2) * blk, blk), F32)
    for pos, dist in ((grp - 2, qq - kk + blk), (grp - 1, qq - kk)):
        bucket = _rel_bucket(dist)
        tab = jnp.zeros((blk, blk), F32)
        for b in range(REL_BUCKETS):
            tab = jnp.where(bucket == b, rb_ref[b, h] - b_far, tab)
        tab_ref[1, pos * blk:(pos + 1) * blk, :] = jnp.where(dist >= 0, tab, NEG)

    q3 = _split3(qf)
    k3 = _split3(kmean)
    gate = jnp.zeros((nblk, seq), F32)
    for a, b in ((0, 0), (0, 1), (1, 0), (1, 1), (0, 2), (2, 0)):
        gate = gate + _dot(k3[a], q3[b], NT)
    nidx = lax.broadcasted_iota(jnp.int32, (nblk, seq), 0)
    qblk = lax.broadcasted_iota(jnp.int32, (nblk, seq), 1) // blk
    nidx_f = nidx.astype(F32)
    avail = jnp.where(nidx < qblk, 1.0, 0.0)
    chosen = jnp.zeros((nblk, seq), F32)
    for _ in range(MOBA_TOPK):
        gm = jnp.where(avail > 0.0, gate, -jnp.inf)
        best = jnp.max(gm, axis=0, keepdims=True)
        first = jnp.min(jnp.where((gm == best) & (avail > 0.0), nidx_f, float(nblk)), axis=0, keepdims=True)
        hit = nidx_f == first
        chosen = jnp.where(hit, 1.0, chosen)
        avail = jnp.where(hit, 0.0, avail)
    add_all = jnp.where(chosen > 0.0, 0.0, NEG)
    for u, (qi, ks, is_first) in enumerate(units):
        cols = slice(qi * blk, (qi + 1) * blk)
        for n in range(grp):
            j = ks + n - lead
            if j == qi and is_first:
                row = jnp.zeros((1, blk), F32)
            elif 0 <= j < (qi if is_first else qi - lead):
                row = add_all[j:j + 1, cols]
            else:
                row = jnp.full((1, blk), NEG, F32)
            mask_ref[u * grp + n] = jnp.broadcast_to(row, (SUBLANES, blk))

    def unit(u):
        uc = jnp.clip(u, 0, nunits - 1)
        return uq_ref[uc], uk_ref[uc], uf_ref[uc], uc

    def scores(slot, u):
        qi, ks, is_first, uc = unit(u)
        qb = qbf_ref[pl.ds(pl.multiple_of(qi * blk, blk), blk), :]
        kg = kbf_ref[pl.ds(pl.multiple_of(ks * blk, blk), grp * blk), :]
        s = _dot(kg, qb, NT) * scale + tab_ref[is_first]
        s_buf[slot] = jnp.concatenate(
            [s[n * blk:(n + 1) * blk] + mask_ref[uc * grp + n][0:1, :] for n in range(grp)], axis=0)

    def softmax(slot, u):
        qi, _, is_first, _ = unit(u)
        st = qi % 2
        s = s_buf[slot]
        m_old = jnp.where(is_first == 1, NEG, m_ref[st])
        l_old = jnp.where(is_first == 1, 0.0, l_ref[st])
        m_new = jnp.maximum(m_old, jnp.max(s, axis=0, keepdims=True))
        alpha = jnp.exp(m_old - m_new)
        p = jnp.exp(s - m_new)
        m_ref[st] = m_new
        l_ref[st] = alpha * l_old + jnp.sum(p, axis=0, keepdims=True)
        alpha_buf[slot] = alpha
        p_buf[slot] = p.astype(BF16)

    def values(slot, u):
        qi, ks, _, _ = unit(u)
        st = qi % 2
        acc = alpha_buf[slot] * acc_ref[st]
        for n in range(grp):
            acc = acc + _dot(vt_ref[ks + n], p_buf[slot, n * blk:(n + 1) * blk, :])
        acc_ref[st] = acc
        o_ref[0, pl.ds(pl.multiple_of(qi * blk, blk), blk), :] = (acc / l_ref[st]).T

    m_ref[...] = jnp.full(m_ref.shape, NEG, F32)
    l_ref[...] = jnp.ones(l_ref.shape, F32)
    acc_ref[...] = jnp.zeros(acc_ref.shape, F32)
    alpha_buf[1] = jnp.ones((1, blk), F32)
    p_buf[1] = jnp.zeros((grp * blk, blk), BF16)
    scores(0, 0)

    def trip(u, carry):
        slot = u % 2
        values(1 - slot, u - 1)
        softmax(slot, u)
        scores(1 - slot, u + 1)
        return carry

    lax.fori_loop(0, nunits, trip, 0)
    values((nunits - 1) % 2, nunits - 1)


def _moba(rel_bias, qk3, rest3):
    bsz, seq, _ = qk3.shape
    blk = MOBA_BLOCK
    grp = MOBA_FAR_GROUP
    nblk = seq // blk
    assert seq % blk == 0 and grp >= 2
    units = _moba_units(nblk, grp)
    uq, uk, uf = (jnp.asarray([u[c] for u in units], jnp.int32) for c in range(3))
    smem = pl.BlockSpec(memory_space=pltpu.SMEM)
    head = lambda col0: pl.BlockSpec((1, seq, HEAD_DIM), lambda b, h: (b, 0, col0 + h))
    return pl.pallas_call(
        functools.partial(_moba_kernel, units=units),
        grid=(bsz, N_HEADS),
        in_specs=[smem, smem, smem, smem, head(0), head(N_HEADS), head(0)],
        out_specs=head(0),
        out_shape=jax.ShapeDtypeStruct((bsz, seq, ATTN_WIDTH), F32),
        scratch_shapes=[pltpu.VMEM((seq, HEAD_DIM), BF16),
                        pltpu.VMEM((seq + (grp - 1) * blk, HEAD_DIM), BF16),
                        pltpu.VMEM((nblk + grp - 1, HEAD_DIM, blk), BF16),
                        pltpu.VMEM((2, grp * blk, blk), F32),
                        pltpu.VMEM((len(units) * grp, SUBLANES, blk), F32),
                        pltpu.VMEM((2, grp * blk, blk), F32),
                        pltpu.VMEM((2, grp * blk, blk), BF16),
                        pltpu.VMEM((2, 1, blk), F32),
                        pltpu.VMEM((2, 1, blk), F32),
                        pltpu.VMEM((2, 1, blk), F32),
                        pltpu.VMEM((2, HEAD_DIM, blk), F32)],
        compiler_params=_params(("arbitrary", "arbitrary")),
    )(uq, uk, uf, rel_bias, qk3, qk3, rest3)


def _ssd_kernel(xs_ref, z_ref, bc_ref, dt_ref, cwx_ref, cwbc_ref, cbx_ref, cbbc_ref,
                dtb_ref, alog_ref, dsk_ref, ng_ref, tri_ref, trit_ref, exp_ref, o_ref,
                xext_ref, bcext_ref, state_ref):
    c = pl.program_id(1)
    t = CHUNK
    pad = SUBLANES

    @pl.when(c == 0)
    def _():
        xext_ref[0:pad, :] = jnp.zeros((pad, SSM_WIDTH), F32)
        bcext_ref[0:pad, :] = jnp.zeros((pad, 2 * N_GROUPS * D_STATE), F32)
        state_ref[...] = jnp.zeros_like(state_ref)

    def conv_silu(ext_ref, src_ref, w_ref, b_ref):
        ext_ref[pad:pad + t, :] = src_ref[0]
        acc = b_ref[...]
        for k in range(SSM_CONV):
            off = pad - (SSM_CONV - 1) + k
            acc = acc + w_ref[k:k + 1, :] * ext_ref[off:off + t, :]
        ext_ref[0:pad, :] = ext_ref[t:t + pad, :]
        return _silu(acc)

    xh = conv_silu(xext_ref, xs_ref, cwx_ref, cbx_ref)
    bc = conv_silu(bcext_ref, bc_ref, cwbc_ref, cbbc_ref)

    dtr = dt_ref[0] + dtb_ref[...]
    dt = jnp.maximum(dtr, 0.0) + jnp.log1p(jnp.exp(-jnp.abs(dtr)))
    adt = dt * (-jnp.exp(alog_ref[...]))
    acs = _dot_exact_lhs(tri_ref[...], adt)
    acs_t = _dot_exact_rhs(adt.T, trit_ref[...])

    stack = jnp.concatenate([dt, jnp.exp(acs), jnp.exp(acs[t - 1:t, :] - acs)], axis=0)
    wide = _dot_exact_rhs(stack, exp_ref[...])
    dt_x, eacs_x, dst_x = wide[0:t], wide[t:2 * t], wide[2 * t:3 * t]

    xdt = xh * dt_x
    xdt_bf = xdt.astype(BF16)
    xdec_bf = (xdt * dst_x).astype(BF16)

    row = lax.broadcasted_iota(jnp.int32, (t, t), 0)
    col = lax.broadcasted_iota(jnp.int32, (t, t), 1)
    tril = row >= col
    lane = lax.broadcasted_iota(jnp.int32, (t, LANES), 1)
    heads_per_group = N_SSM_HEADS // N_GROUPS
    nt = (((1,), (1,)), ((), ()))
    tn = (((0,), (0,)), ((), ()))

    y_parts = []
    for g in range(N_GROUPS):
        bg = bc[:, g * D_STATE:(g + 1) * D_STATE].astype(BF16)
        cg = bc[:, (N_GROUPS + g) * D_STATE:(N_GROUPS + g + 1) * D_STATE].astype(BF16)
        cb = _dot(cg, bg, nt)
        for pair in range(heads_per_group // 2):
            ms = []
            for r in (g * heads_per_group + 2 * pair, g * heads_per_group + 2 * pair + 1):
                seg = acs[:, r:r + 1] - acs_t[r:r + 1, :]
                ms.append((cb * jnp.exp(jnp.where(tril, seg, NEG))).astype(BF16))
            q = g * (heads_per_group // 2) + pair
            y2 = _dot(jnp.concatenate(ms, axis=0), xdt_bf[:, q * LANES:(q + 1) * LANES])
            y_parts.append(jnp.where(lane < SSM_HEAD_DIM, y2[0:t], y2[t:2 * t]))
    y = jnp.concatenate(y_parts, axis=1)

    off_parts = []
    for g in range(N_GROUPS):
        sl = slice(g * GROUP_WIDTH, (g + 1) * GROUP_WIDTH)
        bg = bc[:, g * D_STATE:(g + 1) * D_STATE].astype(BF16)
        cg = bc[:, (N_GROUPS + g) * D_STATE:(N_GROUPS + g + 1) * D_STATE].astype(BF16)
        st = state_ref[:, sl]
        off_parts.append(_dot(cg, st.astype(BF16)))
        state_ref[:, sl] = eacs_x[t - 1:t, sl] * st + _dot(bg, xdec_bf[:, sl], tn)
    y = y + jnp.concatenate(off_parts, axis=1) * eacs_x + dsk_ref[...] * xh
    y = y * _silu(z_ref[0])

    outs = []
    for g in range(N_GROUPS):
        sl = slice(g * GROUP_WIDTH, (g + 1) * GROUP_WIDTH)
        outs.append(_rms(y[:, sl]) * ng_ref[:, sl])
    o_ref[0] = jnp.concatenate(outs, axis=1).astype(o_ref.dtype)


def _ssd(rest3, conv_w, conv_b, dt_bias, a_log, d_skip, norm_g):
    bsz, seq, _ = rest3.shape
    t = CHUNK
    gn2 = 2 * N_GROUPS * D_STATE
    pad_h = LANES - N_SSM_HEADS
    tri = jnp.tril(jnp.ones((t, t), F32)).astype(BF16)
    expand = jnp.pad(jnp.repeat(jnp.eye(N_SSM_HEADS, dtype=F32), SSM_HEAD_DIM, axis=1),
                     ((0, pad_h), (0, 0))).astype(BF16)
    const = lambda shape: pl.BlockSpec(shape, lambda b, c: (0,) * len(shape))
    return pl.pallas_call(
        _ssd_kernel,
        grid=(bsz, seq // t),
        in_specs=[pl.BlockSpec((1, t, SSM_WIDTH), lambda b, c: (b, c, 1)),
                  pl.BlockSpec((1, t, SSM_WIDTH), lambda b, c: (b, c, 2)),
                  pl.BlockSpec((1, t, gn2), lambda b, c: (b, c, 3 * SSM_WIDTH // gn2)),
                  pl.BlockSpec((1, t, LANES), lambda b, c: (b, c, (3 * SSM_WIDTH + gn2) // LANES)),
                  const((SSM_CONV, SSM_WIDTH)), const((SSM_CONV, gn2)),
                  const((1, SSM_WIDTH)), const((1, gn2)),
                  const((1, LANES)), const((1, LANES)),
                  const((1, SSM_WIDTH)), const((1, SSM_WIDTH)),
                  const((t, t)), const((t, t)), const((LANES, SSM_WIDTH))],
        out_specs=pl.BlockSpec((1, t, SSM_WIDTH), lambda b, c: (b, c, 0)),
        out_shape=jax.ShapeDtypeStruct((bsz, seq, SSM_WIDTH), BF16),
        scratch_shapes=[pltpu.VMEM((t + 2 * SUBLANES, SSM_WIDTH), F32),
                        pltpu.VMEM((t + 2 * SUBLANES, gn2), F32),
                        pltpu.VMEM((D_STATE, SSM_WIDTH), F32)],
        compiler_params=_params(("arbitrary", "arbitrary")),
    )(rest3, rest3, rest3, rest3,
      conv_w[:, :SSM_WIDTH], conv_w[:, SSM_WIDTH:],
      conv_b[:SSM_WIDTH].reshape(1, -1), conv_b[SSM_WIDTH:].reshape(1, -1),
      jnp.pad(dt_bias, (0, pad_h)).reshape(1, LANES), jnp.pad(a_log, (0, pad_h)).reshape(1, LANES),
      jnp.repeat(d_skip, SSM_HEAD_DIM).reshape(1, SSM_WIDTH), norm_g.reshape(1, SSM_WIDTH),
      tri, tri.T, expand)


def _outproj_kernel(attn_ref, ssm_ref, x_ref, ag_ref, w_ref, gm_ref, g2_ref, sc_ref, sh_ref,
                    x1_ref, h2_ref, *, sub):
    for r0 in range(0, x_ref.shape[0], sub):
        rows = slice(r0, r0 + sub)
        a = _rms(attn_ref[rows, :]) * ag_ref[...]
        lhs = jnp.concatenate([a.astype(BF16), ssm_ref[rows, :]], axis=-1)
        x1 = x_ref[rows, :] + gm_ref[0] * _dot(lhs, w_ref[...])
        x1_ref[rows, :] = x1
        y = _rms(x1) * g2_ref[...]
        h2_ref[rows, :] = (y * (1.0 + sc_ref[0]) + sh_ref[0]).astype(BF16)


def _outproj(attn2, ssm2, x2, attn_g, w_out_bf, gate_m, g2, scale_f, shift_f, seq, tm, sub):
    m, d = x2.shape
    per_b = seq // tm
    vec = pl.BlockSpec((1, 1, d), lambda i: (i // per_b, 0, 0))
    half = pl.BlockSpec((tm, ATTN_WIDTH), lambda i: (i, 0))
    full = pl.BlockSpec((tm, d), lambda i: (i, 0))
    return pl.pallas_call(
        functools.partial(_outproj_kernel, sub=sub),
        grid=(m // tm,),
        in_specs=[half, half, full,
                  pl.BlockSpec((1, ATTN_WIDTH), lambda i: (0, 0)),
                  _resident((d, d)),
                  vec, pl.BlockSpec((1, d), lambda i: (0, 0)), vec, vec],
        out_specs=[full, full],
        out_shape=[jax.ShapeDtypeStruct((m, d), F32), jax.ShapeDtypeStruct((m, d), BF16)],
        compiler_params=_params(("parallel",)),
    )(attn2, ssm2, x2, attn_g.reshape(1, -1), w_out_bf, gate_m, g2.reshape(1, d), scale_f, shift_f)


def _ffn_up_kernel(h_ref, wg_ref, wv_ref, cw_ref, cb_ref, o_ref, wb_ref, tail_ref, *, tm, sub, seq):
    i = pl.program_id(1)
    pad = SUBLANES
    tn = o_ref.shape[1]

    @pl.when(i == 0)
    def _():
        wb_ref[:, 0:tn] = wg_ref[...].astype(BF16)
        wb_ref[:, tn:2 * tn] = wv_ref[...].astype(BF16)

    @pl.when((i * tm) % seq == 0)
    def _():
        tail_ref[...] = jnp.zeros((pad, 2 * tn), F32)

    tail = tail_ref[...]
    for r0 in range(0, tm, sub):
        u = _dot(h_ref[r0:r0 + sub, :], wb_ref[...])
        ext = jnp.concatenate([tail, u], axis=0)
        tail = u[sub - pad:sub, :]
        acc = cb_ref[...] + cw_ref[FFN_CONV - 1:FFN_CONV, :] * u
        for back in range(1, FFN_CONV):
            shifted = pltpu.roll(ext, back, axis=0)[pad:pad + sub, :]
            acc = acc + cw_ref[FFN_CONV - 1 - back:FFN_CONV - back, :] * shifted
        o_ref[r0:r0 + sub, :] = (_silu(acc[:, 0:tn]) * acc[:, tn:2 * tn]).astype(o_ref.dtype)
    tail_ref[...] = tail


def _ffn_up(h2, w_up, conv_w, conv_b, seq, tm, sub, tn):
    m, d = h2.shape
    nj = FFN_DIM // tn
    pair = lambda a: jnp.concatenate([a[:, :FFN_DIM].reshape(-1, nj, tn), a[:, FFN_DIM:].reshape(-1, nj, tn)],
                                     axis=2).reshape(-1, 2 * FFN_DIM)
    return pl.pallas_call(
        functools.partial(_ffn_up_kernel, tm=tm, sub=sub, seq=seq),
        grid=(nj, m // tm),
        in_specs=[pl.BlockSpec((tm, d), lambda j, i: (i, 0)),
                  pl.BlockSpec((d, tn), lambda j, i: (0, j)),
                  pl.BlockSpec((d, tn), lambda j, i: (0, j + nj)),
                  pl.BlockSpec((FFN_CONV, 2 * tn), lambda j, i: (0, j)),
                  pl.BlockSpec((1, 2 * tn), lambda j, i: (0, j))],
        out_specs=pl.BlockSpec((tm, tn), lambda j, i: (i, j)),
        out_shape=jax.ShapeDtypeStruct((m, FFN_DIM), BF16),
        scratch_shapes=[pltpu.VMEM((d, 2 * tn), BF16), pltpu.VMEM((SUBLANES, 2 * tn), F32)],
        compiler_params=_params(("arbitrary", "arbitrary")),
    )(h2, w_up, w_up, pair(conv_w), pair(conv_b.reshape(1, -1)))


def _ffn_down_kernel(a_ref, w_ref, x1_ref, gf_ref, fg_ref, o_ref):
    x2 = x1_ref[...] + gf_ref[0] * _dot(a_ref[...], w_ref[...])
    o_ref[...] = _rms(x2) * fg_ref[...]


def _ffn_down(act, w_down_bf, x1, gate_f, final_g, seq, tm):
    m, d = x1.shape
    per_b = seq // tm
    row = lambda width: pl.BlockSpec((tm, width), lambda i: (i, 0))
    return pl.pallas_call(
        _ffn_down_kernel,
        grid=(m // tm,),
        in_specs=[row(FFN_DIM), _resident((FFN_DIM, d)), row(d),
                  pl.BlockSpec((1, 1, d), lambda i: (i // per_b, 0, 0)),
                  pl.BlockSpec((1, d), lambda i: (0, 0))],
        out_specs=row(d),
        out_shape=jax.ShapeDtypeStruct((m, d), F32),
        compiler_params=_params(("parallel",)),
    )(act, w_down_bf, x1, gate_f, final_g.reshape(1, d))


def kernel(x, c, w_ada, b_ada, norm_mix_g, w_in, rel_bias, attn_norm_g, conv_ssm_w, conv_ssm_b, dt_bias,
           a_log, d_skip, ssm_norm_g, w_out, norm_ffn_g, w_up, conv_ffn_w, conv_ffn_b, w_down, final_norm_g):
    bsz, seq, d = x.shape
    m = bsz * seq
    assert w_ada.shape[0] == 1, "the final RMSNorm is fused into the (single) layer's ffn_down kernel"
    l = 0
    x2 = x.reshape(m, d)
    mod = _ada(c, w_ada[l], b_ada[l])
    shift_m, scale_m, gate_m, shift_f, scale_f, gate_f = [
        mod[:, k * d:(k + 1) * d].reshape(bsz, 1, d) for k in range(6)]

    w_in_t = w_in[l].T
    w_qk = w_in_t[:QK_COLS]
    w_qk_hi = w_qk.astype(BF16)
    w_qk_lo = (w_qk - w_qk_hi.astype(F32)).astype(BF16)
    w_rest = jnp.pad(w_in_t[QK_COLS:].astype(BF16), ((0, REST_COLS - (w_in.shape[2] - QK_COLS)), (0, 0)))
    qk = _normproj(x2, norm_mix_g[l], scale_m, shift_m, (w_qk_hi, w_qk_lo), seq, 512, 256, F32)
    rest = _normproj(x2, norm_mix_g[l], scale_m, shift_m, (w_rest,), seq, 512, 256, F32)
    qk3 = qk.reshape(bsz, seq, QK_COLS)
    rest3 = rest.reshape(bsz, seq, REST_COLS)

    attn = _moba(rel_bias, qk3, rest3)
    ssm = _ssd(rest3, conv_ssm_w[l], conv_ssm_b[l], dt_bias[l], a_log[l], d_skip[l], ssm_norm_g[l])

    x1, h2 = _outproj(attn.reshape(m, ATTN_WIDTH), ssm.reshape(m, SSM_WIDTH), x2, attn_norm_g[l],
                      w_out[l].astype(BF16), gate_m, norm_ffn_g[l], scale_f, shift_f, seq, 512, 256)
    act = _ffn_up(h2, w_up[l], conv_ffn_w[l], conv_ffn_b[l], seq, 1024, 256, 512)
    out = _ffn_down(act, w_down[l].astype(BF16), x1, gate_f, final_norm_g, seq, 256)
    return out.reshape(bsz, seq, d)
```

```python
import functools
import math

import jax
import jax.numpy as jnp
from jax import lax
from jax.experimental import pallas as pl
from jax.experimental.pallas import tpu as pltpu

F32 = jnp.float32
BF16 = jnp.bfloat16

D_MODEL = 2048
ATTN_WIDTH = 1024
HEAD_DIM = 128
N_HEADS = 8
SSM_WIDTH = 1024
SSM_HEAD_DIM = 64
N_SSM_HEADS = 16
N_GROUPS = 2
GROUP_WIDTH = SSM_WIDTH // N_GROUPS
D_STATE = 128
SSM_CONV = 4
CHUNK = 256
MOBA_BLOCK = 256
MOBA_TOPK = 3
MOBA_FAR_GROUP = 4
REL_BUCKETS = 32
REL_MAX_DIST = 128
FFN_DIM = 5632
FFN_CONV = 3
EPS = 1e-6
NEG = -1e30

LANES = 128
SUBLANES = 8
VMEM_LIMIT = 56 * 1024 * 1024

PROJ_COLS = 5760
COL_XS = 3 * ATTN_WIDTH
COL_Z = COL_XS + SSM_WIDTH
COL_BC = COL_Z + SSM_WIDTH
COL_DT = COL_BC + 2 * N_GROUPS * D_STATE


def _params(sem):
    return pltpu.CompilerParams(dimension_semantics=sem, vmem_limit_bytes=VMEM_LIMIT)


def _split3(x):
    hi = x.astype(BF16)
    r = x - hi.astype(F32)
    mid = r.astype(BF16)
    lo = (r - mid.astype(F32)).astype(BF16)
    return hi, mid, lo


NT = (((1,), (1,)), ((), ()))


def _dot(a, b, dims=(((1,), (0,)), ((), ()))):
    return lax.dot_general(a, b, dims, preferred_element_type=F32)


def _dot_exact_lhs(a_bf, x, dims=(((1,), (0,)), ((), ()))):
    hi, mid, lo = _split3(x)
    return _dot(a_bf, hi, dims) + _dot(a_bf, mid, dims) + _dot(a_bf, lo, dims)


def _dot_exact_rhs(x, b_bf, dims=(((1,), (0,)), ((), ()))):
    hi, mid, lo = _split3(x)
    return _dot(hi, b_bf, dims) + _dot(mid, b_bf, dims) + _dot(lo, b_bf, dims)


def _silu(x):
    return x * jax.nn.sigmoid(x)


def _rms(x):
    return x * lax.rsqrt(jnp.mean(x * x, axis=-1, keepdims=True) + EPS)


def _ada_kernel(ct_ref, w_ref, b_ref, o_ref, sb_ref):
    nb = sb_ref.shape[0]
    d = w_ref.shape[0]
    tn = o_ref.shape[-1]

    @pl.when(pl.program_id(0) == 0)
    def _():
        ct = ct_ref[...]
        st = _silu(ct)
        for b in range(nb):
            sb_ref[b] = jnp.broadcast_to(st[:, b:b + 1], (d, LANES))

    def body(kc, accs):
        r = pl.multiple_of(kc * SUBLANES, SUBLANES)
        w8 = w_ref[pl.ds(r, SUBLANES), :]
        out = []
        for b in range(nb):
            s8 = sb_ref[b, pl.ds(r, SUBLANES), :]
            out.append(accs[b] + w8 * jnp.tile(s8, (1, tn // LANES)))
        return tuple(out)

    accs = lax.fori_loop(0, d // SUBLANES, body,
                         tuple(jnp.zeros((SUBLANES, tn), F32) for _ in range(nb)), unroll=8)
    for b in range(nb):
        o_ref[b:b + 1, :] = jnp.sum(accs[b], axis=0, keepdims=True) + b_ref[...]


def _ada(c, w_ada, b_ada, tn=1024):
    nb, d = c.shape
    n = w_ada.shape[1]
    return pl.pallas_call(
        _ada_kernel,
        grid=(n // tn,),
        in_specs=[pl.BlockSpec((d, nb), lambda j: (0, 0)),
                  pl.BlockSpec((d, tn), lambda j: (0, j)),
                  pl.BlockSpec((1, tn), lambda j: (0, j))],
        out_specs=pl.BlockSpec((nb, tn), lambda j: (0, j)),
        out_shape=jax.ShapeDtypeStruct((nb, n), F32),
        scratch_shapes=[pltpu.VMEM((nb, d, LANES), F32)],
        compiler_params=_params(("arbitrary",)),
    )(c.T, w_ada, b_ada.reshape(1, n))


def _normproj_kernel(x_ref, g_ref, sc_ref, sh_ref, *rest, split, sub):
    if split:
        whi_ref, wlo_ref, o_ref = rest
    else:
        whi_ref, o_ref = rest
    for r0 in range(0, x_ref.shape[0], sub):
        rows = slice(r0, r0 + sub)
        y = _rms(x_ref[rows, :]) * g_ref[...]
        h = y * (1.0 + sc_ref[0]) + sh_ref[0]
        hi = h.astype(BF16)
        acc = _dot(hi, whi_ref[...], NT)
        if split:
            lo = (h - hi.astype(F32)).astype(BF16)
            acc = acc + _dot(lo, whi_ref[...], NT) + _dot(hi, wlo_ref[...], NT)
        o_ref[rows, :] = acc.astype(o_ref.dtype)


def _resident(shape):
    return pl.BlockSpec(shape, lambda *_: (0,) * len(shape), pipeline_mode=pl.Buffered(1))


def _normproj(x2, g, scale, shift, ws, seq, tm, sub, out_dtype):
    m, d = x2.shape
    n = ws[0].shape[0]
    per_b = seq // tm
    vec = pl.BlockSpec((1, 1, d), lambda i: (i // per_b, 0, 0))
    return pl.pallas_call(
        functools.partial(_normproj_kernel, split=len(ws) == 2, sub=sub),
        grid=(m // tm,),
        in_specs=[pl.BlockSpec((tm, d), lambda i: (i, 0)),
                  pl.BlockSpec((1, d), lambda i: (0, 0)),
                  vec, vec] + [_resident((n, d))] * len(ws),
        out_specs=pl.BlockSpec((tm, n), lambda i: (i, 0)),
        out_shape=jax.ShapeDtypeStruct((m, n), out_dtype),
        compiler_params=_params(("parallel",)),
    )(x2, g.reshape(1, d), scale, shift, *ws)


def _rel_bucket(dist):
    n = jnp.maximum(dist, 0)
    max_exact = REL_BUCKETS // 2
    nf = jnp.maximum(n, max_exact).astype(F32)
    large = max_exact + (jnp.log(nf / max_exact) / math.log(REL_MAX_DIST / max_exact)
                         * (REL_BUCKETS - max_exact)).astype(jnp.int32)
    large = jnp.minimum(large, REL_BUCKETS - 1)
    return jnp.where(n < max_exact, n, large)


def _moba_units(nblk, grp):
    lead = grp - 1
    units = []
    for qi in range(nblk):
        units.append((qi, qi, 1))
        for t in range(-(-max(qi - lead, 0) // grp)):
            units.append((qi, lead + t * grp, 0))
    return units


def _moba_kernel(uq_ref, uk_ref, uf_ref, rb_ref, q_ref, k_ref, v_ref, o_ref,
                 qbf_ref, kbf_ref, vt_ref, tab_ref, mask_ref, s_buf, p_buf, alpha_buf, m_ref, l_ref, acc_ref,
                 *, units):
    h = pl.program_id(1)
    seq = k_ref.shape[1]
    blk = MOBA_BLOCK
    grp = MOBA_FAR_GROUP
    lead = grp - 1
    nblk = seq // blk
    nunits = len(units)
    scale = HEAD_DIM ** -0.5
    exp2_scale = scale * math.log2(math.e)

    kf = k_ref[0]
    qf = q_ref[0]
    qbf_ref[...] = qf.astype(BF16)
    kbf_ref[0:lead * blk, :] = jnp.zeros((lead * blk, HEAD_DIM), BF16)
    kbf_ref[lead * blk:, :] = kf.astype(BF16)
    kmean = jnp.mean(kf.reshape(nblk, blk, HEAD_DIM), axis=1)
    for j in range(lead):
        vt_ref[j] = jnp.zeros((HEAD_DIM, blk), BF16)
    for j in range(nblk):
        vt_ref[lead + j] = v_ref[0, j * blk:(j + 1) * blk, :].T.astype(BF16)

    b_far = rb_ref[REL_BUCKETS - 1, h]
    kk = lax.broadcasted_iota(jnp.int32, (blk, blk), 0)
    qq = lax.broadcasted_iota(jnp.int32, (blk, blk), 1)
    tab_ref[0] = jnp.zeros((2 * blk, blk), F32)
    for pos, dist in ((0, qq - kk + blk), (1, qq - kk)):
        bucket = _rel_bucket(dist)
        tab = jnp.zeros((blk, blk), F32)
        for b in range(REL_BUCKETS):
            tab = jnp.where(bucket == b, (rb_ref[b, h] - b_far) / scale, tab)
        tab_ref[1, pos * blk:(pos + 1) * blk, :] = jnp.where(dist >= 0, tab, NEG)

    q3 = _split3(qf)
    k3 = _split3(kmean)
    gate = jnp.zeros((nblk, seq), F32)
    for a, b in ((0, 0), (0, 1), (1, 0), (1, 1), (0, 2), (2, 0)):
        gate = gate + _dot(k3[a], q3[b], NT)
    nidx = lax.broadcasted_iota(jnp.int32, (nblk, seq), 0)
    qblk = lax.broadcasted_iota(jnp.int32, (nblk, seq), 1) // blk
    nidx_f = nidx.astype(F32)
    avail = jnp.where(nidx < qblk, 1.0, 0.0)
    chosen = jnp.zeros((nblk, seq), F32)
    for _ in range(MOBA_TOPK):
        gm = jnp.where(avail > 0.0, gate, -jnp.inf)
        best = jnp.max(gm, axis=0, keepdims=True)
        first = jnp.min(jnp.where((gm == best) & (avail > 0.0), nidx_f, float(nblk)), axis=0, keepdims=True)
        hit = nidx_f == first
        chosen = jnp.where(hit, 1.0, chosen)
        avail = jnp.where(hit, 0.0, avail)
    add_all = jnp.where(chosen > 0.0, 0.0, NEG)
    for u, (qi, ks, is_first) in enumerate(units):
        cols = slice(qi * blk, (qi + 1) * blk)
        for n in range(grp):
            j = ks + n - lead
            if j == qi and is_first:
                row = jnp.zeros((1, blk), F32)
            elif 0 <= j < (qi if is_first else qi - lead):
                row = add_all[j:j + 1, cols]
            else:
                row = jnp.full((1, blk), NEG, F32)
            mask_ref[u * grp + n] = jnp.broadcast_to(row, (SUBLANES, blk))

    def unit(u):
        uc = jnp.clip(u, 0, nunits - 1)
        return uq_ref[uc], uk_ref[uc], uf_ref[uc], uc

    def scores(slot, u):
        qi, ks, is_first, uc = unit(u)
        qb = qbf_ref[pl.ds(pl.multiple_of(qi * blk, blk), blk), :]
        kg = kbf_ref[pl.ds(pl.multiple_of(ks * blk, blk), grp * blk), :]
        s = _dot(kg, qb, NT)
        tab = tab_ref[is_first]
        parts = []
        for n in range(grp):
            part = s[n * blk:(n + 1) * blk] + mask_ref[uc * grp + n][0:1, :]
            if n >= grp - 2:
                part = part + tab[(n - grp + 2) * blk:(n - grp + 3) * blk]
            parts.append(part)
        s_buf[slot] = jnp.concatenate(parts, axis=0)

    def softmax(slot, u):
        qi, _, is_first, _ = unit(u)
        st = qi % 2
        s = s_buf[slot]
        m_old = jnp.where(is_first == 1, NEG, m_ref[st])
        l_old = jnp.where(is_first == 1, 0.0, l_ref[st])
        m_new = jnp.maximum(m_old, jnp.max(s, axis=0, keepdims=True))
        alpha = jnp.exp2((m_old - m_new) * exp2_scale)
        p = jnp.exp2((s - m_new) * exp2_scale)
        m_ref[st] = m_new
        l_ref[st] = alpha * l_old + jnp.sum(p, axis=0, keepdims=True)
        alpha_buf[slot] = alpha
        p_buf[slot] = p.astype(BF16)

    def values(slot, u):
        qi, ks, _, _ = unit(u)
        st = qi % 2
        acc = alpha_buf[slot] * acc_ref[st]
        for n in range(grp):
            acc = acc + _dot(vt_ref[ks + n], p_buf[slot, n * blk:(n + 1) * blk, :])
        acc_ref[st] = acc
        o_ref[0, pl.ds(pl.multiple_of(qi * blk, blk), blk), :] = (acc / l_ref[st]).T

    m_ref[...] = jnp.full(m_ref.shape, NEG, F32)
    l_ref[...] = jnp.ones(l_ref.shape, F32)
    acc_ref[...] = jnp.zeros(acc_ref.shape, F32)
    alpha_buf[1] = jnp.ones((1, blk), F32)
    p_buf[1] = jnp.zeros((grp * blk, blk), BF16)
    scores(0, 0)

    def trip(u, carry):
        slot = u % 2
        values(1 - slot, u - 1)
        softmax(slot, u)
        scores(1 - slot, u + 1)
        return carry

    lax.fori_loop(0, nunits, trip, 0)
    values((nunits - 1) % 2, nunits - 1)


def _moba(rel_bias, proj3):
    bsz, seq, _ = proj3.shape
    blk = MOBA_BLOCK
    grp = MOBA_FAR_GROUP
    nblk = seq // blk
    assert seq % blk == 0 and grp >= 2
    units = _moba_units(nblk, grp)
    uq, uk, uf = (jnp.asarray([u[c] for u in units], jnp.int32) for c in range(3))
    smem = pl.BlockSpec(memory_space=pltpu.SMEM)
    head = lambda col0: pl.BlockSpec((1, seq, HEAD_DIM), lambda b, h: (b, 0, col0 + h))
    return pl.pallas_call(
        functools.partial(_moba_kernel, units=units),
        grid=(bsz, N_HEADS),
        in_specs=[smem, smem, smem, smem, head(0), head(N_HEADS), head(2 * N_HEADS)],
        out_specs=head(0),
        out_shape=jax.ShapeDtypeStruct((bsz, seq, ATTN_WIDTH), F32),
        scratch_shapes=[pltpu.VMEM((seq, HEAD_DIM), BF16),
                        pltpu.VMEM((seq + (grp - 1) * blk, HEAD_DIM), BF16),
                        pltpu.VMEM((nblk + grp - 1, HEAD_DIM, blk), BF16),
                        pltpu.VMEM((2, 2 * blk, blk), F32),
                        pltpu.VMEM((len(units) * grp, SUBLANES, blk), F32),
                        pltpu.VMEM((2, grp * blk, blk), F32),
                        pltpu.VMEM((2, grp * blk, blk), BF16),
                        pltpu.VMEM((2, 1, blk), F32),
                        pltpu.VMEM((2, 1, blk), F32),
                        pltpu.VMEM((2, 1, blk), F32),
                        pltpu.VMEM((2, HEAD_DIM, blk), F32)],
        compiler_params=_params(("arbitrary", "arbitrary")),
    )(uq, uk, uf, rel_bias, proj3, proj3, proj3)


def _ssd_kernel(xs_ref, z_ref, bc_ref, dt_ref, cwx_ref, cwbc_ref, cbx_ref, cbbc_ref,
                dtb_ref, alog_ref, dsk_ref, ng_ref, tri_ref, trit_ref, exp_ref, o_ref,
                xext_ref, bcext_ref, state_ref):
    c = pl.program_id(1)
    t = CHUNK
    pad = SUBLANES

    @pl.when(c == 0)
    def _():
        xext_ref[0:pad, :] = jnp.zeros((pad, SSM_WIDTH), F32)
        bcext_ref[0:pad, :] = jnp.zeros((pad, 2 * N_GROUPS * D_STATE), F32)
        state_ref[...] = jnp.zeros_like(state_ref)

    def conv_silu(ext_ref, src_ref, w_ref, b_ref):
        ext_ref[pad:pad + t, :] = src_ref[0]
        acc = b_ref[...]
        for k in range(SSM_CONV):
            off = pad - (SSM_CONV - 1) + k
            acc = acc + w_ref[k:k + 1, :] * ext_ref[off:off + t, :]
        ext_ref[0:pad, :] = ext_ref[t:t + pad, :]
        return _silu(acc)

    xh = conv_silu(xext_ref, xs_ref, cwx_ref, cbx_ref)
    bc = conv_silu(bcext_ref, bc_ref, cwbc_ref, cbbc_ref)

    dtr = dt_ref[0] + dtb_ref[...]
    dt = jnp.maximum(dtr, 0.0) + jnp.log1p(jnp.exp(-jnp.abs(dtr)))
    adt = dt * (-jnp.exp(alog_ref[...]))
    acs = _dot_exact_lhs(tri_ref[...], adt)
    acs_t = _dot_exact_rhs(adt.T, trit_ref[...])

    stack = jnp.concatenate([dt, jnp.exp(acs), jnp.exp(acs[t - 1:t, :] - acs)], axis=0)
    wide = _dot_exact_rhs(stack, exp_ref[...])
    dt_x, eacs_x, dst_x = wide[0:t], wide[t:2 * t], wide[2 * t:3 * t]

    xdt = xh * dt_x
    xdt_bf = xdt.astype(BF16)
    xdec_bf = (xdt * dst_x).astype(BF16)

    row = lax.broadcasted_iota(jnp.int32, (t, t), 0)
    col = lax.broadcasted_iota(jnp.int32, (t, t), 1)
    tril = row >= col
    lane = lax.broadcasted_iota(jnp.int32, (t, LANES), 1)
    heads_per_group = N_SSM_HEADS // N_GROUPS
    nt = (((1,), (1,)), ((), ()))
    tn = (((0,), (0,)), ((), ()))

    y_parts = []
    for g in range(N_GROUPS):
        bg = bc[:, g * D_STATE:(g + 1) * D_STATE].astype(BF16)
        cg = bc[:, (N_GROUPS + g) * D_STATE:(N_GROUPS + g + 1) * D_STATE].astype(BF16)
        cb = _dot(cg, bg, nt)
        for pair in range(heads_per_group // 2):
            ms = []
            for r in (g * heads_per_group + 2 * pair, g * heads_per_group + 2 * pair + 1):
                seg = acs[:, r:r + 1] - acs_t[r:r + 1, :]
                ms.append((cb * jnp.exp(jnp.where(tril, seg, NEG))).astype(BF16))
            q = g * (heads_per_group // 2) + pair
            y2 = _dot(jnp.concatenate(ms, axis=0), xdt_bf[:, q * LANES:(q + 1) * LANES])
            y_parts.append(jnp.where(lane < SSM_HEAD_DIM, y2[0:t], y2[t:2 * t]))
    y = jnp.concatenate(y_parts, axis=1)

    off_parts = []
    for g in range(N_GROUPS):
        sl = slice(g * GROUP_WIDTH, (g + 1) * GROUP_WIDTH)
        bg = bc[:, g * D_STATE:(g + 1) * D_STATE].astype(BF16)
        cg = bc[:, (N_GROUPS + g) * D_STATE:(N_GROUPS + g + 1) * D_STATE].astype(BF16)
        st = state_ref[:, sl]
        off_parts.append(_dot(cg, st.astype(BF16)))
        state_ref[:, sl] = eacs_x[t - 1:t, sl] * st + _dot(bg, xdec_bf[:, sl], tn)
    y = y + jnp.concatenate(off_parts, axis=1) * eacs_x + dsk_ref[...] * xh
    y = y * _silu(z_ref[0])

    outs = []
    for g in range(N_GROUPS):
        sl = slice(g * GROUP_WIDTH, (g + 1) * GROUP_WIDTH)
        outs.append(_rms(y[:, sl]) * ng_ref[:, sl])
    o_ref[0] = jnp.concatenate(outs, axis=1).astype(o_ref.dtype)


def _ssd(proj3, conv_w, conv_b, dt_bias, a_log, d_skip, norm_g):
    bsz, seq, _ = proj3.shape
    t = CHUNK
    gn2 = 2 * N_GROUPS * D_STATE
    pad_h = LANES - N_SSM_HEADS
    tri = jnp.tril(jnp.ones((t, t), F32)).astype(BF16)
    expand = jnp.pad(jnp.repeat(jnp.eye(N_SSM_HEADS, dtype=F32), SSM_HEAD_DIM, axis=1),
                     ((0, pad_h), (0, 0))).astype(BF16)
    const = lambda shape: pl.BlockSpec(shape, lambda b, c: (0,) * len(shape))
    return pl.pallas_call(
        _ssd_kernel,
        grid=(bsz, seq // t),
        in_specs=[pl.BlockSpec((1, t, SSM_WIDTH), lambda b, c: (b, c, COL_XS // SSM_WIDTH)),
                  pl.BlockSpec((1, t, SSM_WIDTH), lambda b, c: (b, c, COL_Z // SSM_WIDTH)),
                  pl.BlockSpec((1, t, gn2), lambda b, c: (b, c, COL_BC // gn2)),
                  pl.BlockSpec((1, t, LANES), lambda b, c: (b, c, COL_DT // LANES)),
                  const((SSM_CONV, SSM_WIDTH)), const((SSM_CONV, gn2)),
                  const((1, SSM_WIDTH)), const((1, gn2)),
                  const((1, LANES)), const((1, LANES)),
                  const((1, SSM_WIDTH)), const((1, SSM_WIDTH)),
                  const((t, t)), const((t, t)), const((LANES, SSM_WIDTH))],
        out_specs=pl.BlockSpec((1, t, SSM_WIDTH), lambda b, c: (b, c, 0)),
        out_shape=jax.ShapeDtypeStruct((bsz, seq, SSM_WIDTH), BF16),
        scratch_shapes=[pltpu.VMEM((t + 2 * SUBLANES, SSM_WIDTH), F32),
                        pltpu.VMEM((t + 2 * SUBLANES, gn2), F32),
                        pltpu.VMEM((D_STATE, SSM_WIDTH), F32)],
        compiler_params=_params(("arbitrary", "arbitrary")),
    )(proj3, proj3, proj3, proj3,
      conv_w[:, :SSM_WIDTH], conv_w[:, SSM_WIDTH:],
      conv_b[:SSM_WIDTH].reshape(1, -1), conv_b[SSM_WIDTH:].reshape(1, -1),
      jnp.pad(dt_bias, (0, pad_h)).reshape(1, LANES), jnp.pad(a_log, (0, pad_h)).reshape(1, LANES),
      jnp.repeat(d_skip, SSM_HEAD_DIM).reshape(1, SSM_WIDTH), norm_g.reshape(1, SSM_WIDTH),
      tri, tri.T, expand)


def _outproj_kernel(attn_ref, ssm_ref, x_ref, ag_ref, w_ref, gm_ref, g2_ref, sc_ref, sh_ref,
                    x1_ref, h2_ref, *, sub):
    for r0 in range(0, x_ref.shape[0], sub):
        rows = slice(r0, r0 + sub)
        a = _rms(attn_ref[rows, :]) * ag_ref[...]
        lhs = jnp.concatenate([a.astype(BF16), ssm_ref[rows, :]], axis=-1)
        x1 = x_ref[rows, :] + gm_ref[0] * _dot(lhs, w_ref[...])
        x1_ref[rows, :] = x1
        y = _rms(x1) * g2_ref[...]
        h2_ref[rows, :] = (y * (1.0 + sc_ref[0]) + sh_ref[0]).astype(BF16)


def _outproj(attn2, ssm2, x2, attn_g, w_out_bf, gate_m, g2, scale_f, shift_f, seq, tm, sub):
    m, d = x2.shape
    per_b = seq // tm
    vec = pl.BlockSpec((1, 1, d), lambda i: (i // per_b, 0, 0))
    half = pl.BlockSpec((tm, ATTN_WIDTH), lambda i: (i, 0))
    full = pl.BlockSpec((tm, d), lambda i: (i, 0))
    return pl.pallas_call(
        functools.partial(_outproj_kernel, sub=sub),
        grid=(m // tm,),
        in_specs=[half, half, full,
                  pl.BlockSpec((1, ATTN_WIDTH), lambda i: (0, 0)),
                  _resident((d, d)),
                  vec, pl.BlockSpec((1, d), lambda i: (0, 0)), vec, vec],
        out_specs=[full, full],
        out_shape=[jax.ShapeDtypeStruct((m, d), F32), jax.ShapeDtypeStruct((m, d), BF16)],
        compiler_params=_params(("parallel",)),
    )(attn2, ssm2, x2, attn_g.reshape(1, -1), w_out_bf, gate_m, g2.reshape(1, d), scale_f, shift_f)


def _ffn_up_kernel(h_ref, wg_ref, wv_ref, cw_ref, cb_ref, o_ref, wb_ref, tail_ref, *, tm, sub, seq):
    i = pl.program_id(1)
    pad = SUBLANES
    tn = o_ref.shape[1]

    @pl.when(i == 0)
    def _():
        wb_ref[:, 0:tn] = wg_ref[...].astype(BF16)
        wb_ref[:, tn:2 * tn] = wv_ref[...].astype(BF16)

    @pl.when((i * tm) % seq == 0)
    def _():
        tail_ref[...] = jnp.zeros((pad, 2 * tn), F32)

    tail = tail_ref[...]
    for r0 in range(0, tm, sub):
        u = _dot(h_ref[r0:r0 + sub, :], wb_ref[...])
        ext = jnp.concatenate([tail, u], axis=0)
        tail = u[sub - pad:sub, :]
        acc = cb_ref[...] + cw_ref[FFN_CONV - 1:FFN_CONV, :] * u
        for back in range(1, FFN_CONV):
            shifted = pltpu.roll(ext, back, axis=0)[pad:pad + sub, :]
            acc = acc + cw_ref[FFN_CONV - 1 - back:FFN_CONV - back, :] * shifted
        o_ref[r0:r0 + sub, :] = (_silu(acc[:, 0:tn]) * acc[:, tn:2 * tn]).astype(o_ref.dtype)
    tail_ref[...] = tail


def _ffn_up(h2, w_up, conv_w, conv_b, seq, tm, sub, tn):
    m, d = h2.shape
    nj = FFN_DIM // tn
    pair = lambda a: jnp.concatenate([a[:, :FFN_DIM].reshape(-1, nj, tn), a[:, FFN_DIM:].reshape(-1, nj, tn)],
                                     axis=2).reshape(-1, 2 * FFN_DIM)
    return pl.pallas_call(
        functools.partial(_ffn_up_kernel, tm=tm, sub=sub, seq=seq),
        grid=(nj, m // tm),
        in_specs=[pl.BlockSpec((tm, d), lambda j, i: (i, 0)),
                  pl.BlockSpec((d, tn), lambda j, i: (0, j)),
                  pl.BlockSpec((d, tn), lambda j, i: (0, j + nj)),
                  pl.BlockSpec((FFN_CONV, 2 * tn), lambda j, i: (0, j)),
                  pl.BlockSpec((1, 2 * tn), lambda j, i: (0, j))],
        out_specs=pl.BlockSpec((tm, tn), lambda j, i: (i, j)),
        out_shape=jax.ShapeDtypeStruct((m, FFN_DIM), BF16),
        scratch_shapes=[pltpu.VMEM((d, 2 * tn), BF16), pltpu.VMEM((SUBLANES, 2 * tn), F32)],
        compiler_params=_params(("arbitrary", "arbitrary")),
    )(h2, w_up, w_up, pair(conv_w), pair(conv_b.reshape(1, -1)))


def _ffn_down_kernel(a_ref, w_ref, x1_ref, gf_ref, fg_ref, o_ref):
    x2 = x1_ref[...] + gf_ref[0] * _dot(a_ref[...], w_ref[...])
    o_ref[...] = _rms(x2) * fg_ref[...]


def _ffn_down(act, w_down_bf, x1, gate_f, final_g, seq, tm):
    m, d = x1.shape
    per_b = seq // tm
    row = lambda width: pl.BlockSpec((tm, width), lambda i: (i, 0))
    return pl.pallas_call(
        _ffn_down_kernel,
        grid=(m // tm,),
        in_specs=[row(FFN_DIM), _resident((FFN_DIM, d)), row(d),
                  pl.BlockSpec((1, 1, d), lambda i: (i // per_b, 0, 0)),
                  pl.BlockSpec((1, d), lambda i: (0, 0))],
        out_specs=row(d),
        out_shape=jax.ShapeDtypeStruct((m, d), F32),
        compiler_params=_params(("parallel",)),
    )(act, w_down_bf, x1, gate_f, final_g.reshape(1, d))


def kernel(x, c, w_ada, b_ada, norm_mix_g, w_in, rel_bias, attn_norm_g, conv_ssm_w, conv_ssm_b, dt_bias,
           a_log, d_skip, ssm_norm_g, w_out, norm_ffn_g, w_up, conv_ffn_w, conv_ffn_b, w_down, final_norm_g):
    bsz, seq, d = x.shape
    m = bsz * seq
    assert w_ada.shape[0] == 1, "the final RMSNorm is fused into the (single) layer's ffn_down kernel"
    l = 0
    x2 = x.reshape(m, d)
    mod = _ada(c, w_ada[l], b_ada[l])
    shift_m, scale_m, gate_m, shift_f, scale_f, gate_f = [
        mod[:, k * d:(k + 1) * d].reshape(bsz, 1, d) for k in range(6)]

    w_in_t = w_in[l].T
    w_proj = jnp.pad(w_in_t.astype(BF16), ((0, PROJ_COLS - w_in.shape[2]), (0, 0)))
    proj3 = _normproj(x2, norm_mix_g[l], scale_m, shift_m, (w_proj,), seq, 256, 256, F32).reshape(bsz, seq, PROJ_COLS)

    attn = _moba(rel_bias, proj3)
    ssm = _ssd(proj3, conv_ssm_w[l], conv_ssm_b[l], dt_bias[l], a_log[l], d_skip[l], ssm_norm_g[l])

    x1, h2 = _outproj(attn.reshape(m, ATTN_WIDTH), ssm.reshape(m, SSM_WIDTH), x2, attn_norm_g[l],
                      w_out[l].astype(BF16), gate_m, norm_ffn_g[l], scale_f, shift_f, seq, 512, 256)
    act = _ffn_up(h2, w_up[l], conv_ffn_w[l], conv_ffn_b[l], seq, 1024, 256, 512)
    out = _ffn_down(act, w_down[l].astype(BF16), x1, gate_f, final_norm_g, seq, 256)
    return out.reshape(bsz, seq, d)
```

```python
import functools
import math

import jax
import jax.numpy as jnp
from jax import lax
from jax.experimental import pallas as pl
from jax.experimental.pallas import tpu as pltpu

F32 = jnp.float32
BF16 = jnp.bfloat16

D_MODEL = 2048
ATTN_WIDTH = 1024
HEAD_DIM = 128
N_HEADS = 8
SSM_WIDTH = 1024
SSM_HEAD_DIM = 64
N_SSM_HEADS = 16
N_GROUPS = 2
GROUP_WIDTH = SSM_WIDTH // N_GROUPS
D_STATE = 128
SSM_CONV = 4
CHUNK = 256
MOBA_BLOCK = 256
MOBA_TOPK = 3
MOBA_FAR_GROUP = 4
REL_BUCKETS = 32
REL_MAX_DIST = 128
FFN_DIM = 5632
FFN_CONV = 3
EPS = 1e-6
NEG = -1e30

LANES = 128
SUBLANES = 8
VMEM_LIMIT = 56 * 1024 * 1024

PROJ_COLS = 5760
COL_XS = 3 * ATTN_WIDTH
COL_Z = COL_XS + SSM_WIDTH
COL_BC = COL_Z + SSM_WIDTH
COL_DT = COL_BC + 2 * N_GROUPS * D_STATE


def _params(sem):
    return pltpu.CompilerParams(dimension_semantics=sem, vmem_limit_bytes=VMEM_LIMIT)


def _split3(x):
    hi = x.astype(BF16)
    r = x - hi.astype(F32)
    mid = r.astype(BF16)
    lo = (r - mid.astype(F32)).astype(BF16)
    return hi, mid, lo


NT = (((1,), (1,)), ((), ()))


def _dot(a, b, dims=(((1,), (0,)), ((), ()))):
    return lax.dot_general(a, b, dims, preferred_element_type=F32)


def _dot_exact_lhs(a_bf, x, dims=(((1,), (0,)), ((), ()))):
    hi, mid, lo = _split3(x)
    return _dot(a_bf, hi, dims) + _dot(a_bf, mid, dims) + _dot(a_bf, lo, dims)


def _dot_exact_rhs(x, b_bf, dims=(((1,), (0,)), ((), ()))):
    hi, mid, lo = _split3(x)
    return _dot(hi, b_bf, dims) + _dot(mid, b_bf, dims) + _dot(lo, b_bf, dims)


def _silu(x):
    return x * jax.nn.sigmoid(x)


def _rms(x):
    return x * lax.rsqrt(jnp.mean(x * x, axis=-1, keepdims=True) + EPS)


def _causal_conv(tail, x, w_ref, b):
    rows, width = x.shape
    taps = w_ref.shape[0]
    ext = jnp.concatenate([tail, x], axis=0).reshape(rows // SUBLANES + 1, SUBLANES, width)
    row_in_slab = lax.broadcasted_iota(jnp.int32, (rows // SUBLANES, SUBLANES, width), 1)
    acc = b + w_ref[taps - 1:taps, :] * x
    for back in range(1, taps):
        rot = pltpu.roll(ext, back, axis=1)
        shifted = jnp.where(row_in_slab < back, rot[:-1], rot[1:]).reshape(rows, width)
        acc = acc + w_ref[taps - 1 - back:taps - back, :] * shifted
    return acc


def _ada_kernel(ct_ref, w_ref, b_ref, o_ref, sb_ref):
    nb = sb_ref.shape[0]
    d = w_ref.shape[0]
    tn = o_ref.shape[-1]

    @pl.when(pl.program_id(0) == 0)
    def _():
        ct = ct_ref[...]
        st = _silu(ct)
        for b in range(nb):
            sb_ref[b] = jnp.broadcast_to(st[:, b:b + 1], (d, LANES))

    def body(kc, accs):
        r = pl.multiple_of(kc * SUBLANES, SUBLANES)
        w8 = w_ref[pl.ds(r, SUBLANES), :]
        out = []
        for b in range(nb):
            s8 = sb_ref[b, pl.ds(r, SUBLANES), :]
            out.append(accs[b] + w8 * jnp.tile(s8, (1, tn // LANES)))
        return tuple(out)

    accs = lax.fori_loop(0, d // SUBLANES, body,
                         tuple(jnp.zeros((SUBLANES, tn), F32) for _ in range(nb)), unroll=8)
    for b in range(nb):
        o_ref[b:b + 1, :] = jnp.sum(accs[b], axis=0, keepdims=True) + b_ref[...]


def _ada(c, w_ada, b_ada, tn=1024):
    nb, d = c.shape
    n = w_ada.shape[1]
    return pl.pallas_call(
        _ada_kernel,
        grid=(n // tn,),
        in_specs=[pl.BlockSpec((d, nb), lambda j: (0, 0)),
                  pl.BlockSpec((d, tn), lambda j: (0, j)),
                  pl.BlockSpec((1, tn), lambda j: (0, j))],
        out_specs=pl.BlockSpec((nb, tn), lambda j: (0, j)),
        out_shape=jax.ShapeDtypeStruct((nb, n), F32),
        scratch_shapes=[pltpu.VMEM((nb, d, LANES), F32)],
        compiler_params=_params(("arbitrary",)),
    )(c.T, w_ada, b_ada.reshape(1, n))


def _normproj_kernel(x_ref, g_ref, sc_ref, sh_ref, *rest, split, sub):
    if split:
        whi_ref, wlo_ref, o_ref = rest
    else:
        whi_ref, o_ref = rest
    for r0 in range(0, x_ref.shape[0], sub):
        rows = slice(r0, r0 + sub)
        y = _rms(x_ref[rows, :]) * g_ref[...]
        h = y * (1.0 + sc_ref[0]) + sh_ref[0]
        hi = h.astype(BF16)
        acc = _dot(hi, whi_ref[...], NT)
        if split:
            lo = (h - hi.astype(F32)).astype(BF16)
            acc = acc + _dot(lo, whi_ref[...], NT) + _dot(hi, wlo_ref[...], NT)
        o_ref[rows, :] = acc.astype(o_ref.dtype)


def _resident(shape):
    return pl.BlockSpec(shape, lambda *_: (0,) * len(shape), pipeline_mode=pl.Buffered(1))


def _normproj(x2, g, scale, shift, ws, seq, tm, sub, out_dtype):
    m, d = x2.shape
    n = ws[0].shape[0]
    per_b = seq // tm
    vec = pl.BlockSpec((1, 1, d), lambda i: (i // per_b, 0, 0))
    return pl.pallas_call(
        functools.partial(_normproj_kernel, split=len(ws) == 2, sub=sub),
        grid=(m // tm,),
        in_specs=[pl.BlockSpec((tm, d), lambda i: (i, 0)),
                  pl.BlockSpec((1, d), lambda i: (0, 0)),
                  vec, vec] + [_resident((n, d))] * len(ws),
        out_specs=pl.BlockSpec((tm, n), lambda i: (i, 0)),
        out_shape=jax.ShapeDtypeStruct((m, n), out_dtype),
        compiler_params=_params(("parallel",)),
    )(x2, g.reshape(1, d), scale, shift, *ws)


def _rel_bucket(dist):
    n = jnp.maximum(dist, 0)
    max_exact = REL_BUCKETS // 2
    nf = jnp.maximum(n, max_exact).astype(F32)
    large = max_exact + (jnp.log(nf / max_exact) / math.log(REL_MAX_DIST / max_exact)
                         * (REL_BUCKETS - max_exact)).astype(jnp.int32)
    large = jnp.minimum(large, REL_BUCKETS - 1)
    return jnp.where(n < max_exact, n, large)


def _moba_units(nblk, grp):
    lead = grp - 1
    units = []
    for qi in range(nblk):
        units.append((qi, qi, 1))
        for t in range(-(-max(qi - lead, 0) // grp)):
            units.append((qi, lead + t * grp, 0))
    return units


def _moba_kernel(uq_ref, uk_ref, uf_ref, rb_ref, q_ref, k_ref, v_ref, o_ref,
                 qbf_ref, kbf_ref, vt_ref, tab_ref, mask_ref, s_buf, p_buf, alpha_buf, m_ref, l_ref, acc_ref,
                 *, units):
    h = pl.program_id(1)
    seq = k_ref.shape[1]
    blk = MOBA_BLOCK
    grp = MOBA_FAR_GROUP
    lead = grp - 1
    nblk = seq // blk
    nunits = len(units)
    scale = HEAD_DIM ** -0.5
    exp2_scale = scale * math.log2(math.e)

    kf = k_ref[0]
    qf = q_ref[0]
    qbf_ref[...] = qf.astype(BF16)
    kbf_ref[0:lead * blk, :] = jnp.zeros((lead * blk, HEAD_DIM), BF16)
    kbf_ref[lead * blk:, :] = kf.astype(BF16)
    kmean = jnp.mean(kf.reshape(nblk, blk, HEAD_DIM), axis=1)
    for j in range(lead):
        vt_ref[j] = jnp.zeros((HEAD_DIM, blk), BF16)
    for j in range(nblk):
        vt_ref[lead + j] = v_ref[0, j * blk:(j + 1) * blk, :].T.astype(BF16)

    b_far = rb_ref[REL_BUCKETS - 1, h]
    kk = lax.broadcasted_iota(jnp.int32, (blk, blk), 0)
    qq = lax.broadcasted_iota(jnp.int32, (blk, blk), 1)
    tab_ref[0] = jnp.zeros((2 * blk, blk), F32)
    for pos, dist in ((0, qq - kk + blk), (1, qq - kk)):
        bucket = _rel_bucket(dist)
        tab = jnp.zeros((blk, blk), F32)
        for b in range(REL_BUCKETS):
            tab = jnp.where(bucket == b, (rb_ref[b, h] - b_far) / scale, tab)
        tab_ref[1, pos * blk:(pos + 1) * blk, :] = jnp.where(dist >= 0, tab, NEG)

    q3 = _split3(qf)
    k3 = _split3(kmean)
    gate = jnp.zeros((nblk, seq), F32)
    for a, b in ((0, 0), (0, 1), (1, 0), (1, 1), (0, 2), (2, 0)):
        gate = gate + _dot(k3[a], q3[b], NT)
    nidx = lax.broadcasted_iota(jnp.int32, (nblk, seq), 0)
    qblk = lax.broadcasted_iota(jnp.int32, (nblk, seq), 1) // blk
    nidx_f = nidx.astype(F32)
    avail = jnp.where(nidx < qblk, 1.0, 0.0)
    chosen = jnp.zeros((nblk, seq), F32)
    for _ in range(MOBA_TOPK):
        gm = jnp.where(avail > 0.0, gate, -jnp.inf)
        best = jnp.max(gm, axis=0, keepdims=True)
        first = jnp.min(jnp.where((gm == best) & (avail > 0.0), nidx_f, float(nblk)), axis=0, keepdims=True)
        hit = nidx_f == first
        chosen = jnp.where(hit, 1.0, chosen)
        avail = jnp.where(hit, 0.0, avail)
    add_all = jnp.where(chosen > 0.0, 0.0, NEG)
    for u, (qi, ks, is_first) in enumerate(units):
        cols = slice(qi * blk, (qi + 1) * blk)
        for n in range(grp):
            j = ks + n - lead
            if j == qi and is_first:
                row = jnp.zeros((1, blk), F32)
            elif 0 <= j < (qi if is_first else qi - lead):
                row = add_all[j:j + 1, cols]
            else:
                row = jnp.full((1, blk), NEG, F32)
            mask_ref[u * grp + n] = jnp.broadcast_to(row, (SUBLANES, blk))

    def unit(u):
        uc = jnp.clip(u, 0, nunits - 1)
        return uq_ref[uc], uk_ref[uc], uf_ref[uc], uc

    def scores(slot, u):
        qi, ks, is_first, uc = unit(u)
        qb = qbf_ref[pl.ds(pl.multiple_of(qi * blk, blk), blk), :]
        kg = kbf_ref[pl.ds(pl.multiple_of(ks * blk, blk), grp * blk), :]
        s = _dot(kg, qb, NT)
        tab = tab_ref[is_first]
        parts = []
        for n in range(grp):
            part = s[n * blk:(n + 1) * blk] + mask_ref[uc * grp + n][0:1, :]
            if n >= grp - 2:
                part = part + tab[(n - grp + 2) * blk:(n - grp + 3) * blk]
            parts.append(part)
        s_buf[slot] = jnp.concatenate(parts, axis=0)

    def softmax(slot, u):
        qi, _, is_first, _ = unit(u)
        st = qi % 2
        s = s_buf[slot]
        m_old = jnp.where(is_first == 1, NEG, m_ref[st])
        l_old = jnp.where(is_first == 1, 0.0, l_ref[st])
        m_new = jnp.maximum(m_old, jnp.max(s, axis=0, keepdims=True))
        alpha = jnp.exp2((m_old - m_new) * exp2_scale)
        p = jnp.exp2((s - m_new) * exp2_scale)
        m_ref[st] = m_new
        l_ref[st] = alpha * l_old + jnp.sum(p, axis=0, keepdims=True)
        alpha_buf[slot] = alpha
        p_buf[slot] = p.astype(BF16)

    def values(slot, u):
        qi, ks, _, _ = unit(u)
        st = qi % 2
        acc = alpha_buf[slot] * acc_ref[st]
        for n in range(grp):
            acc = acc + _dot(vt_ref[ks + n], p_buf[slot, n * blk:(n + 1) * blk, :])
        acc_ref[st] = acc
        o_ref[0, pl.ds(pl.multiple_of(qi * blk, blk), blk), :] = (acc / l_ref[st]).T

    m_ref[...] = jnp.full(m_ref.shape, NEG, F32)
    l_ref[...] = jnp.ones(l_ref.shape, F32)
    acc_ref[...] = jnp.zeros(acc_ref.shape, F32)
    alpha_buf[1] = jnp.ones((1, blk), F32)
    p_buf[1] = jnp.zeros((grp * blk, blk), BF16)
    scores(0, 0)

    def trip(u, carry):
        slot = u % 2
        values(1 - slot, u - 1)
        softmax(slot, u)
        scores(1 - slot, u + 1)
        return carry

    lax.fori_loop(0, nunits, trip, 0)
    values((nunits - 1) % 2, nunits - 1)


def _moba(rel_bias, proj3):
    bsz, seq, _ = proj3.shape
    blk = MOBA_BLOCK
    grp = MOBA_FAR_GROUP
    nblk = seq // blk
    assert seq % blk == 0 and grp >= 2
    units = _moba_units(nblk, grp)
    uq, uk, uf = (jnp.asarray([u[c] for u in units], jnp.int32) for c in range(3))
    smem = pl.BlockSpec(memory_space=pltpu.SMEM)
    head = lambda col0: pl.BlockSpec((1, seq, HEAD_DIM), lambda b, h: (b, 0, col0 + h))
    return pl.pallas_call(
        functools.partial(_moba_kernel, units=units),
        grid=(bsz, N_HEADS),
        in_specs=[smem, smem, smem, smem, head(0), head(N_HEADS), head(2 * N_HEADS)],
        out_specs=head(0),
        out_shape=jax.ShapeDtypeStruct((bsz, seq, ATTN_WIDTH), F32),
        scratch_shapes=[pltpu.VMEM((seq, HEAD_DIM), BF16),
                        pltpu.VMEM((seq + (grp - 1) * blk, HEAD_DIM), BF16),
                        pltpu.VMEM((nblk + grp - 1, HEAD_DIM, blk), BF16),
                        pltpu.VMEM((2, 2 * blk, blk), F32),
                        pltpu.VMEM((len(units) * grp, SUBLANES, blk), F32),
                        pltpu.VMEM((2, grp * blk, blk), F32),
                        pltpu.VMEM((2, grp * blk, blk), BF16),
                        pltpu.VMEM((2, 1, blk), F32),
                        pltpu.VMEM((2, 1, blk), F32),
                        pltpu.VMEM((2, 1, blk), F32),
                        pltpu.VMEM((2, HEAD_DIM, blk), F32)],
        compiler_params=_params(("arbitrary", "arbitrary")),
    )(uq, uk, uf, rel_bias, proj3, proj3, proj3)


def _ssd_kernel(xs_ref, z_ref, bc_ref, dt_ref, cwx_ref, cwbc_ref, cbx_ref, cbbc_ref,
                dtb_ref, alog_ref, dsk_ref, ng_ref, tri_ref, trit_ref, exp_ref, o_ref,
                xtail_ref, bctail_ref, state_ref):
    c = pl.program_id(1)
    t = CHUNK
    pad = SUBLANES

    @pl.when(c == 0)
    def _():
        xtail_ref[...] = jnp.zeros_like(xtail_ref)
        bctail_ref[...] = jnp.zeros_like(bctail_ref)
        state_ref[...] = jnp.zeros_like(state_ref)

    def conv_silu(tail_ref, src_ref, w_ref, b_ref):
        x = src_ref[0]
        acc = _causal_conv(tail_ref[...], x, w_ref, b_ref[...])
        tail_ref[...] = x[t - pad:t, :]
        return _silu(acc)

    xh = conv_silu(xtail_ref, xs_ref, cwx_ref, cbx_ref)
    bc = conv_silu(bctail_ref, bc_ref, cwbc_ref, cbbc_ref)

    dtr = dt_ref[0] + dtb_ref[...]
    dt = jnp.maximum(dtr, 0.0) + jnp.log1p(jnp.exp(-jnp.abs(dtr)))
    adt = dt * (-jnp.exp(alog_ref[...]))
    acs = _dot_exact_lhs(tri_ref[...], adt)
    acs_t = _dot_exact_rhs(adt.T, trit_ref[...])

    stack = jnp.concatenate([dt, jnp.exp(acs), jnp.exp(acs[t - 1:t, :] - acs)], axis=0)
    wide = _dot_exact_rhs(stack, exp_ref[...])
    dt_x, eacs_x, dst_x = wide[0:t], wide[t:2 * t], wide[2 * t:3 * t]

    xdt = xh * dt_x
    xdt_bf = xdt.astype(BF16)
    xdec_bf = (xdt * dst_x).astype(BF16)

    row = lax.broadcasted_iota(jnp.int32, (t, t), 0)
    col = lax.broadcasted_iota(jnp.int32, (t, t), 1)
    tril = row >= col
    lane = lax.broadcasted_iota(jnp.int32, (t, LANES), 1)
    heads_per_group = N_SSM_HEADS // N_GROUPS
    nt = (((1,), (1,)), ((), ()))
    tn = (((0,), (0,)), ((), ()))

    y_parts = []
    for g in range(N_GROUPS):
        bg = bc[:, g * D_STATE:(g + 1) * D_STATE].astype(BF16)
        cg = bc[:, (N_GROUPS + g) * D_STATE:(N_GROUPS + g + 1) * D_STATE].astype(BF16)
        cb = _dot(cg, bg, nt)
        for pair in range(heads_per_group // 2):
            ms = []
            for r in (g * heads_per_group + 2 * pair, g * heads_per_group + 2 * pair + 1):
                seg = acs[:, r:r + 1] - acs_t[r:r + 1, :]
                ms.append((cb * jnp.exp(jnp.where(tril, seg, NEG))).astype(BF16))
            q = g * (heads_per_group // 2) + pair
            y2 = _dot(jnp.concatenate(ms, axis=0), xdt_bf[:, q * LANES:(q + 1) * LANES])
            y_parts.append(jnp.where(lane < SSM_HEAD_DIM, y2[0:t], y2[t:2 * t]))
    y = jnp.concatenate(y_parts, axis=1)

    off_parts = []
    for g in range(N_GROUPS):
        sl = slice(g * GROUP_WIDTH, (g + 1) * GROUP_WIDTH)
        bg = bc[:, g * D_STATE:(g + 1) * D_STATE].astype(BF16)
        cg = bc[:, (N_GROUPS + g) * D_STATE:(N_GROUPS + g + 1) * D_STATE].astype(BF16)
        st = state_ref[:, sl]
        off_parts.append(_dot(cg, st.astype(BF16)))
        state_ref[:, sl] = eacs_x[t - 1:t, sl] * st + _dot(bg, xdec_bf[:, sl], tn)
    y = y + jnp.concatenate(off_parts, axis=1) * eacs_x + dsk_ref[...] * xh
    y = y * _silu(z_ref[0])

    outs = []
    for g in range(N_GROUPS):
        sl = slice(g * GROUP_WIDTH, (g + 1) * GROUP_WIDTH)
        outs.append(_rms(y[:, sl]) * ng_ref[:, sl])
    o_ref[0] = jnp.concatenate(outs, axis=1).astype(o_ref.dtype)


def _ssd(proj3, conv_w, conv_b, dt_bias, a_log, d_skip, norm_g):
    bsz, seq, _ = proj3.shape
    t = CHUNK
    gn2 = 2 * N_GROUPS * D_STATE
    pad_h = LANES - N_SSM_HEADS
    tri = jnp.tril(jnp.ones((t, t), F32)).astype(BF16)
    expand = jnp.pad(jnp.repeat(jnp.eye(N_SSM_HEADS, dtype=F32), SSM_HEAD_DIM, axis=1),
                     ((0, pad_h), (0, 0))).astype(BF16)
    const = lambda shape: pl.BlockSpec(shape, lambda b, c: (0,) * len(shape))
    return pl.pallas_call(
        _ssd_kernel,
        grid=(bsz, seq // t),
        in_specs=[pl.BlockSpec((1, t, SSM_WIDTH), lambda b, c: (b, c, COL_XS // SSM_WIDTH)),
                  pl.BlockSpec((1, t, SSM_WIDTH), lambda b, c: (b, c, COL_Z // SSM_WIDTH)),
                  pl.BlockSpec((1, t, gn2), lambda b, c: (b, c, COL_BC // gn2)),
                  pl.BlockSpec((1, t, LANES), lambda b, c: (b, c, COL_DT // LANES)),
                  const((SSM_CONV, SSM_WIDTH)), const((SSM_CONV, gn2)),
                  const((1, SSM_WIDTH)), const((1, gn2)),
                  const((1, LANES)), const((1, LANES)),
                  const((1, SSM_WIDTH)), const((1, SSM_WIDTH)),
                  const((t, t)), const((t, t)), const((LANES, SSM_WIDTH))],
        out_specs=pl.BlockSpec((1, t, SSM_WIDTH), lambda b, c: (b, c, 0)),
        out_shape=jax.ShapeDtypeStruct((bsz, seq, SSM_WIDTH), BF16),
        scratch_shapes=[pltpu.VMEM((SUBLANES, SSM_WIDTH), F32),
                        pltpu.VMEM((SUBLANES, gn2), F32),
                        pltpu.VMEM((D_STATE, SSM_WIDTH), F32)],
        compiler_params=_params(("arbitrary", "arbitrary")),
    )(proj3, proj3, proj3, proj3,
      conv_w[:, :SSM_WIDTH], conv_w[:, SSM_WIDTH:],
      conv_b[:SSM_WIDTH].reshape(1, -1), conv_b[SSM_WIDTH:].reshape(1, -1),
      jnp.pad(dt_bias, (0, pad_h)).reshape(1, LANES), jnp.pad(a_log, (0, pad_h)).reshape(1, LANES),
      jnp.repeat(d_skip, SSM_HEAD_DIM).reshape(1, SSM_WIDTH), norm_g.reshape(1, SSM_WIDTH),
      tri, tri.T, expand)


def _outproj_kernel(attn_ref, ssm_ref, x_ref, ag_ref, w_ref, gm_ref, g2_ref, sc_ref, sh_ref,
                    x1_ref, h2_ref, *, sub):
    for r0 in range(0, x_ref.shape[0], sub):
        rows = slice(r0, r0 + sub)
        a = _rms(attn_ref[rows, :]) * ag_ref[...]
        lhs = jnp.concatenate([a.astype(BF16), ssm_ref[rows, :]], axis=-1)
        x1 = x_ref[rows, :] + gm_ref[0] * _dot(lhs, w_ref[...])
        x1_ref[rows, :] = x1
        y = _rms(x1) * g2_ref[...]
        h2_ref[rows, :] = (y * (1.0 + sc_ref[0]) + sh_ref[0]).astype(BF16)


def _outproj(attn2, ssm2, x2, attn_g, w_out_bf, gate_m, g2, scale_f, shift_f, seq, tm, sub):
    m, d = x2.shape
    per_b = seq // tm
    vec = pl.BlockSpec((1, 1, d), lambda i: (i // per_b, 0, 0))
    half = pl.BlockSpec((tm, ATTN_WIDTH), lambda i: (i, 0))
    full = pl.BlockSpec((tm, d), lambda i: (i, 0))
    return pl.pallas_call(
        functools.partial(_outproj_kernel, sub=sub),
        grid=(m // tm,),
        in_specs=[half, half, full,
                  pl.BlockSpec((1, ATTN_WIDTH), lambda i: (0, 0)),
                  _resident((d, d)),
                  vec, pl.BlockSpec((1, d), lambda i: (0, 0)), vec, vec],
        out_specs=[full, full],
        out_shape=[jax.ShapeDtypeStruct((m, d), F32), jax.ShapeDtypeStruct((m, d), BF16)],
        compiler_params=_params(("parallel",)),
    )(attn2, ssm2, x2, attn_g.reshape(1, -1), w_out_bf, gate_m, g2.reshape(1, d), scale_f, shift_f)


def _ffn_up_kernel(h_ref, wg_ref, wv_ref, cw_ref, cb_ref, o_ref, wb_ref, tail_ref, *, tm, sub, seq):
    i = pl.program_id(1)
    pad = SUBLANES
    tn = o_ref.shape[1]

    @pl.when(i == 0)
    def _():
        wb_ref[:, 0:tn] = wg_ref[...].astype(BF16)
        wb_ref[:, tn:2 * tn] = wv_ref[...].astype(BF16)

    @pl.when((i * tm) % seq == 0)
    def _():
        tail_ref[...] = jnp.zeros((pad, 2 * tn), F32)

    tail = tail_ref[...]
    for r0 in range(0, tm, sub):
        u = _dot(h_ref[r0:r0 + sub, :], wb_ref[...])
        acc = _causal_conv(tail, u, cw_ref, cb_ref[...])
        tail = u[sub - pad:sub, :]
        o_ref[r0:r0 + sub, :] = (_silu(acc[:, 0:tn]) * acc[:, tn:2 * tn]).astype(o_ref.dtype)
    tail_ref[...] = tail


def _ffn_up(h2, w_up, conv_w, conv_b, seq, tm, sub, tn):
    m, d = h2.shape
    nj = FFN_DIM // tn
    pair = lambda a: jnp.concatenate([a[:, :FFN_DIM].reshape(-1, nj, tn), a[:, FFN_DIM:].reshape(-1, nj, tn)],
                                     axis=2).reshape(-1, 2 * FFN_DIM)
    return pl.pallas_call(
        functools.partial(_ffn_up_kernel, tm=tm, sub=sub, seq=seq),
        grid=(nj, m // tm),
        in_specs=[pl.BlockSpec((tm, d), lambda j, i: (i, 0)),
                  pl.BlockSpec((d, tn), lambda j, i: (0, j)),
                  pl.BlockSpec((d, tn), lambda j, i: (0, j + nj)),
                  pl.BlockSpec((FFN_CONV, 2 * tn), lambda j, i: (0, j)),
                  pl.BlockSpec((1, 2 * tn), lambda j, i: (0, j))],
        out_specs=pl.BlockSpec((tm, tn), lambda j, i: (i, j)),
        out_shape=jax.ShapeDtypeStruct((m, FFN_DIM), BF16),
        scratch_shapes=[pltpu.VMEM((d, 2 * tn), BF16), pltpu.VMEM((SUBLANES, 2 * tn), F32)],
        compiler_params=_params(("arbitrary", "arbitrary")),
    )(h2, w_up, w_up, pair(conv_w), pair(conv_b.reshape(1, -1)))


def _ffn_down_kernel(a_ref, w_ref, x1_ref, gf_ref, fg_ref, o_ref):
    x2 = x1_ref[...] + gf_ref[0] * _dot(a_ref[...], w_ref[...])
    o_ref[...] = _rms(x2) * fg_ref[...]


def _ffn_down(act, w_down_bf, x1, gate_f, final_g, seq, tm):
    m, d = x1.shape
    per_b = seq // tm
    row = lambda width: pl.BlockSpec((tm, width), lambda i: (i, 0))
    return pl.pallas_call(
        _ffn_down_kernel,
        grid=(m // tm,),
        in_specs=[row(FFN_DIM), _resident((FFN_DIM, d)), row(d),
                  pl.BlockSpec((1, 1, d), lambda i: (i // per_b, 0, 0)),
                  pl.BlockSpec((1, d), lambda i: (0, 0))],
        out_specs=row(d),
        out_shape=jax.ShapeDtypeStruct((m, d), F32),
        compiler_params=_params(("parallel",)),
    )(act, w_down_bf, x1, gate_f, final_g.reshape(1, d))


def kernel(x, c, w_ada, b_ada, norm_mix_g, w_in, rel_bias, attn_norm_g, conv_ssm_w, conv_ssm_b, dt_bias,
           a_log, d_skip, ssm_norm_g, w_out, norm_ffn_g, w_up, conv_ffn_w, conv_ffn_b, w_down, final_norm_g):
    bsz, seq, d = x.shape
    m = bsz * seq
    assert w_ada.shape[0] == 1, "the final RMSNorm is fused into the (single) layer's ffn_down kernel"
    l = 0
    x2 = x.reshape(m, d)
    mod = _ada(c, w_ada[l], b_ada[l])
    shift_m, scale_m, gate_m, shift_f, scale_f, gate_f = [
        mod[:, k * d:(k + 1) * d].reshape(bsz, 1, d) for k in range(6)]

    w_in_t = w_in[l].T
    w_proj = jnp.pad(w_in_t.astype(BF16), ((0, PROJ_COLS - w_in.shape[2]), (0, 0)))
    proj3 = _normproj(x2, norm_mix_g[l], scale_m, shift_m, (w_proj,), seq, 256, 256, F32).reshape(bsz, seq, PROJ_COLS)

    attn = _moba(rel_bias, proj3)
    ssm = _ssd(proj3, conv_ssm_w[l], conv_ssm_b[l], dt_bias[l], a_log[l], d_skip[l], ssm_norm_g[l])

    x1, h2 = _outproj(attn.reshape(m, ATTN_WIDTH), ssm.reshape(m, SSM_WIDTH), x2, attn_norm_g[l],
                      w_out[l].astype(BF16), gate_m, norm_ffn_g[l], scale_f, shift_f, seq, 512, 256)
    act = _ffn_up(h2, w_up[l], conv_ffn_w[l], conv_ffn_b[l], seq, 1024, 256, 512)
    out = _ffn_down(act, w_down[l].astype(BF16), x1, gate_f, final_norm_g, seq, 256)
    return out.reshape(bsz, seq, d)
```

```python
import functools
import math

import jax
import jax.numpy as jnp
from jax import lax
from jax.experimental import pallas as pl
from jax.experimental.pallas import tpu as pltpu

F32 = jnp.float32
BF16 = jnp.bfloat16

D_MODEL = 2048
ATTN_WIDTH = 1024
HEAD_DIM = 128
N_HEADS = 8
SSM_WIDTH = 1024
SSM_HEAD_DIM = 64
N_SSM_HEADS = 16
N_GROUPS = 2
GROUP_WIDTH = SSM_WIDTH // N_GROUPS
D_STATE = 128
SSM_CONV = 4
CHUNK = 256
MOBA_BLOCK = 256
MOBA_TOPK = 3
MOBA_FAR_GROUP = 4
REL_BUCKETS = 32
REL_MAX_DIST = 128
FFN_DIM = 5632
FFN_CONV = 3
EPS = 1e-6
NEG = -1e30

LANES = 128
SUBLANES = 8
VMEM_LIMIT = 56 * 1024 * 1024

PROJ_COLS = 5760
COL_XS = 3 * ATTN_WIDTH
COL_Z = COL_XS + SSM_WIDTH
COL_BC = COL_Z + SSM_WIDTH
COL_DT = COL_BC + 2 * N_GROUPS * D_STATE


def _params(sem):
    return pltpu.CompilerParams(dimension_semantics=sem, vmem_limit_bytes=VMEM_LIMIT)


def _split3(x):
    hi = x.astype(BF16)
    r = x - hi.astype(F32)
    mid = r.astype(BF16)
    lo = (r - mid.astype(F32)).astype(BF16)
    return hi, mid, lo


NT = (((1,), (1,)), ((), ()))


def _dot(a, b, dims=(((1,), (0,)), ((), ()))):
    return lax.dot_general(a, b, dims, preferred_element_type=F32)


def _dot_exact_lhs(a_bf, x, dims=(((1,), (0,)), ((), ()))):
    hi, mid, lo = _split3(x)
    return _dot(a_bf, hi, dims) + _dot(a_bf, mid, dims) + _dot(a_bf, lo, dims)


def _dot_exact_rhs(x, b_bf, dims=(((1,), (0,)), ((), ()))):
    hi, mid, lo = _split3(x)
    return _dot(hi, b_bf, dims) + _dot(mid, b_bf, dims) + _dot(lo, b_bf, dims)


def _silu(x):
    return x * jax.nn.sigmoid(x)


def _rms(x):
    return x * lax.rsqrt(jnp.mean(x * x, axis=-1, keepdims=True) + EPS)


def _causal_conv(tail, x, w_ref, b):
    rows, width = x.shape
    taps = w_ref.shape[0]
    ext = jnp.concatenate([tail, x], axis=0).reshape(rows // SUBLANES + 1, SUBLANES, width)
    row_in_slab = lax.broadcasted_iota(jnp.int32, (rows // SUBLANES, SUBLANES, width), 1)
    acc = b + w_ref[taps - 1:taps, :] * x
    for back in range(1, taps):
        rot = pltpu.roll(ext, back, axis=1)
        shifted = jnp.where(row_in_slab < back, rot[:-1], rot[1:]).reshape(rows, width)
        acc = acc + w_ref[taps - 1 - back:taps - back, :] * shifted
    return acc


def _ada_kernel(ct_ref, w_ref, b_ref, o_ref, sb_ref):
    nb = sb_ref.shape[0]
    d = w_ref.shape[0]
    tn = o_ref.shape[-1]

    @pl.when(pl.program_id(0) == 0)
    def _():
        ct = ct_ref[...]
        st = _silu(ct)
        for b in range(nb):
            sb_ref[b] = jnp.broadcast_to(st[:, b:b + 1], (d, LANES))

    def body(kc, accs):
        r = pl.multiple_of(kc * SUBLANES, SUBLANES)
        w8 = w_ref[pl.ds(r, SUBLANES), :]
        out = []
        for b in range(nb):
            s8 = sb_ref[b, pl.ds(r, SUBLANES), :]
            out.append(accs[b] + w8 * jnp.tile(s8, (1, tn // LANES)))
        return tuple(out)

    accs = lax.fori_loop(0, d // SUBLANES, body,
                         tuple(jnp.zeros((SUBLANES, tn), F32) for _ in range(nb)), unroll=8)
    for b in range(nb):
        o_ref[b:b + 1, :] = jnp.sum(accs[b], axis=0, keepdims=True) + b_ref[...]


def _ada(c, w_ada, b_ada, tn=1024):
    nb, d = c.shape
    n = w_ada.shape[1]
    return pl.pallas_call(
        _ada_kernel,
        grid=(n // tn,),
        in_specs=[pl.BlockSpec((d, nb), lambda j: (0, 0)),
                  pl.BlockSpec((d, tn), lambda j: (0, j)),
                  pl.BlockSpec((1, tn), lambda j: (0, j))],
        out_specs=pl.BlockSpec((nb, tn), lambda j: (0, j)),
        out_shape=jax.ShapeDtypeStruct((nb, n), F32),
        scratch_shapes=[pltpu.VMEM((nb, d, LANES), F32)],
        compiler_params=_params(("arbitrary",)),
    )(c.T, w_ada, b_ada.reshape(1, n))


def _normproj_kernel(x_ref, g_ref, sc_ref, sh_ref, w_ref, o_ref, *, sub):
    n = w_ref.shape[0]
    for r0 in range(0, x_ref.shape[0], sub):
        rows = slice(r0, r0 + sub)
        y = _rms(x_ref[rows, :]) * g_ref[...]
        h = y * (1.0 + sc_ref[0]) + sh_ref[0]
        o_ref[rows, 0:n] = _dot(h.astype(BF16), w_ref[...], NT)
        if n < o_ref.shape[1]:
            o_ref[rows, n:] = jnp.zeros((sub, o_ref.shape[1] - n), F32)


def _resident(shape):
    return pl.BlockSpec(shape, lambda *_: (0,) * len(shape), pipeline_mode=pl.Buffered(1))


def _normproj(x2, g, scale, shift, w_t, n_out, seq, tm, sub):
    m, d = x2.shape
    per_b = seq // tm
    vec = pl.BlockSpec((1, 1, d), lambda i: (i // per_b, 0, 0))
    return pl.pallas_call(
        functools.partial(_normproj_kernel, sub=sub),
        grid=(m // tm,),
        in_specs=[pl.BlockSpec((tm, d), lambda i: (i, 0)),
                  pl.BlockSpec((1, d), lambda i: (0, 0)),
                  vec, vec, _resident(w_t.shape)],
        out_specs=pl.BlockSpec((tm, n_out), lambda i: (i, 0)),
        out_shape=jax.ShapeDtypeStruct((m, n_out), F32),
        compiler_params=_params(("parallel",)),
    )(x2, g.reshape(1, d), scale, shift, w_t)


def _rel_bucket(dist):
    n = jnp.maximum(dist, 0)
    max_exact = REL_BUCKETS // 2
    nf = jnp.maximum(n, max_exact).astype(F32)
    large = max_exact + (jnp.log(nf / max_exact) / math.log(REL_MAX_DIST / max_exact)
                         * (REL_BUCKETS - max_exact)).astype(jnp.int32)
    large = jnp.minimum(large, REL_BUCKETS - 1)
    return jnp.where(n < max_exact, n, large)


def _moba_units(nblk, grp):
    lead = grp - 1
    units = []
    for qi in range(nblk):
        units.append((qi, qi, 1))
        for t in range(-(-max(qi - lead, 0) // grp)):
            units.append((qi, lead + t * grp, 0))
    return units


def _moba_kernel(uq_ref, uk_ref, uf_ref, rb_ref, q_ref, k_ref, v_ref, o_ref,
                 qbf_ref, kbf_ref, vt_ref, tab_ref, mask_ref, s_buf, p_buf, alpha_buf, m_ref, l_ref, acc_ref,
                 *, units):
    h = pl.program_id(1)
    seq = k_ref.shape[1]
    blk = MOBA_BLOCK
    grp = MOBA_FAR_GROUP
    lead = grp - 1
    nblk = seq // blk
    nunits = len(units)
    scale = HEAD_DIM ** -0.5
    exp2_scale = scale * math.log2(math.e)

    kf = k_ref[0]
    qf = q_ref[0]
    qbf_ref[...] = qf.astype(BF16)
    kbf_ref[0:lead * blk, :] = jnp.zeros((lead * blk, HEAD_DIM), BF16)
    kbf_ref[lead * blk:, :] = kf.astype(BF16)
    kmean = jnp.mean(kf.reshape(nblk, blk, HEAD_DIM), axis=1)
    for j in range(lead):
        vt_ref[j] = jnp.zeros((HEAD_DIM, blk), BF16)
    for j in range(nblk):
        vt_ref[lead + j] = v_ref[0, j * blk:(j + 1) * blk, :].T.astype(BF16)

    b_far = rb_ref[REL_BUCKETS - 1, h]
    kk = lax.broadcasted_iota(jnp.int32, (blk, blk), 0)
    qq = lax.broadcasted_iota(jnp.int32, (blk, blk), 1)
    tab_ref[0] = jnp.zeros((2 * blk, blk), F32)
    for pos, dist in ((0, qq - kk + blk), (1, qq - kk)):
        bucket = _rel_bucket(dist)
        tab = jnp.zeros((blk, blk), F32)
        for b in range(REL_BUCKETS):
            tab = jnp.where(bucket == b, (rb_ref[b, h] - b_far) / scale, tab)
        tab_ref[1, pos * blk:(pos + 1) * blk, :] = jnp.where(dist >= 0, tab, NEG)

    q3 = _split3(qf)
    k3 = _split3(kmean)
    gate = jnp.zeros((nblk, seq), F32)
    for a, b in ((0, 0), (0, 1), (1, 0), (1, 1), (0, 2), (2, 0)):
        gate = gate + _dot(k3[a], q3[b], NT)
    nidx = lax.broadcasted_iota(jnp.int32, (nblk, seq), 0)
    qblk = lax.broadcasted_iota(jnp.int32, (nblk, seq), 1) // blk
    nidx_f = nidx.astype(F32)
    avail = jnp.where(nidx < qblk, 1.0, 0.0)
    chosen = jnp.zeros((nblk, seq), F32)
    for _ in range(MOBA_TOPK):
        gm = jnp.where(avail > 0.0, gate, -jnp.inf)
        best = jnp.max(gm, axis=0, keepdims=True)
        first = jnp.min(jnp.where((gm == best) & (avail > 0.0), nidx_f, float(nblk)), axis=0, keepdims=True)
        hit = nidx_f == first
        chosen = jnp.where(hit, 1.0, chosen)
        avail = jnp.where(hit, 0.0, avail)
    add_all = jnp.where(chosen > 0.0, 0.0, NEG)
    for u, (qi, ks, is_first) in enumerate(units):
        cols = slice(qi * blk, (qi + 1) * blk)
        for n in range(grp):
            j = ks + n - lead
            if j == qi and is_first:
                row = jnp.zeros((1, blk), F32)
            elif 0 <= j < (qi if is_first else qi - lead):
                row = add_all[j:j + 1, cols]
            else:
                row = jnp.full((1, blk), NEG, F32)
            mask_ref[u * grp + n] = jnp.broadcast_to(row, (SUBLANES, blk))

    def unit(u):
        uc = jnp.clip(u, 0, nunits - 1)
        return uq_ref[uc], uk_ref[uc], uf_ref[uc], uc

    def scores(slot, u):
        qi, ks, is_first, uc = unit(u)
        qb = qbf_ref[pl.ds(pl.multiple_of(qi * blk, blk), blk), :]
        kg = kbf_ref[pl.ds(pl.multiple_of(ks * blk, blk), grp * blk), :]
        s = _dot(kg, qb, NT)
        tab = tab_ref[is_first]
        parts = []
        for n in range(grp):
            part = s[n * blk:(n + 1) * blk] + mask_ref[uc * grp + n][0:1, :]
            if n >= grp - 2:
                part = part + tab[(n - grp + 2) * blk:(n - grp + 3) * blk]
            parts.append(part)
        s_buf[slot] = jnp.concatenate(parts, axis=0)

    def softmax(slot, u):
        qi, _, is_first, _ = unit(u)
        st = qi % 2
        s = s_buf[slot]
        m_old = jnp.where(is_first == 1, NEG, m_ref[st])
        l_old = jnp.where(is_first == 1, 0.0, l_ref[st])
        m_new = jnp.maximum(m_old, jnp.max(s, axis=0, keepdims=True))
        alpha = jnp.exp2((m_old - m_new) * exp2_scale)
        p = jnp.exp2((s - m_new) * exp2_scale)
        m_ref[st] = m_new
        l_ref[st] = alpha * l_old + jnp.sum(p, axis=0, keepdims=True)
        alpha_buf[slot] = alpha
        p_buf[slot] = p.astype(BF16)

    def values(slot, u):
        qi, ks, _, _ = unit(u)
        st = qi % 2
        acc = alpha_buf[slot] * acc_ref[st]
        for n in range(grp):
            acc = acc + _dot(vt_ref[ks + n], p_buf[slot, n * blk:(n + 1) * blk, :])
        acc_ref[st] = acc
        o_ref[0, pl.ds(pl.multiple_of(qi * blk, blk), blk), :] = (acc / l_ref[st]).T

    m_ref[...] = jnp.full(m_ref.shape, NEG, F32)
    l_ref[...] = jnp.ones(l_ref.shape, F32)
    acc_ref[...] = jnp.zeros(acc_ref.shape, F32)
    alpha_buf[1] = jnp.ones((1, blk), F32)
    p_buf[1] = jnp.zeros((grp * blk, blk), BF16)
    scores(0, 0)

    def trip(u, carry):
        slot = u % 2
        values(1 - slot, u - 1)
        softmax(slot, u)
        scores(1 - slot, u + 1)
        return carry

    lax.fori_loop(0, nunits, trip, 0)
    values((nunits - 1) % 2, nunits - 1)


def _moba(rel_bias, proj3):
    bsz, seq, _ = proj3.shape
    blk = MOBA_BLOCK
    grp = MOBA_FAR_GROUP
    nblk = seq // blk
    assert seq % blk == 0 and grp >= 2
    units = _moba_units(nblk, grp)
    uq, uk, uf = (jnp.asarray([u[c] for u in units], jnp.int32) for c in range(3))
    smem = pl.BlockSpec(memory_space=pltpu.SMEM)
    head = lambda col0: pl.BlockSpec((1, seq, HEAD_DIM), lambda b, h: (b, 0, col0 + h))
    return pl.pallas_call(
        functools.partial(_moba_kernel, units=units),
        grid=(bsz, N_HEADS),
        in_specs=[smem, smem, smem, smem, head(0), head(N_HEADS), head(2 * N_HEADS)],
        out_specs=head(0),
        out_shape=jax.ShapeDtypeStruct((bsz, seq, ATTN_WIDTH), F32),
        scratch_shapes=[pltpu.VMEM((seq, HEAD_DIM), BF16),
                        pltpu.VMEM((seq + (grp - 1) * blk, HEAD_DIM), BF16),
                        pltpu.VMEM((nblk + grp - 1, HEAD_DIM, blk), BF16),
                        pltpu.VMEM((2, 2 * blk, blk), F32),
                        pltpu.VMEM((len(units) * grp, SUBLANES, blk), F32),
                        pltpu.VMEM((2, grp * blk, blk), F32),
                        pltpu.VMEM((2, grp * blk, blk), BF16),
                        pltpu.VMEM((2, 1, blk), F32),
                        pltpu.VMEM((2, 1, blk), F32),
                        pltpu.VMEM((2, 1, blk), F32),
                        pltpu.VMEM((2, HEAD_DIM, blk), F32)],
        compiler_params=_params(("arbitrary", "arbitrary")),
    )(uq, uk, uf, rel_bias, proj3, proj3, proj3)


def _ssd_kernel(xs_ref, z_ref, bc_ref, dt_ref, cwx_ref, cwbc_ref, cbx_ref, cbbc_ref,
                dtb_ref, alog_ref, dsk_ref, ng_ref, tri_ref, trit_ref, exp_ref, o_ref,
                xtail_ref, bctail_ref, state_ref):
    c = pl.program_id(1)
    t = CHUNK
    pad = SUBLANES

    @pl.when(c == 0)
    def _():
        xtail_ref[...] = jnp.zeros_like(xtail_ref)
        bctail_ref[...] = jnp.zeros_like(bctail_ref)
        state_ref[...] = jnp.zeros_like(state_ref)

    def conv_silu(tail_ref, src_ref, w_ref, b_ref):
        x = src_ref[0]
        acc = _causal_conv(tail_ref[...], x, w_ref, b_ref[...])
        tail_ref[...] = x[t - pad:t, :]
        return _silu(acc)

    xh = conv_silu(xtail_ref, xs_ref, cwx_ref, cbx_ref)
    bc = conv_silu(bctail_ref, bc_ref, cwbc_ref, cbbc_ref)

    dtr = dt_ref[0] + dtb_ref[...]
    dt = jnp.maximum(dtr, 0.0) + jnp.log1p(jnp.exp(-jnp.abs(dtr)))
    adt = dt * (-jnp.exp(alog_ref[...]))
    acs = _dot_exact_lhs(tri_ref[...], adt)
    acs_t = _dot_exact_rhs(adt.T, trit_ref[...])

    stack = jnp.concatenate([dt, jnp.exp(acs), jnp.exp(acs[t - 1:t, :] - acs)], axis=0)
    wide = _dot_exact_rhs(stack, exp_ref[...])
    dt_x, eacs_x, dst_x = wide[0:t], wide[t:2 * t], wide[2 * t:3 * t]

    xdt = xh * dt_x
    xdt_bf = xdt.astype(BF16)
    xdec_bf = (xdt * dst_x).astype(BF16)

    row = lax.broadcasted_iota(jnp.int32, (t, t), 0)
    col = lax.broadcasted_iota(jnp.int32, (t, t), 1)
    tril = row >= col
    lane = lax.broadcasted_iota(jnp.int32, (t, LANES), 1)
    heads_per_group = N_SSM_HEADS // N_GROUPS
    nt = (((1,), (1,)), ((), ()))
    tn = (((0,), (0,)), ((), ()))

    y_parts = []
    for g in range(N_GROUPS):
        bg = bc[:, g * D_STATE:(g + 1) * D_STATE].astype(BF16)
        cg = bc[:, (N_GROUPS + g) * D_STATE:(N_GROUPS + g + 1) * D_STATE].astype(BF16)
        cb = _dot(cg, bg, nt)
        for pair in range(heads_per_group // 2):
            ms = []
            for r in (g * heads_per_group + 2 * pair, g * heads_per_group + 2 * pair + 1):
                seg = acs[:, r:r + 1] - acs_t[r:r + 1, :]
                ms.append((cb * jnp.exp(jnp.where(tril, seg, NEG))).astype(BF16))
            q = g * (heads_per_group // 2) + pair
            y2 = _dot(jnp.concatenate(ms, axis=0), xdt_bf[:, q * LANES:(q + 1) * LANES])
            y_parts.append(jnp.where(lane < SSM_HEAD_DIM, y2[0:t], y2[t:2 * t]))
    y = jnp.concatenate(y_parts, axis=1)

    off_parts = []
    for g in range(N_GROUPS):
        sl = slice(g * GROUP_WIDTH, (g + 1) * GROUP_WIDTH)
        bg = bc[:, g * D_STATE:(g + 1) * D_STATE].astype(BF16)
        cg = bc[:, (N_GROUPS + g) * D_STATE:(N_GROUPS + g + 1) * D_STATE].astype(BF16)
        st = state_ref[:, sl]
        off_parts.append(_dot(cg, st.astype(BF16)))
        state_ref[:, sl] = eacs_x[t - 1:t, sl] * st + _dot(bg, xdec_bf[:, sl], tn)
    y = y + jnp.concatenate(off_parts, axis=1) * eacs_x + dsk_ref[...] * xh
    y = y * _silu(z_ref[0])

    outs = []
    for g in range(N_GROUPS):
        sl = slice(g * GROUP_WIDTH, (g + 1) * GROUP_WIDTH)
        outs.append(_rms(y[:, sl]) * ng_ref[:, sl])
    o_ref[0] = jnp.concatenate(outs, axis=1).astype(o_ref.dtype)


def _ssd(proj3, conv_w, conv_b, dt_bias, a_log, d_skip, norm_g):
    bsz, seq, _ = proj3.shape
    t = CHUNK
    gn2 = 2 * N_GROUPS * D_STATE
    pad_h = LANES - N_SSM_HEADS
    tri = jnp.tril(jnp.ones((t, t), F32)).astype(BF16)
    expand = jnp.pad(jnp.repeat(jnp.eye(N_SSM_HEADS, dtype=F32), SSM_HEAD_DIM, axis=1),
                     ((0, pad_h), (0, 0))).astype(BF16)
    const = lambda shape: pl.BlockSpec(shape, lambda b, c: (0,) * len(shape))
    return pl.pallas_call(
        _ssd_kernel,
        grid=(bsz, seq // t),
        in_specs=[pl.BlockSpec((1, t, SSM_WIDTH), lambda b, c: (b, c, COL_XS // SSM_WIDTH)),
                  pl.BlockSpec((1, t, SSM_WIDTH), lambda b, c: (b, c, COL_Z // SSM_WIDTH)),
                  pl.BlockSpec((1, t, gn2), lambda b, c: (b, c, COL_BC // gn2)),
                  pl.BlockSpec((1, t, LANES), lambda b, c: (b, c, COL_DT // LANES)),
                  const((SSM_CONV, SSM_WIDTH)), const((SSM_CONV, gn2)),
                  const((1, SSM_WIDTH)), const((1, gn2)),
                  const((1, LANES)), const((1, LANES)),
                  const((1, SSM_WIDTH)), const((1, SSM_WIDTH)),
                  const((t, t)), const((t, t)), const((LANES, SSM_WIDTH))],
        out_specs=pl.BlockSpec((1, t, SSM_WIDTH), lambda b, c: (b, c, 0)),
        out_shape=jax.ShapeDtypeStruct((bsz, seq, SSM_WIDTH), BF16),
        scratch_shapes=[pltpu.VMEM((SUBLANES, SSM_WIDTH), F32),
                        pltpu.VMEM((SUBLANES, gn2), F32),
                        pltpu.VMEM((D_STATE, SSM_WIDTH), F32)],
        compiler_params=_params(("arbitrary", "arbitrary")),
    )(proj3, proj3, proj3, proj3,
      conv_w[:, :SSM_WIDTH], conv_w[:, SSM_WIDTH:],
      conv_b[:SSM_WIDTH].reshape(1, -1), conv_b[SSM_WIDTH:].reshape(1, -1),
      jnp.pad(dt_bias, (0, pad_h)).reshape(1, LANES), jnp.pad(a_log, (0, pad_h)).reshape(1, LANES),
      jnp.repeat(d_skip, SSM_HEAD_DIM).reshape(1, SSM_WIDTH), norm_g.reshape(1, SSM_WIDTH),
      tri, tri.T, expand)


def _outproj_kernel(attn_ref, ssm_ref, x_ref, ag_ref, w_ref, gm_ref, g2_ref, sc_ref, sh_ref,
                    x1_ref, h2_ref, *, sub):
    for r0 in range(0, x_ref.shape[0], sub):
        rows = slice(r0, r0 + sub)
        a = _rms(attn_ref[rows, :]) * ag_ref[...]
        lhs = jnp.concatenate([a.astype(BF16), ssm_ref[rows, :]], axis=-1)
        x1 = x_ref[rows, :] + gm_ref[0] * _dot(lhs, w_ref[...])
        x1_ref[rows, :] = x1
        y = _rms(x1) * g2_ref[...]
        h2_ref[rows, :] = (y * (1.0 + sc_ref[0]) + sh_ref[0]).astype(BF16)


def _outproj(attn2, ssm2, x2, attn_g, w_out_bf, gate_m, g2, scale_f, shift_f, seq, tm, sub):
    m, d = x2.shape
    per_b = seq // tm
    vec = pl.BlockSpec((1, 1, d), lambda i: (i // per_b, 0, 0))
    half = pl.BlockSpec((tm, ATTN_WIDTH), lambda i: (i, 0))
    full = pl.BlockSpec((tm, d), lambda i: (i, 0))
    return pl.pallas_call(
        functools.partial(_outproj_kernel, sub=sub),
        grid=(m // tm,),
        in_specs=[half, half, full,
                  pl.BlockSpec((1, ATTN_WIDTH), lambda i: (0, 0)),
                  _resident((d, d)),
                  vec, pl.BlockSpec((1, d), lambda i: (0, 0)), vec, vec],
        out_specs=[full, full],
        out_shape=[jax.ShapeDtypeStruct((m, d), F32), jax.ShapeDtypeStruct((m, d), BF16)],
        compiler_params=_params(("parallel",)),
    )(attn2, ssm2, x2, attn_g.reshape(1, -1), w_out_bf, gate_m, g2.reshape(1, d), scale_f, shift_f)


def _ffn_up_kernel(h_ref, wg_ref, wv_ref, cw_ref, cb_ref, o_ref, wb_ref, tail_ref, *, tm, sub, seq):
    i = pl.program_id(1)
    pad = SUBLANES
    tn = o_ref.shape[1]

    @pl.when(i == 0)
    def _():
        wb_ref[:, 0:tn] = wg_ref[...].astype(BF16)
        wb_ref[:, tn:2 * tn] = wv_ref[...].astype(BF16)

    @pl.when((i * tm) % seq == 0)
    def _():
        tail_ref[...] = jnp.zeros((pad, 2 * tn), F32)

    tail = tail_ref[...]
    for r0 in range(0, tm, sub):
        u = _dot(h_ref[r0:r0 + sub, :], wb_ref[...])
        acc = _causal_conv(tail, u, cw_ref, cb_ref[...])
        tail = u[sub - pad:sub, :]
        o_ref[r0:r0 + sub, :] = (_silu(acc[:, 0:tn]) * acc[:, tn:2 * tn]).astype(o_ref.dtype)
    tail_ref[...] = tail


def _ffn_up(h2, w_up, conv_w, conv_b, seq, tm, sub, tn):
    m, d = h2.shape
    nj = FFN_DIM // tn
    pair = lambda a: jnp.concatenate([a[:, :FFN_DIM].reshape(-1, nj, tn), a[:, FFN_DIM:].reshape(-1, nj, tn)],
                                     axis=2).reshape(-1, 2 * FFN_DIM)
    return pl.pallas_call(
        functools.partial(_ffn_up_kernel, tm=tm, sub=sub, seq=seq),
        grid=(nj, m // tm),
        in_specs=[pl.BlockSpec((tm, d), lambda j, i: (i, 0)),
                  pl.BlockSpec((d, tn), lambda j, i: (0, j)),
                  pl.BlockSpec((d, tn), lambda j, i: (0, j + nj)),
                  pl.BlockSpec((FFN_CONV, 2 * tn), lambda j, i: (0, j)),
                  pl.BlockSpec((1, 2 * tn), lambda j, i: (0, j))],
        out_specs=pl.BlockSpec((tm, tn), lambda j, i: (i, j)),
        out_shape=jax.ShapeDtypeStruct((m, FFN_DIM), BF16),
        scratch_shapes=[pltpu.VMEM((d, 2 * tn), BF16), pltpu.VMEM((SUBLANES, 2 * tn), F32)],
        compiler_params=_params(("arbitrary", "arbitrary")),
    )(h2, w_up, w_up, pair(conv_w), pair(conv_b.reshape(1, -1)))


def _ffn_down_kernel(a_ref, w_ref, x1_ref, gf_ref, fg_ref, o_ref):
    x2 = x1_ref[...] + gf_ref[0] * _dot(a_ref[...], w_ref[...])
    o_ref[...] = _rms(x2) * fg_ref[...]


def _ffn_down(act, w_down_bf, x1, gate_f, final_g, seq, tm):
    m, d = x1.shape
    per_b = seq // tm
    row = lambda width: pl.BlockSpec((tm, width), lambda i: (i, 0))
    return pl.pallas_call(
        _ffn_down_kernel,
        grid=(m // tm,),
        in_specs=[row(FFN_DIM), _resident((FFN_DIM, d)), row(d),
                  pl.BlockSpec((1, 1, d), lambda i: (i // per_b, 0, 0)),
                  pl.BlockSpec((1, d), lambda i: (0, 0))],
        out_specs=row(d),
        out_shape=jax.ShapeDtypeStruct((m, d), F32),
        compiler_params=_params(("parallel",)),
    )(act, w_down_bf, x1, gate_f, final_g.reshape(1, d))


def kernel(x, c, w_ada, b_ada, norm_mix_g, w_in, rel_bias, attn_norm_g, conv_ssm_w, conv_ssm_b, dt_bias,
           a_log, d_skip, ssm_norm_g, w_out, norm_ffn_g, w_up, conv_ffn_w, conv_ffn_b, w_down, final_norm_g):
    bsz, seq, d = x.shape
    m = bsz * seq
    assert w_ada.shape[0] == 1, "the final RMSNorm is fused into the (single) layer's ffn_down kernel"
    l = 0
    x2 = x.reshape(m, d)
    mod = _ada(c, w_ada[l], b_ada[l])
    shift_m, scale_m, gate_m, shift_f, scale_f, gate_f = [
        mod[:, k * d:(k + 1) * d].reshape(bsz, 1, d) for k in range(6)]

    w_in_t = w_in[l].T
    proj3 = _normproj(x2, norm_mix_g[l], scale_m, shift_m, w_in_t.astype(BF16), PROJ_COLS, seq, 256, 256
                      ).reshape(bsz, seq, PROJ_COLS)

    attn = _moba(rel_bias, proj3)
    ssm = _ssd(proj3, conv_ssm_w[l], conv_ssm_b[l], dt_bias[l], a_log[l], d_skip[l], ssm_norm_g[l])

    x1, h2 = _outproj(attn.reshape(m, ATTN_WIDTH), ssm.reshape(m, SSM_WIDTH), x2, attn_norm_g[l],
                      w_out[l].astype(BF16), gate_m, norm_ffn_g[l], scale_f, shift_f, seq, 512, 256)
    act = _ffn_up(h2, w_up[l], conv_ffn_w[l], conv_ffn_b[l], seq, 1024, 256, 512)
    out = _ffn_down(act, w_down[l].astype(BF16), x1, gate_f, final_norm_g, seq, 256)
    return out.reshape(bsz, seq, d)
```

```python
import functools
import math

import jax
import jax.numpy as jnp
from jax import lax
from jax.experimental import pallas as pl
from jax.experimental.pallas import tpu as pltpu

F32 = jnp.float32
BF16 = jnp.bfloat16

D_MODEL = 2048
ATTN_WIDTH = 1024
HEAD_DIM = 128
N_HEADS = 8
SSM_WIDTH = 1024
SSM_HEAD_DIM = 64
N_SSM_HEADS = 16
N_GROUPS = 2
GROUP_WIDTH = SSM_WIDTH // N_GROUPS
D_STATE = 128
SSM_CONV = 4
CHUNK = 256
MOBA_BLOCK = 256
MOBA_TOPK = 3
MOBA_FAR_GROUP = 4
REL_BUCKETS = 32
REL_MAX_DIST = 128
FFN_DIM = 5632
FFN_CONV = 3
EPS = 1e-6
NEG = -1e30

LANES = 128
SUBLANES = 8
VMEM_LIMIT = 56 * 1024 * 1024

PROJ_COLS = 5760
COL_XS = 3 * ATTN_WIDTH
COL_Z = COL_XS + SSM_WIDTH
COL_BC = COL_Z + SSM_WIDTH
COL_DT = COL_BC + 2 * N_GROUPS * D_STATE


def _params(sem):
    return pltpu.CompilerParams(dimension_semantics=sem, vmem_limit_bytes=VMEM_LIMIT)


def _split3(x):
    hi = x.astype(BF16)
    r = x - hi.astype(F32)
    mid = r.astype(BF16)
    lo = (r - mid.astype(F32)).astype(BF16)
    return hi, mid, lo


NT = (((1,), (1,)), ((), ()))


def _dot(a, b, dims=(((1,), (0,)), ((), ()))):
    return lax.dot_general(a, b, dims, preferred_element_type=F32)


def _dot_exact_lhs(a_bf, x, dims=(((1,), (0,)), ((), ()))):
    hi, mid, lo = _split3(x)
    return _dot(a_bf, hi, dims) + _dot(a_bf, mid, dims) + _dot(a_bf, lo, dims)


def _dot_exact_rhs(x, b_bf, dims=(((1,), (0,)), ((), ()))):
    hi, mid, lo = _split3(x)
    return _dot(hi, b_bf, dims) + _dot(mid, b_bf, dims) + _dot(lo, b_bf, dims)


def _silu(x):
    return x * jax.nn.sigmoid(x)


def _rms(x):
    return x * lax.rsqrt(jnp.mean(x * x, axis=-1, keepdims=True) + EPS)


def _causal_conv(tail, x, w_ref, b):
    rows, width = x.shape
    taps = w_ref.shape[0]
    ext = jnp.concatenate([tail, x], axis=0).reshape(rows // SUBLANES + 1, SUBLANES, width)
    row_in_slab = lax.broadcasted_iota(jnp.int32, (rows // SUBLANES, SUBLANES, width), 1)
    acc = b + w_ref[taps - 1:taps, :] * x
    for back in range(1, taps):
        rot = pltpu.roll(ext, back, axis=1)
        shifted = jnp.where(row_in_slab < back, rot[:-1], rot[1:]).reshape(rows, width)
        acc = acc + w_ref[taps - 1 - back:taps - back, :] * shifted
    return acc


def _ada_kernel(ct_ref, w_ref, b_ref, o_ref, sb_ref):
    nb = sb_ref.shape[0]
    d = w_ref.shape[0]
    tn = o_ref.shape[-1]

    @pl.when(pl.program_id(0) == 0)
    def _():
        ct = ct_ref[...]
        st = _silu(ct)
        for b in range(nb):
            sb_ref[b] = jnp.broadcast_to(st[:, b:b + 1], (d, LANES))

    def body(kc, accs):
        r = pl.multiple_of(kc * SUBLANES, SUBLANES)
        w8 = w_ref[pl.ds(r, SUBLANES), :]
        out = []
        for b in range(nb):
            s8 = sb_ref[b, pl.ds(r, SUBLANES), :]
            out.append(accs[b] + w8 * jnp.tile(s8, (1, tn // LANES)))
        return tuple(out)

    accs = lax.fori_loop(0, d // SUBLANES, body,
                         tuple(jnp.zeros((SUBLANES, tn), F32) for _ in range(nb)), unroll=8)
    for b in range(nb):
        o_ref[b:b + 1, :] = jnp.sum(accs[b], axis=0, keepdims=True) + b_ref[...]


def _ada(c, w_ada, b_ada, tn=1024):
    nb, d = c.shape
    n = w_ada.shape[1]
    return pl.pallas_call(
        _ada_kernel,
        grid=(n // tn,),
        in_specs=[pl.BlockSpec((d, nb), lambda j: (0, 0)),
                  pl.BlockSpec((d, tn), lambda j: (0, j)),
                  pl.BlockSpec((1, tn), lambda j: (0, j))],
        out_specs=pl.BlockSpec((nb, tn), lambda j: (0, j)),
        out_shape=jax.ShapeDtypeStruct((nb, n), F32),
        scratch_shapes=[pltpu.VMEM((nb, d, LANES), F32)],
        compiler_params=_params(("arbitrary",)),
    )(c.T, w_ada, b_ada.reshape(1, n))


def _normproj_kernel(x_ref, g_ref, sc_ref, sh_ref, w_ref, wnext_ref, o_ref, wnext_bf_ref, *, sub):
    n = w_ref.shape[0]
    for r0 in range(0, x_ref.shape[0], sub):
        rows = slice(r0, r0 + sub)
        y = _rms(x_ref[rows, :]) * g_ref[...]
        h = y * (1.0 + sc_ref[0]) + sh_ref[0]
        o_ref[rows, 0:n] = _dot(h.astype(BF16), w_ref[...], NT)
        if n < o_ref.shape[1]:
            o_ref[rows, n:] = jnp.zeros((sub, o_ref.shape[1] - n), F32)
    wnext_bf_ref[...] = wnext_ref[...].astype(BF16)


def _resident(shape):
    return pl.BlockSpec(shape, lambda *_: (0,) * len(shape), pipeline_mode=pl.Buffered(1))


def _normproj(x2, g, scale, shift, w_t, n_out, w_next, seq, tm, sub):
    m, d = x2.shape
    per_b = seq // tm
    steps = m // tm
    assert w_next.shape[0] % (steps * 2 * SUBLANES) == 0
    ride = pl.BlockSpec((w_next.shape[0] // steps, w_next.shape[1]), lambda i: (i, 0))
    vec = pl.BlockSpec((1, 1, d), lambda i: (i // per_b, 0, 0))
    return pl.pallas_call(
        functools.partial(_normproj_kernel, sub=sub),
        grid=(m // tm,),
        in_specs=[pl.BlockSpec((tm, d), lambda i: (i, 0)),
                  pl.BlockSpec((1, d), lambda i: (0, 0)),
                  vec, vec, _resident(w_t.shape), ride],
        out_specs=[pl.BlockSpec((tm, n_out), lambda i: (i, 0)), ride],
        out_shape=[jax.ShapeDtypeStruct((m, n_out), F32), jax.ShapeDtypeStruct(w_next.shape, BF16)],
        compiler_params=_params(("parallel",)),
    )(x2, g.reshape(1, d), scale, shift, w_t, w_next)


def _rel_bucket(dist):
    n = jnp.maximum(dist, 0)
    max_exact = REL_BUCKETS // 2
    nf = jnp.maximum(n, max_exact).astype(F32)
    large = max_exact + (jnp.log(nf / max_exact) / math.log(REL_MAX_DIST / max_exact)
                         * (REL_BUCKETS - max_exact)).astype(jnp.int32)
    large = jnp.minimum(large, REL_BUCKETS - 1)
    return jnp.where(n < max_exact, n, large)


def _moba_units(nblk, grp):
    lead = grp - 1
    units = []
    for qi in range(nblk):
        units.append((qi, qi, 1))
        for t in range(-(-max(qi - lead, 0) // grp)):
            units.append((qi, lead + t * grp, 0))
    last = [int(n + 1 == len(units) or units[n + 1][0] != u[0]) for n, u in enumerate(units)]
    return units, last


def _moba_kernel(uq_ref, uk_ref, uf_ref, ul_ref, rb_ref, q_ref, k_ref, v_ref, o_ref,
                 qbf_ref, kbf_ref, vt_ref, tab_ref, mask_ref, s_buf, p_buf, alpha_buf, m_ref, l_ref, acc_ref,
                 *, units):
    h = pl.program_id(1)
    seq = k_ref.shape[1]
    blk = MOBA_BLOCK
    grp = MOBA_FAR_GROUP
    lead = grp - 1
    nblk = seq // blk
    nunits = len(units)
    scale = HEAD_DIM ** -0.5
    exp2_scale = scale * math.log2(math.e)

    kf = k_ref[0]
    qf = q_ref[0]
    qbf_ref[...] = qf.astype(BF16)
    kbf_ref[0:lead * blk, :] = jnp.zeros((lead * blk, HEAD_DIM), BF16)
    kbf_ref[lead * blk:, :] = kf.astype(BF16)
    kmean = jnp.mean(kf.reshape(nblk, blk, HEAD_DIM), axis=1)
    for j in range(lead):
        vt_ref[j] = jnp.zeros((HEAD_DIM, blk), BF16)
    for j in range(nblk):
        vt_ref[lead + j] = v_ref[0, j * blk:(j + 1) * blk, :].T.astype(BF16)

    b_far = rb_ref[REL_BUCKETS - 1, h]
    kk = lax.broadcasted_iota(jnp.int32, (blk, blk), 0)
    qq = lax.broadcasted_iota(jnp.int32, (blk, blk), 1)
    tab_ref[0] = jnp.zeros((2 * blk, blk), F32)
    for pos, dist in ((0, qq - kk + blk), (1, qq - kk)):
        bucket = _rel_bucket(dist)
        tab = jnp.zeros((blk, blk), F32)
        for b in range(REL_BUCKETS):
            tab = jnp.where(bucket == b, (rb_ref[b, h] - b_far) / scale, tab)
        tab_ref[1, pos * blk:(pos + 1) * blk, :] = jnp.where(dist >= 0, tab, NEG)

    q3 = _split3(qf)
    k3 = _split3(kmean)
    gate = jnp.zeros((nblk, seq), F32)
    for a, b in ((0, 0), (0, 1), (1, 0), (1, 1), (0, 2), (2, 0)):
        gate = gate + _dot(k3[a], q3[b], NT)
    nidx = lax.broadcasted_iota(jnp.int32, (nblk, seq), 0)
    qblk = lax.broadcasted_iota(jnp.int32, (nblk, seq), 1) // blk
    nidx_f = nidx.astype(F32)
    avail = jnp.where(nidx < qblk, 1.0, 0.0)
    chosen = jnp.zeros((nblk, seq), F32)
    for _ in range(MOBA_TOPK):
        gm = jnp.where(avail > 0.0, gate, -jnp.inf)
        best = jnp.max(gm, axis=0, keepdims=True)
        first = jnp.min(jnp.where((gm == best) & (avail > 0.0), nidx_f, float(nblk)), axis=0, keepdims=True)
        hit = nidx_f == first
        chosen = jnp.where(hit, 1.0, chosen)
        avail = jnp.where(hit, 0.0, avail)
    add_all = jnp.where(chosen > 0.0, 0.0, NEG)
    for u, (qi, ks, is_first) in enumerate(units):
        cols = slice(qi * blk, (qi + 1) * blk)
        for n in range(grp):
            j = ks + n - lead
            if j == qi and is_first:
                row = jnp.zeros((1, blk), F32)
            elif 0 <= j < (qi if is_first else qi - lead):
                row = add_all[j:j + 1, cols]
            else:
                row = jnp.full((1, blk), NEG, F32)
            mask_ref[u * grp + n] = jnp.broadcast_to(row, (SUBLANES, blk))

    def unit(u):
        uc = jnp.clip(u, 0, nunits - 1)
        return uq_ref[uc], uk_ref[uc], uf_ref[uc], uc

    def scores(slot, u):
        qi, ks, is_first, uc = unit(u)
        qb = qbf_ref[pl.ds(pl.multiple_of(qi * blk, blk), blk), :]
        kg = kbf_ref[pl.ds(pl.multiple_of(ks * blk, blk), grp * blk), :]
        s = _dot(kg, qb, NT)
        tab = tab_ref[is_first]
        parts = []
        for n in range(grp):
            part = s[n * blk:(n + 1) * blk] + mask_ref[uc * grp + n][0:1, :]
            if n >= grp - 2:
                part = part + tab[(n - grp + 2) * blk:(n - grp + 3) * blk]
            parts.append(part)
        s_buf[slot] = jnp.concatenate(parts, axis=0)

    def softmax(slot, u):
        qi, _, is_first, _ = unit(u)
        st = qi % 2
        s = s_buf[slot]
        m_old = jnp.where(is_first == 1, NEG, m_ref[st])
        l_old = jnp.where(is_first == 1, 0.0, l_ref[st])
        m_new = jnp.maximum(m_old, jnp.max(s, axis=0, keepdims=True))
        alpha = jnp.exp2((m_old - m_new) * exp2_scale)
        p = jnp.exp2((s - m_new) * exp2_scale)
        m_ref[st] = m_new
        l_ref[st] = alpha * l_old + jnp.sum(p, axis=0, keepdims=True)
        alpha_buf[slot] = alpha
        p_buf[slot] = p.astype(BF16)

    def values(slot, u):
        qi, ks, _, _ = unit(u)
        st = qi % 2
        acc = alpha_buf[slot] * acc_ref[st]
        for n in range(grp):
            acc = acc + _dot(vt_ref[ks + n], p_buf[slot, n * blk:(n + 1) * blk, :])
        acc_ref[st] = acc

    def finish(u):
        qi, _, _, _ = unit(u)
        st = qi % 2
        o_ref[0, pl.ds(pl.multiple_of(qi * blk, blk), blk), :] = (acc_ref[st] / l_ref[st]).T

    m_ref[...] = jnp.full(m_ref.shape, NEG, F32)
    l_ref[...] = jnp.ones(l_ref.shape, F32)
    acc_ref[...] = jnp.zeros(acc_ref.shape, F32)
    alpha_buf[1] = jnp.ones((1, blk), F32)
    p_buf[1] = jnp.zeros((grp * blk, blk), BF16)
    scores(0, 0)

    def trip(u, carry):
        slot = u % 2
        values(1 - slot, u - 1)
        softmax(slot, u)
        scores(1 - slot, u + 1)

        @pl.when((u >= 1) & (ul_ref[jnp.maximum(u - 1, 0)] == 1))
        def _():
            finish(u - 1)

        return carry

    lax.fori_loop(0, nunits, trip, 0)
    values((nunits - 1) % 2, nunits - 1)
    finish(nunits - 1)


def _moba(rel_bias, proj3):
    bsz, seq, _ = proj3.shape
    blk = MOBA_BLOCK
    grp = MOBA_FAR_GROUP
    nblk = seq // blk
    assert seq % blk == 0 and grp >= 2
    units, last = _moba_units(nblk, grp)
    uq, uk, uf = (jnp.asarray([u[c] for u in units], jnp.int32) for c in range(3))
    ul = jnp.asarray(last, jnp.int32)
    smem = pl.BlockSpec(memory_space=pltpu.SMEM)
    head = lambda col0: pl.BlockSpec((1, seq, HEAD_DIM), lambda b, h: (b, 0, col0 + h))
    return pl.pallas_call(
        functools.partial(_moba_kernel, units=units),
        grid=(bsz, N_HEADS),
        in_specs=[smem, smem, smem, smem, smem, head(0), head(N_HEADS), head(2 * N_HEADS)],
        out_specs=head(0),
        out_shape=jax.ShapeDtypeStruct((bsz, seq, ATTN_WIDTH), F32),
        scratch_shapes=[pltpu.VMEM((seq, HEAD_DIM), BF16),
                        pltpu.VMEM((seq + (grp - 1) * blk, HEAD_DIM), BF16),
                        pltpu.VMEM((nblk + grp - 1, HEAD_DIM, blk), BF16),
                        pltpu.VMEM((2, 2 * blk, blk), F32),
                        pltpu.VMEM((len(units) * grp, SUBLANES, blk), F32),
                        pltpu.VMEM((2, grp * blk, blk), F32),
                        pltpu.VMEM((2, grp * blk, blk), BF16),
                        pltpu.VMEM((2, 1, blk), F32),
                        pltpu.VMEM((2, 1, blk), F32),
                        pltpu.VMEM((2, 1, blk), F32),
                        pltpu.VMEM((2, HEAD_DIM, blk), F32)],
        compiler_params=_params(("arbitrary", "arbitrary")),
    )(uq, uk, uf, ul, rel_bias, proj3, proj3, proj3)


def _ssd_kernel(xs_ref, z_ref, bc_ref, dt_ref, cwx_ref, cwbc_ref, cbx_ref, cbbc_ref,
                dtb_ref, alog_ref, dsk_ref, ng_ref, tri_ref, trit_ref, exp_ref, o_ref,
                xtail_ref, bctail_ref, state_ref):
    c = pl.program_id(1)
    t = CHUNK
    pad = SUBLANES

    @pl.when(c == 0)
    def _():
        xtail_ref[...] = jnp.zeros_like(xtail_ref)
        bctail_ref[...] = jnp.zeros_like(bctail_ref)
        state_ref[...] = jnp.zeros_like(state_ref)

    def conv_silu(tail_ref, src_ref, w_ref, b_ref):
        x = src_ref[0]
        acc = _causal_conv(tail_ref[...], x, w_ref, b_ref[...])
        tail_ref[...] = x[t - pad:t, :]
        return _silu(acc)

    xh = conv_silu(xtail_ref, xs_ref, cwx_ref, cbx_ref)
    bc = conv_silu(bctail_ref, bc_ref, cwbc_ref, cbbc_ref)

    dtr = dt_ref[0] + dtb_ref[...]
    dt = jnp.maximum(dtr, 0.0) + jnp.log1p(jnp.exp(-jnp.abs(dtr)))
    adt = dt * (-jnp.exp(alog_ref[...]))
    acs = _dot_exact_lhs(tri_ref[...], adt)
    acs_t = _dot_exact_rhs(adt.T, trit_ref[...])

    stack = jnp.concatenate([dt, jnp.exp(acs), jnp.exp(acs[t - 1:t, :] - acs)], axis=0)
    wide = _dot_exact_rhs(stack, exp_ref[...])
    dt_x, eacs_x, dst_x = wide[0:t], wide[t:2 * t], wide[2 * t:3 * t]

    xdt = xh * dt_x
    xdt_bf = xdt.astype(BF16)
    xdec_bf = (xdt * dst_x).astype(BF16)

    row = lax.broadcasted_iota(jnp.int32, (t, t), 0)
    col = lax.broadcasted_iota(jnp.int32, (t, t), 1)
    tril = row >= col
    lane = lax.broadcasted_iota(jnp.int32, (t, LANES), 1)
    heads_per_group = N_SSM_HEADS // N_GROUPS
    nt = (((1,), (1,)), ((), ()))
    tn = (((0,), (0,)), ((), ()))

    y_parts = []
    for g in range(N_GROUPS):
        bg = bc[:, g * D_STATE:(g + 1) * D_STATE].astype(BF16)
        cg = bc[:, (N_GROUPS + g) * D_STATE:(N_GROUPS + g + 1) * D_STATE].astype(BF16)
        cb = _dot(cg, bg, nt)
        for pair in range(heads_per_group // 2):
            ms = []
            for r in (g * heads_per_group + 2 * pair, g * heads_per_group + 2 * pair + 1):
                seg = acs[:, r:r + 1] - acs_t[r:r + 1, :]
                ms.append((cb * jnp.exp(jnp.where(tril, seg, NEG))).astype(BF16))
            q = g * (heads_per_group // 2) + pair
            y2 = _dot(jnp.concatenate(ms, axis=0), xdt_bf[:, q * LANES:(q + 1) * LANES])
            y_parts.append(jnp.where(lane < SSM_HEAD_DIM, y2[0:t], y2[t:2 * t]))
    y = jnp.concatenate(y_parts, axis=1)

    off_parts = []
    for g in range(N_GROUPS):
        sl = slice(g * GROUP_WIDTH, (g + 1) * GROUP_WIDTH)
        bg = bc[:, g * D_STATE:(g + 1) * D_STATE].astype(BF16)
        cg = bc[:, (N_GROUPS + g) * D_STATE:(N_GROUPS + g + 1) * D_STATE].astype(BF16)
        st = state_ref[:, sl]
        off_parts.append(_dot(cg, st.astype(BF16)))
        state_ref[:, sl] = eacs_x[t - 1:t, sl] * st + _dot(bg, xdec_bf[:, sl], tn)
    y = y + jnp.concatenate(off_parts, axis=1) * eacs_x + dsk_ref[...] * xh
    y = y * _silu(z_ref[0])

    outs = []
    for g in range(N_GROUPS):
        sl = slice(g * GROUP_WIDTH, (g + 1) * GROUP_WIDTH)
        outs.append(_rms(y[:, sl]) * ng_ref[:, sl])
    o_ref[0] = jnp.concatenate(outs, axis=1).astype(o_ref.dtype)


def _ssd(proj3, conv_w, conv_b, dt_bias, a_log, d_skip, norm_g):
    bsz, seq, _ = proj3.shape
    t = CHUNK
    gn2 = 2 * N_GROUPS * D_STATE
    pad_h = LANES - N_SSM_HEADS
    tri = jnp.tril(jnp.ones((t, t), F32)).astype(BF16)
    expand = jnp.pad(jnp.repeat(jnp.eye(N_SSM_HEADS, dtype=F32), SSM_HEAD_DIM, axis=1),
                     ((0, pad_h), (0, 0))).astype(BF16)
    const = lambda shape: pl.BlockSpec(shape, lambda b, c: (0,) * len(shape))
    return pl.pallas_call(
        _ssd_kernel,
        grid=(bsz, seq // t),
        in_specs=[pl.BlockSpec((1, t, SSM_WIDTH), lambda b, c: (b, c, COL_XS // SSM_WIDTH)),
                  pl.BlockSpec((1, t, SSM_WIDTH), lambda b, c: (b, c, COL_Z // SSM_WIDTH)),
                  pl.BlockSpec((1, t, gn2), lambda b, c: (b, c, COL_BC // gn2)),
                  pl.BlockSpec((1, t, LANES), lambda b, c: (b, c, COL_DT // LANES)),
                  const((SSM_CONV, SSM_WIDTH)), const((SSM_CONV, gn2)),
                  const((1, SSM_WIDTH)), const((1, gn2)),
                  const((1, LANES)), const((1, LANES)),
                  const((1, SSM_WIDTH)), const((1, SSM_WIDTH)),
                  const((t, t)), const((t, t)), const((LANES, SSM_WIDTH))],
        out_specs=pl.BlockSpec((1, t, SSM_WIDTH), lambda b, c: (b, c, 0)),
        out_shape=jax.ShapeDtypeStruct((bsz, seq, SSM_WIDTH), BF16),
        scratch_shapes=[pltpu.VMEM((SUBLANES, SSM_WIDTH), F32),
                        pltpu.VMEM((SUBLANES, gn2), F32),
                        pltpu.VMEM((D_STATE, SSM_WIDTH), F32)],
        compiler_params=_params(("arbitrary", "arbitrary")),
    )(proj3, proj3, proj3, proj3,
      conv_w[:, :SSM_WIDTH], conv_w[:, SSM_WIDTH:],
      conv_b[:SSM_WIDTH].reshape(1, -1), conv_b[SSM_WIDTH:].reshape(1, -1),
      jnp.pad(dt_bias, (0, pad_h)).reshape(1, LANES), jnp.pad(a_log, (0, pad_h)).reshape(1, LANES),
      jnp.repeat(d_skip, SSM_HEAD_DIM).reshape(1, SSM_WIDTH), norm_g.reshape(1, SSM_WIDTH),
      tri, tri.T, expand)


def _outproj_kernel(attn_ref, ssm_ref, x_ref, ag_ref, w_ref, gm_ref, g2_ref, sc_ref, sh_ref,
                    x1_ref, h2_ref, *, sub):
    for r0 in range(0, x_ref.shape[0], sub):
        rows = slice(r0, r0 + sub)
        a = _rms(attn_ref[rows, :]) * ag_ref[...]
        lhs = jnp.concatenate([a.astype(BF16), ssm_ref[rows, :]], axis=-1)
        x1 = x_ref[rows, :] + gm_ref[0] * _dot(lhs, w_ref[...])
        x1_ref[rows, :] = x1
        y = _rms(x1) * g2_ref[...]
        h2_ref[rows, :] = (y * (1.0 + sc_ref[0]) + sh_ref[0]).astype(BF16)


def _outproj(attn2, ssm2, x2, attn_g, w_out_bf, gate_m, g2, scale_f, shift_f, seq, tm, sub):
    m, d = x2.shape
    per_b = seq // tm
    vec = pl.BlockSpec((1, 1, d), lambda i: (i // per_b, 0, 0))
    half = pl.BlockSpec((tm, ATTN_WIDTH), lambda i: (i, 0))
    full = pl.BlockSpec((tm, d), lambda i: (i, 0))
    return pl.pallas_call(
        functools.partial(_outproj_kernel, sub=sub),
        grid=(m // tm,),
        in_specs=[half, half, full,
                  pl.BlockSpec((1, ATTN_WIDTH), lambda i: (0, 0)),
                  _resident((d, d)),
                  vec, pl.BlockSpec((1, d), lambda i: (0, 0)), vec, vec],
        out_specs=[full, full],
        out_shape=[jax.ShapeDtypeStruct((m, d), F32), jax.ShapeDtypeStruct((m, d), BF16)],
        compiler_params=_params(("parallel",)),
    )(attn2, ssm2, x2, attn_g.reshape(1, -1), w_out_bf, gate_m, g2.reshape(1, d), scale_f, shift_f)


def _ffn_up_kernel(h_ref, wg_ref, wv_ref, cw_ref, cb_ref, wnext_ref, o_ref, wnext_bf_ref, wb_ref, tail_ref,
                   *, tm, sub, seq):
    i = pl.program_id(1)
    pad = SUBLANES
    tn = o_ref.shape[1]

    @pl.when(i == 0)
    def _():
        wb_ref[:, 0:tn] = wg_ref[...].astype(BF16)
        wb_ref[:, tn:2 * tn] = wv_ref[...].astype(BF16)

    @pl.when((i * tm) % seq == 0)
    def _():
        tail_ref[...] = jnp.zeros((pad, 2 * tn), F32)

    tail = tail_ref[...]
    for r0 in range(0, tm, sub):
        u = _dot(h_ref[r0:r0 + sub, :], wb_ref[...])
        acc = _causal_conv(tail, u, cw_ref, cb_ref[...])
        tail = u[sub - pad:sub, :]
        o_ref[r0:r0 + sub, :] = (_silu(acc[:, 0:tn]) * acc[:, tn:2 * tn]).astype(o_ref.dtype)
    tail_ref[...] = tail
    wnext_bf_ref[...] = wnext_ref[...].astype(BF16)


def _ffn_up(h2, w_up, conv_w, conv_b, w_next, seq, tm, sub, tn):
    m, d = h2.shape
    nj = FFN_DIM // tn
    ni = m // tm
    ride = pl.BlockSpec((w_next.shape[0] // (nj * ni), w_next.shape[1]), lambda j, i: (j * ni + i, 0))
    assert w_next.shape[0] % (nj * ni * 2 * SUBLANES) == 0
    pair = lambda a: jnp.concatenate([a[:, :FFN_DIM].reshape(-1, nj, tn), a[:, FFN_DIM:].reshape(-1, nj, tn)],
                                     axis=2).reshape(-1, 2 * FFN_DIM)
    return pl.pallas_call(
        functools.partial(_ffn_up_kernel, tm=tm, sub=sub, seq=seq),
        grid=(nj, ni),
        in_specs=[pl.BlockSpec((tm, d), lambda j, i: (i, 0)),
                  pl.BlockSpec((d, tn), lambda j, i: (0, j)),
                  pl.BlockSpec((d, tn), lambda j, i: (0, j + nj)),
                  pl.BlockSpec((FFN_CONV, 2 * tn), lambda j, i: (0, j)),
                  pl.BlockSpec((1, 2 * tn), lambda j, i: (0, j)),
                  ride],
        out_specs=[pl.BlockSpec((tm, tn), lambda j, i: (i, j)), ride],
        out_shape=[jax.ShapeDtypeStruct((m, FFN_DIM), BF16), jax.ShapeDtypeStruct(w_next.shape, BF16)],
        scratch_shapes=[pltpu.VMEM((d, 2 * tn), BF16), pltpu.VMEM((SUBLANES, 2 * tn), F32)],
        compiler_params=_params(("arbitrary", "arbitrary")),
    )(h2, w_up, w_up, pair(conv_w), pair(conv_b.reshape(1, -1)), w_next)


def _ffn_down_kernel(a_ref, w_ref, x1_ref, gf_ref, fg_ref, o_ref):
    x2 = x1_ref[...] + gf_ref[0] * _dot(a_ref[...], w_ref[...])
    o_ref[...] = _rms(x2) * fg_ref[...]


def _ffn_down(act, w_down_bf, x1, gate_f, final_g, seq, tm):
    m, d = x1.shape
    per_b = seq // tm
    row = lambda width: pl.BlockSpec((tm, width), lambda i: (i, 0))
    return pl.pallas_call(
        _ffn_down_kernel,
        grid=(m // tm,),
        in_specs=[row(FFN_DIM), _resident((FFN_DIM, d)), row(d),
                  pl.BlockSpec((1, 1, d), lambda i: (i // per_b, 0, 0)),
                  pl.BlockSpec((1, d), lambda i: (0, 0))],
        out_specs=row(d),
        out_shape=jax.ShapeDtypeStruct((m, d), F32),
        compiler_params=_params(("parallel",)),
    )(act, w_down_bf, x1, gate_f, final_g.reshape(1, d))


def kernel(x, c, w_ada, b_ada, norm_mix_g, w_in, rel_bias, attn_norm_g, conv_ssm_w, conv_ssm_b, dt_bias,
           a_log, d_skip, ssm_norm_g, w_out, norm_ffn_g, w_up, conv_ffn_w, conv_ffn_b, w_down, final_norm_g):
    bsz, seq, d = x.shape
    m = bsz * seq
    assert w_ada.shape[0] == 1, "the final RMSNorm is fused into the (single) layer's ffn_down kernel"
    l = 0
    x2 = x.reshape(m, d)
    mod = _ada(c, w_ada[l], b_ada[l])
    shift_m, scale_m, gate_m, shift_f, scale_f, gate_f = [
        mod[:, k * d:(k + 1) * d].reshape(bsz, 1, d) for k in range(6)]

    w_in_t = w_in[l].T
    proj, w_out_bf = _normproj(x2, norm_mix_g[l], scale_m, shift_m, w_in_t.astype(BF16), PROJ_COLS, w_out[l],
                               seq, 256, 256)
    proj3 = proj.reshape(bsz, seq, PROJ_COLS)

    attn = _moba(rel_bias, proj3)
    ssm = _ssd(proj3, conv_ssm_w[l], conv_ssm_b[l], dt_bias[l], a_log[l], d_skip[l], ssm_norm_g[l])

    x1, h2 = _outproj(attn.reshape(m, ATTN_WIDTH), ssm.reshape(m, SSM_WIDTH), x2, attn_norm_g[l],
                      w_out_bf, gate_m, norm_ffn_g[l], scale_f, shift_f, seq, 512, 256)
    act, w_down_bf = _ffn_up(h2, w_up[l], conv_ffn_w[l], conv_ffn_b[l], w_down[l], seq, 1024, 256, 512)
    out = _ffn_down(act, w_down_bf, x1, gate_f, final_norm_g, seq, 256)
    return out.reshape(bsz, seq, d)
```

```python
import functools
import math

import jax
import jax.numpy as jnp
from jax import lax
from jax.experimental import pallas as pl
from jax.experimental.pallas import tpu as pltpu

F32 = jnp.float32
BF16 = jnp.bfloat16

D_MODEL = 2048
ATTN_WIDTH = 1024
HEAD_DIM = 128
N_HEADS = 8
SSM_WIDTH = 1024
SSM_HEAD_DIM = 64
N_SSM_HEADS = 16
N_GROUPS = 2
GROUP_WIDTH = SSM_WIDTH // N_GROUPS
D_STATE = 128
SSM_CONV = 4
CHUNK = 256
MOBA_BLOCK = 256
MOBA_TOPK = 3
MOBA_FAR_GROUP = 4
MOBA_UNROLL = 2
REL_BUCKETS = 32
REL_MAX_DIST = 128
FFN_DIM = 5632
FFN_CONV = 3
EPS = 1e-6
NEG = -1e30

LANES = 128
SUBLANES = 8
VMEM_LIMIT = 56 * 1024 * 1024

PROJ_COLS = 5760
COL_XS = 3 * ATTN_WIDTH
COL_Z = COL_XS + SSM_WIDTH
COL_BC = COL_Z + SSM_WIDTH
COL_DT = COL_BC + 2 * N_GROUPS * D_STATE


def _params(sem):
    return pltpu.CompilerParams(dimension_semantics=sem, vmem_limit_bytes=VMEM_LIMIT)


def _split3(x):
    hi = x.astype(BF16)
    r = x - hi.astype(F32)
    mid = r.astype(BF16)
    lo = (r - mid.astype(F32)).astype(BF16)
    return hi, mid, lo


NT = (((1,), (1,)), ((), ()))


def _dot(a, b, dims=(((1,), (0,)), ((), ()))):
    return lax.dot_general(a, b, dims, preferred_element_type=F32)


def _dot_exact_lhs(a_bf, x, dims=(((1,), (0,)), ((), ()))):
    hi, mid, lo = _split3(x)
    return _dot(a_bf, hi, dims) + _dot(a_bf, mid, dims) + _dot(a_bf, lo, dims)


def _dot_exact_rhs(x, b_bf, dims=(((1,), (0,)), ((), ()))):
    hi, mid, lo = _split3(x)
    return _dot(hi, b_bf, dims) + _dot(mid, b_bf, dims) + _dot(lo, b_bf, dims)


def _silu(x):
    return x * jax.nn.sigmoid(x)


def _rms(x):
    return x * lax.rsqrt(jnp.mean(x * x, axis=-1, keepdims=True) + EPS)


def _causal_conv(tail, x, w_ref, b):
    rows, width = x.shape
    taps = w_ref.shape[0]
    ext = jnp.concatenate([tail, x], axis=0).reshape(rows // SUBLANES + 1, SUBLANES, width)
    row_in_slab = lax.broadcasted_iota(jnp.int32, (rows // SUBLANES, SUBLANES, width), 1)
    acc = b + w_ref[taps - 1:taps, :] * x
    for back in range(1, taps):
        rot = pltpu.roll(ext, back, axis=1)
        shifted = jnp.where(row_in_slab < back, rot[:-1], rot[1:]).reshape(rows, width)
        acc = acc + w_ref[taps - 1 - back:taps - back, :] * shifted
    return acc


def _ada_kernel(ct_ref, w_ref, b_ref, o_ref, sb_ref):
    nb = sb_ref.shape[0]
    d = w_ref.shape[0]
    tn = o_ref.shape[-1]

    @pl.when(pl.program_id(0) == 0)
    def _():
        ct = ct_ref[...]
        st = _silu(ct)
        for b in range(nb):
            sb_ref[b] = jnp.broadcast_to(st[:, b:b + 1], (d, LANES))

    def body(kc, accs):
        r = pl.multiple_of(kc * SUBLANES, SUBLANES)
        w8 = w_ref[pl.ds(r, SUBLANES), :]
        out = []
        for b in range(nb):
            s8 = sb_ref[b, pl.ds(r, SUBLANES), :]
            out.append(accs[b] + w8 * jnp.tile(s8, (1, tn // LANES)))
        return tuple(out)

    accs = lax.fori_loop(0, d // SUBLANES, body,
                         tuple(jnp.zeros((SUBLANES, tn), F32) for _ in range(nb)), unroll=8)
    for b in range(nb):
        o_ref[b:b + 1, :] = jnp.sum(accs[b], axis=0, keepdims=True) + b_ref[...]


def _ada(c, w_ada, b_ada, tn=1024):
    nb, d = c.shape
    n = w_ada.shape[1]
    return pl.pallas_call(
        _ada_kernel,
        grid=(n // tn,),
        in_specs=[pl.BlockSpec((d, nb), lambda j: (0, 0)),
                  pl.BlockSpec((d, tn), lambda j: (0, j)),
                  pl.BlockSpec((1, tn), lambda j: (0, j))],
        out_specs=pl.BlockSpec((nb, tn), lambda j: (0, j)),
        out_shape=jax.ShapeDtypeStruct((nb, n), F32),
        scratch_shapes=[pltpu.VMEM((nb, d, LANES), F32)],
        compiler_params=_params(("arbitrary",)),
    )(c.T, w_ada, b_ada.reshape(1, n))


def _normproj_kernel(x_ref, g_ref, sc_ref, sh_ref, w_ref, wnext_ref, o_ref, wnext_bf_ref, *, sub):
    n = w_ref.shape[0]
    for r0 in range(0, x_ref.shape[0], sub):
        rows = slice(r0, r0 + sub)
        y = _rms(x_ref[rows, :]) * g_ref[...]
        h = y * (1.0 + sc_ref[0]) + sh_ref[0]
        o_ref[rows, 0:n] = _dot(h.astype(BF16), w_ref[...], NT)
        if n < o_ref.shape[1]:
            o_ref[rows, n:] = jnp.zeros((sub, o_ref.shape[1] - n), F32)
    wnext_bf_ref[...] = wnext_ref[...].astype(BF16)


def _resident(shape):
    return pl.BlockSpec(shape, lambda *_: (0,) * len(shape), pipeline_mode=pl.Buffered(1))


def _normproj(x2, g, scale, shift, w_t, n_out, w_next, seq, tm, sub):
    m, d = x2.shape
    per_b = seq // tm
    steps = m // tm
    assert w_next.shape[0] % (steps * 2 * SUBLANES) == 0
    ride = pl.BlockSpec((w_next.shape[0] // steps, w_next.shape[1]), lambda i: (i, 0))
    vec = pl.BlockSpec((1, 1, d), lambda i: (i // per_b, 0, 0))
    return pl.pallas_call(
        functools.partial(_normproj_kernel, sub=sub),
        grid=(m // tm,),
        in_specs=[pl.BlockSpec((tm, d), lambda i: (i, 0)),
                  pl.BlockSpec((1, d), lambda i: (0, 0)),
                  vec, vec, _resident(w_t.shape), ride],
        out_specs=[pl.BlockSpec((tm, n_out), lambda i: (i, 0)), ride],
        out_shape=[jax.ShapeDtypeStruct((m, n_out), F32), jax.ShapeDtypeStruct(w_next.shape, BF16)],
        compiler_params=_params(("parallel",)),
    )(x2, g.reshape(1, d), scale, shift, w_t, w_next)


def _rel_bucket(dist):
    n = jnp.maximum(dist, 0)
    max_exact = REL_BUCKETS // 2
    nf = jnp.maximum(n, max_exact).astype(F32)
    large = max_exact + (jnp.log(nf / max_exact) / math.log(REL_MAX_DIST / max_exact)
                         * (REL_BUCKETS - max_exact)).astype(jnp.int32)
    large = jnp.minimum(large, REL_BUCKETS - 1)
    return jnp.where(n < max_exact, n, large)


def _moba_units(nblk, grp):
    lead = grp - 1
    units = []
    for qi in range(nblk):
        units.append((qi, qi, 1))
        for t in range(-(-max(qi - lead, 0) // grp)):
            units.append((qi, lead + t * grp, 0))
    return units


def _moba_kernel(uq_ref, uk_ref, uf_ref, rb_ref, q_ref, k_ref, v_ref, o_ref,
                 qbf_ref, kbf_ref, vt_ref, tab_ref, mask_ref, s_buf, p_buf, alpha_buf, m_ref, l_ref, acc_ref,
                 *, units):
    h = pl.program_id(1)
    seq = k_ref.shape[1]
    blk = MOBA_BLOCK
    grp = MOBA_FAR_GROUP
    lead = grp - 1
    nblk = seq // blk
    nunits = len(units)
    scale = HEAD_DIM ** -0.5
    exp2_scale = scale * math.log2(math.e)

    kf = k_ref[0]
    qf = q_ref[0]
    qbf_ref[...] = qf.astype(BF16)
    kbf_ref[0:lead * blk, :] = jnp.zeros((lead * blk, HEAD_DIM), BF16)
    kbf_ref[lead * blk:, :] = kf.astype(BF16)
    kmean = jnp.mean(kf.reshape(nblk, blk, HEAD_DIM), axis=1)
    for j in range(lead):
        vt_ref[j] = jnp.zeros((HEAD_DIM, blk), BF16)
    for j in range(nblk):
        vt_ref[lead + j] = v_ref[0, j * blk:(j + 1) * blk, :].T.astype(BF16)

    b_far = rb_ref[REL_BUCKETS - 1, h]
    kk = lax.broadcasted_iota(jnp.int32, (blk, blk), 0)
    qq = lax.broadcasted_iota(jnp.int32, (blk, blk), 1)
    tab_ref[0] = jnp.zeros((2 * blk, blk), F32)
    for pos, dist in ((0, qq - kk + blk), (1, qq - kk)):
        bucket = _rel_bucket(dist)
        tab = jnp.zeros((blk, blk), F32)
        for b in range(REL_BUCKETS):
            tab = jnp.where(bucket == b, (rb_ref[b, h] - b_far) / scale, tab)
        tab_ref[1, pos * blk:(pos + 1) * blk, :] = jnp.where(dist >= 0, tab, NEG)

    q3 = _split3(qf)
    k3 = _split3(kmean)
    gate = jnp.zeros((nblk, seq), F32)
    for a, b in ((0, 0), (0, 1), (1, 0), (1, 1), (0, 2), (2, 0)):
        gate = gate + _dot(k3[a], q3[b], NT)
    nidx = lax.broadcasted_iota(jnp.int32, (nblk, seq), 0)
    qblk = lax.broadcasted_iota(jnp.int32, (nblk, seq), 1) // blk
    nidx_f = nidx.astype(F32)
    avail = jnp.where(nidx < qblk, 1.0, 0.0)
    chosen = jnp.zeros((nblk, seq), F32)
    for _ in range(MOBA_TOPK):
        gm = jnp.where(avail > 0.0, gate, -jnp.inf)
        best = jnp.max(gm, axis=0, keepdims=True)
        first = jnp.min(jnp.where((gm == best) & (avail > 0.0), nidx_f, float(nblk)), axis=0, keepdims=True)
        hit = nidx_f == first
        chosen = jnp.where(hit, 1.0, chosen)
        avail = jnp.where(hit, 0.0, avail)
    add_all = jnp.where(chosen > 0.0, 0.0, NEG)
    for u, (qi, ks, is_first) in enumerate(units):
        cols = slice(qi * blk, (qi + 1) * blk)
        for n in range(grp):
            j = ks + n - lead
            if j == qi and is_first:
                row = jnp.zeros((1, blk), F32)
            elif 0 <= j < (qi if is_first else qi - lead):
                row = add_all[j:j + 1, cols]
            else:
                row = jnp.full((1, blk), NEG, F32)
            mask_ref[u * grp + n] = jnp.broadcast_to(row, (SUBLANES, blk))

    def unit(u):
        uc = jnp.clip(u, 0, nunits - 1)
        return uq_ref[uc], uk_ref[uc], uf_ref[uc], uc

    def scores(slot, u):
        qi, ks, is_first, uc = unit(u)
        qb = qbf_ref[pl.ds(pl.multiple_of(qi * blk, blk), blk), :]
        kg = kbf_ref[pl.ds(pl.multiple_of(ks * blk, blk), grp * blk), :]
        s = _dot(kg, qb, NT)
        tab = tab_ref[is_first]
        parts = []
        for n in range(grp):
            part = s[n * blk:(n + 1) * blk] + mask_ref[uc * grp + n][0:1, :]
            if n >= grp - 2:
                part = part + tab[(n - grp + 2) * blk:(n - grp + 3) * blk]
            parts.append(part)
        s_buf[slot] = jnp.concatenate(parts, axis=0)

    def softmax(slot, u):
        qi, _, is_first, _ = unit(u)
        st = qi % 2
        s = s_buf[slot]
        m_old = jnp.where(is_first == 1, NEG, m_ref[st])
        l_old = jnp.where(is_first == 1, 0.0, l_ref[st])
        m_new = jnp.maximum(m_old, jnp.max(s, axis=0, keepdims=True))
        alpha = jnp.exp2((m_old - m_new) * exp2_scale)
        p = jnp.exp2((s - m_new) * exp2_scale)
        m_ref[st] = m_new
        l_ref[st] = alpha * l_old + jnp.sum(p, axis=0, keepdims=True)
        alpha_buf[slot] = alpha
        p_buf[slot] = p.astype(BF16)

    def values(slot, u):
        qi, ks, _, _ = unit(u)
        st = qi % 2
        acc = alpha_buf[slot] * acc_ref[st]
        for n in range(grp):
            acc = acc + _dot(vt_ref[ks + n], p_buf[slot, n * blk:(n + 1) * blk, :])
        acc_ref[st] = acc
        o_ref[0, pl.ds(pl.multiple_of(qi * blk, blk), blk), :] = (acc * (1.0 / l_ref[st])).T

    m_ref[...] = jnp.full(m_ref.shape, NEG, F32)
    l_ref[...] = jnp.ones(l_ref.shape, F32)
    acc_ref[...] = jnp.zeros(acc_ref.shape, F32)
    alpha_buf[1] = jnp.ones((1, blk), F32)
    p_buf[1] = jnp.zeros((grp * blk, blk), BF16)
    scores(0, 0)

    def trip(u, slot):
        values(1 - slot, u - 1)
        softmax(slot, u)
        scores(1 - slot, u + 1)

    def trips(it, carry):
        for n in range(MOBA_UNROLL):
            trip(MOBA_UNROLL * it + n, n % 2)
        return carry

    lax.fori_loop(0, nunits // MOBA_UNROLL, trips, 0)
    for u in range(nunits - nunits % MOBA_UNROLL, nunits):
        trip(u, u % 2)
    values((nunits - 1) % 2, nunits - 1)


def _moba(rel_bias, proj3):
    bsz, seq, _ = proj3.shape
    blk = MOBA_BLOCK
    grp = MOBA_FAR_GROUP
    nblk = seq // blk
    assert seq % blk == 0 and grp >= 2
    units = _moba_units(nblk, grp)
    uq, uk, uf = (jnp.asarray([u[c] for u in units], jnp.int32) for c in range(3))
    smem = pl.BlockSpec(memory_space=pltpu.SMEM)
    head = lambda col0: pl.BlockSpec((1, seq, HEAD_DIM), lambda b, h: (b, 0, col0 + h))
    return pl.pallas_call(
        functools.partial(_moba_kernel, units=units),
        grid=(bsz, N_HEADS),
        in_specs=[smem, smem, smem, smem, head(0), head(N_HEADS), head(2 * N_HEADS)],
        out_specs=head(0),
        out_shape=jax.ShapeDtypeStruct((bsz, seq, ATTN_WIDTH), F32),
        scratch_shapes=[pltpu.VMEM((seq, HEAD_DIM), BF16),
                        pltpu.VMEM((seq + (grp - 1) * blk, HEAD_DIM), BF16),
                        pltpu.VMEM((nblk + grp - 1, HEAD_DIM, blk), BF16),
                        pltpu.VMEM((2, 2 * blk, blk), F32),
                        pltpu.VMEM((len(units) * grp, SUBLANES, blk), F32),
                        pltpu.VMEM((2, grp * blk, blk), F32),
                        pltpu.VMEM((2, grp * blk, blk), BF16),
                        pltpu.VMEM((2, 1, blk), F32),
                        pltpu.VMEM((2, 1, blk), F32),
                        pltpu.VMEM((2, 1, blk), F32),
                        pltpu.VMEM((2, HEAD_DIM, blk), F32)],
        compiler_params=_params(("arbitrary", "arbitrary")),
    )(uq, uk, uf, rel_bias, proj3, proj3, proj3)


def _ssd_kernel(xs_ref, z_ref, bc_ref, dt_ref, cwx_ref, cwbc_ref, cbx_ref, cbbc_ref,
                dtb_ref, alog_ref, dsk_ref, ng_ref, tri_ref, trit_ref, exp_ref, o_ref,
                xtail_ref, bctail_ref, state_ref):
    c = pl.program_id(1)
    t = CHUNK
    pad = SUBLANES

    @pl.when(c == 0)
    def _():
        xtail_ref[...] = jnp.zeros_like(xtail_ref)
        bctail_ref[...] = jnp.zeros_like(bctail_ref)
        state_ref[...] = jnp.zeros_like(state_ref)

    def conv_silu(tail_ref, src_ref, w_ref, b_ref):
        x = src_ref[0]
        acc = _causal_conv(tail_ref[...], x, w_ref, b_ref[...])
        tail_ref[...] = x[t - pad:t, :]
        return _silu(acc)

    xh = conv_silu(xtail_ref, xs_ref, cwx_ref, cbx_ref)
    bc = conv_silu(bctail_ref, bc_ref, cwbc_ref, cbbc_ref)

    dtr = dt_ref[0] + dtb_ref[...]
    dt = jnp.maximum(dtr, 0.0) + jnp.log1p(jnp.exp(-jnp.abs(dtr)))
    adt = dt * (-jnp.exp(alog_ref[...]))
    acs = _dot_exact_lhs(tri_ref[...], adt)
    acs_t = _dot_exact_rhs(adt.T, trit_ref[...])

    stack = jnp.concatenate([dt, jnp.exp(acs), jnp.exp(acs[t - 1:t, :] - acs)], axis=0)
    wide = _dot_exact_rhs(stack, exp_ref[...])
    dt_x, eacs_x, dst_x = wide[0:t], wide[t:2 * t], wide[2 * t:3 * t]

    xdt = xh * dt_x
    xdt_bf = xdt.astype(BF16)
    xdec_bf = (xdt * dst_x).astype(BF16)

    row = lax.broadcasted_iota(jnp.int32, (t, t), 0)
    col = lax.broadcasted_iota(jnp.int32, (t, t), 1)
    tril = row >= col
    lane = lax.broadcasted_iota(jnp.int32, (t, LANES), 1)
    heads_per_group = N_SSM_HEADS // N_GROUPS
    nt = (((1,), (1,)), ((), ()))
    tn = (((0,), (0,)), ((), ()))

    y_parts = []
    for g in range(N_GROUPS):
        bg = bc[:, g * D_STATE:(g + 1) * D_STATE].astype(BF16)
        cg = bc[:, (N_GROUPS + g) * D_STATE:(N_GROUPS + g + 1) * D_STATE].astype(BF16)
        cb = _dot(cg, bg, nt)
        for pair in range(heads_per_group // 2):
            ms = []
            for r in (g * heads_per_group + 2 * pair, g * heads_per_group + 2 * pair + 1):
                seg = acs[:, r:r + 1] - acs_t[r:r + 1, :]
                ms.append((cb * jnp.exp(jnp.where(tril, seg, NEG))).astype(BF16))
            q = g * (heads_per_group // 2) + pair
            y2 = _dot(jnp.concatenate(ms, axis=0), xdt_bf[:, q * LANES:(q + 1) * LANES])
            y_parts.append(jnp.where(lane < SSM_HEAD_DIM, y2[0:t], y2[t:2 * t]))
    y = jnp.concatenate(y_parts, axis=1)

    off_parts = []
    for g in range(N_GROUPS):
        sl = slice(g * GROUP_WIDTH, (g + 1) * GROUP_WIDTH)
        bg = bc[:, g * D_STATE:(g + 1) * D_STATE].astype(BF16)
        cg = bc[:, (N_GROUPS + g) * D_STATE:(N_GROUPS + g + 1) * D_STATE].astype(BF16)
        st = state_ref[:, sl]
        off_parts.append(_dot(cg, st.astype(BF16)))
        state_ref[:, sl] = eacs_x[t - 1:t, sl] * st + _dot(bg, xdec_bf[:, sl], tn)
    y = y + jnp.concatenate(off_parts, axis=1) * eacs_x + dsk_ref[...] * xh
    y = y * _silu(z_ref[0])

    outs = []
    for g in range(N_GROUPS):
        sl = slice(g * GROUP_WIDTH, (g + 1) * GROUP_WIDTH)
        outs.append(_rms(y[:, sl]) * ng_ref[:, sl])
    o_ref[0] = jnp.concatenate(outs, axis=1).astype(o_ref.dtype)


def _ssd(proj3, conv_w, conv_b, dt_bias, a_log, d_skip, norm_g):
    bsz, seq, _ = proj3.shape
    t = CHUNK
    gn2 = 2 * N_GROUPS * D_STATE
    pad_h = LANES - N_SSM_HEADS
    tri = jnp.tril(jnp.ones((t, t), F32)).astype(BF16)
    expand = jnp.pad(jnp.repeat(jnp.eye(N_SSM_HEADS, dtype=F32), SSM_HEAD_DIM, axis=1),
                     ((0, pad_h), (0, 0))).astype(BF16)
    const = lambda shape: pl.BlockSpec(shape, lambda b, c: (0,) * len(shape))
    return pl.pallas_call(
        _ssd_kernel,
        grid=(bsz, seq // t),
        in_specs=[pl.BlockSpec((1, t, SSM_WIDTH), lambda b, c: (b, c, COL_XS // SSM_WIDTH)),
                  pl.BlockSpec((1, t, SSM_WIDTH), lambda b, c: (b, c, COL_Z // SSM_WIDTH)),
                  pl.BlockSpec((1, t, gn2), lambda b, c: (b, c, COL_BC // gn2)),
                  pl.BlockSpec((1, t, LANES), lambda b, c: (b, c, COL_DT // LANES)),
                  const((SSM_CONV, SSM_WIDTH)), const((SSM_CONV, gn2)),
                  const((1, SSM_WIDTH)), const((1, gn2)),
                  const((1, LANES)), const((1, LANES)),
                  const((1, SSM_WIDTH)), const((1, SSM_WIDTH)),
                  const((t, t)), const((t, t)), const((LANES, SSM_WIDTH))],
        out_specs=pl.BlockSpec((1, t, SSM_WIDTH), lambda b, c: (b, c, 0)),
        out_shape=jax.ShapeDtypeStruct((bsz, seq, SSM_WIDTH), BF16),
        scratch_shapes=[pltpu.VMEM((SUBLANES, SSM_WIDTH), F32),
                        pltpu.VMEM((SUBLANES, gn2), F32),
                        pltpu.VMEM((D_STATE, SSM_WIDTH), F32)],
        compiler_params=_params(("arbitrary", "arbitrary")),
    )(proj3, proj3, proj3, proj3,
      conv_w[:, :SSM_WIDTH], conv_w[:, SSM_WIDTH:],
      conv_b[:SSM_WIDTH].reshape(1, -1), conv_b[SSM_WIDTH:].reshape(1, -1),
      jnp.pad(dt_bias, (0, pad_h)).reshape(1, LANES), jnp.pad(a_log, (0, pad_h)).reshape(1, LANES),
      jnp.repeat(d_skip, SSM_HEAD_DIM).reshape(1, SSM_WIDTH), norm_g.reshape(1, SSM_WIDTH),
      tri, tri.T, expand)


def _outproj_kernel(attn_ref, ssm_ref, x_ref, ag_ref, w_ref, gm_ref, g2_ref, sc_ref, sh_ref,
                    x1_ref, h2_ref, *, sub):
    for r0 in range(0, x_ref.shape[0], sub):
        rows = slice(r0, r0 + sub)
        a = _rms(attn_ref[rows, :]) * ag_ref[...]
        lhs = jnp.concatenate([a.astype(BF16), ssm_ref[rows, :]], axis=-1)
        x1 = x_ref[rows, :] + gm_ref[0] * _dot(lhs, w_ref[...])
        x1_ref[rows, :] = x1
        y = _rms(x1) * g2_ref[...]
        h2_ref[rows, :] = (y * (1.0 + sc_ref[0]) + sh_ref[0]).astype(BF16)


def _outproj(attn2, ssm2, x2, attn_g, w_out_bf, gate_m, g2, scale_f, shift_f, seq, tm, sub):
    m, d = x2.shape
    per_b = seq // tm
    vec = pl.BlockSpec((1, 1, d), lambda i: (i // per_b, 0, 0))
    half = pl.BlockSpec((tm, ATTN_WIDTH), lambda i: (i, 0))
    full = pl.BlockSpec((tm, d), lambda i: (i, 0))
    return pl.pallas_call(
        functools.partial(_outproj_kernel, sub=sub),
        grid=(m // tm,),
        in_specs=[half, half, full,
                  pl.BlockSpec((1, ATTN_WIDTH), lambda i: (0, 0)),
                  _resident((d, d)),
                  vec, pl.BlockSpec((1, d), lambda i: (0, 0)), vec, vec],
        out_specs=[full, full],
        out_shape=[jax.ShapeDtypeStruct((m, d), F32), jax.ShapeDtypeStruct((m, d), BF16)],
        compiler_params=_params(("parallel",)),
    )(attn2, ssm2, x2, attn_g.reshape(1, -1), w_out_bf, gate_m, g2.reshape(1, d), scale_f, shift_f)


def _ffn_up_kernel(h_ref, wg_ref, wv_ref, cw_ref, cb_ref, wnext_ref, o_ref, wnext_bf_ref, wb_ref, tail_ref,
                   *, tm, sub, seq):
    i = pl.program_id(1)
    pad = SUBLANES
    tn = o_ref.shape[1]

    @pl.when(i == 0)
    def _():
        wb_ref[:, 0:tn] = wg_ref[...].astype(BF16)
        wb_ref[:, tn:2 * tn] = wv_ref[...].astype(BF16)

    @pl.when((i * tm) % seq == 0)
    def _():
        tail_ref[...] = jnp.zeros((pad, 2 * tn), F32)

    tail = tail_ref[...]
    for r0 in range(0, tm, sub):
        u = _dot(h_ref[r0:r0 + sub, :], wb_ref[...])
        acc = _causal_conv(tail, u, cw_ref, cb_ref[...])
        tail = u[sub - pad:sub, :]
        o_ref[r0:r0 + sub, :] = (_silu(acc[:, 0:tn]) * acc[:, tn:2 * tn]).astype(o_ref.dtype)
    tail_ref[...] = tail
    wnext_bf_ref[...] = wnext_ref[...].astype(BF16)


def _ffn_up(h2, w_up, conv_w, conv_b, w_next, seq, tm, sub, tn):
    m, d = h2.shape
    nj = FFN_DIM // tn
    ni = m // tm
    ride = pl.BlockSpec((w_next.shape[0] // (nj * ni), w_next.shape[1]), lambda j, i: (j * ni + i, 0))
    assert w_next.shape[0] % (nj * ni * 2 * SUBLANES) == 0
    pair = lambda a: jnp.concatenate([a[:, :FFN_DIM].reshape(-1, nj, tn), a[:, FFN_DIM:].reshape(-1, nj, tn)],
                                     axis=2).reshape(-1, 2 * FFN_DIM)
    return pl.pallas_call(
        functools.partial(_ffn_up_kernel, tm=tm, sub=sub, seq=seq),
        grid=(nj, ni),
        in_specs=[pl.BlockSpec((tm, d), lambda j, i: (i, 0)),
                  pl.BlockSpec((d, tn), lambda j, i: (0, j)),
                  pl.BlockSpec((d, tn), lambda j, i: (0, j + nj)),
                  pl.BlockSpec((FFN_CONV, 2 * tn), lambda j, i: (0, j)),
                  pl.BlockSpec((1, 2 * tn), lambda j, i: (0, j)),
                  ride],
        out_specs=[pl.BlockSpec((tm, tn), lambda j, i: (i, j)), ride],
        out_shape=[jax.ShapeDtypeStruct((m, FFN_DIM), BF16), jax.ShapeDtypeStruct(w_next.shape, BF16)],
        scratch_shapes=[pltpu.VMEM((d, 2 * tn), BF16), pltpu.VMEM((SUBLANES, 2 * tn), F32)],
        compiler_params=_params(("arbitrary", "arbitrary")),
    )(h2, w_up, w_up, pair(conv_w), pair(conv_b.reshape(1, -1)), w_next)


def _ffn_down_kernel(a_ref, w_ref, x1_ref, gf_ref, fg_ref, o_ref):
    x2 = x1_ref[...] + gf_ref[0] * _dot(a_ref[...], w_ref[...])
    o_ref[...] = _rms(x2) * fg_ref[...]


def _ffn_down(act, w_down_bf, x1, gate_f, final_g, seq, tm):
    m, d = x1.shape
    per_b = seq // tm
    row = lambda width: pl.BlockSpec((tm, width), lambda i: (i, 0))
    return pl.pallas_call(
        _ffn_down_kernel,
        grid=(m // tm,),
        in_specs=[row(FFN_DIM), _resident((FFN_DIM, d)), row(d),
                  pl.BlockSpec((1, 1, d), lambda i: (i // per_b, 0, 0)),
                  pl.BlockSpec((1, d), lambda i: (0, 0))],
        out_specs=row(d),
        out_shape=jax.ShapeDtypeStruct((m, d), F32),
        compiler_params=_params(("parallel",)),
    )(act, w_down_bf, x1, gate_f, final_g.reshape(1, d))


def kernel(x, c, w_ada, b_ada, norm_mix_g, w_in, rel_bias, attn_norm_g, conv_ssm_w, conv_ssm_b, dt_bias,
           a_log, d_skip, ssm_norm_g, w_out, norm_ffn_g, w_up, conv_ffn_w, conv_ffn_b, w_down, final_norm_g):
    bsz, seq, d = x.shape
    m = bsz * seq
    assert w_ada.shape[0] == 1, "the final RMSNorm is fused into the (single) layer's ffn_down kernel"
    l = 0
    x2 = x.reshape(m, d)
    mod = _ada(c, w_ada[l], b_ada[l])
    shift_m, scale_m, gate_m, shift_f, scale_f, gate_f = [
        mod[:, k * d:(k + 1) * d].reshape(bsz, 1, d) for k in range(6)]

    w_in_t = w_in[l].T
    proj, w_out_bf = _normproj(x2, norm_mix_g[l], scale_m, shift_m, w_in_t.astype(BF16), PROJ_COLS, w_out[l],
                               seq, 256, 256)
    proj3 = proj.reshape(bsz, seq, PROJ_COLS)

    attn = _moba(rel_bias, proj3)
    ssm = _ssd(proj3, conv_ssm_w[l], conv_ssm_b[l], dt_bias[l], a_log[l], d_skip[l], ssm_norm_g[l])

    x1, h2 = _outproj(attn.reshape(m, ATTN_WIDTH), ssm.reshape(m, SSM_WIDTH), x2, attn_norm_g[l],
                      w_out_bf, gate_m, norm_ffn_g[l], scale_f, shift_f, seq, 512, 256)
    act, w_down_bf = _ffn_up(h2, w_up[l], conv_ffn_w[l], conv_ffn_b[l], w_down[l], seq, 1024, 256, 512)
    out = _ffn_down(act, w_down_bf, x1, gate_f, final_norm_g, seq, 256)
    return out.reshape(bsz, seq, d)
```

```python
import functools
import math

import jax
import jax.numpy as jnp
from jax import lax
from jax.experimental import pallas as pl
from jax.experimental.pallas import tpu as pltpu

F32 = jnp.float32
BF16 = jnp.bfloat16

D_MODEL = 2048
ATTN_WIDTH = 1024
HEAD_DIM = 128
N_HEADS = 8
SSM_WIDTH = 1024
SSM_HEAD_DIM = 64
N_SSM_HEADS = 16
N_GROUPS = 2
GROUP_WIDTH = SSM_WIDTH // N_GROUPS
D_STATE = 128
SSM_CONV = 4
CHUNK = 256
MOBA_BLOCK = 256
MOBA_TOPK = 3
MOBA_FAR_GROUP = 4
MOBA_UNROLL = 2
REL_BUCKETS = 32
REL_MAX_DIST = 128
FFN_DIM = 5632
FFN_CONV = 3
EPS = 1e-6
NEG = -1e30

LANES = 128
SUBLANES = 8
VMEM_LIMIT = 56 * 1024 * 1024

PROJ_COLS = 5760
COL_XS = 3 * ATTN_WIDTH
COL_Z = COL_XS + SSM_WIDTH
COL_BC = COL_Z + SSM_WIDTH
COL_DT = COL_BC + 2 * N_GROUPS * D_STATE


def _params(sem):
    return pltpu.CompilerParams(dimension_semantics=sem, vmem_limit_bytes=VMEM_LIMIT)


def _split3(x):
    hi = x.astype(BF16)
    r = x - hi.astype(F32)
    mid = r.astype(BF16)
    lo = (r - mid.astype(F32)).astype(BF16)
    return hi, mid, lo


NT = (((1,), (1,)), ((), ()))


def _dot(a, b, dims=(((1,), (0,)), ((), ()))):
    return lax.dot_general(a, b, dims, preferred_element_type=F32)


def _dot_exact_lhs(a_bf, x, dims=(((1,), (0,)), ((), ()))):
    hi, mid, lo = _split3(x)
    return _dot(a_bf, hi, dims) + _dot(a_bf, mid, dims) + _dot(a_bf, lo, dims)


def _dot_exact_rhs(x, b_bf, dims=(((1,), (0,)), ((), ()))):
    hi, mid, lo = _split3(x)
    return _dot(hi, b_bf, dims) + _dot(mid, b_bf, dims) + _dot(lo, b_bf, dims)


def _silu(x):
    return x * jax.nn.sigmoid(x)


def _rms(x):
    return x * lax.rsqrt(jnp.mean(x * x, axis=-1, keepdims=True) + EPS)


def _causal_conv(tail, x, w_ref, b):
    rows, width = x.shape
    taps = w_ref.shape[0]
    ext = jnp.concatenate([tail, x], axis=0).reshape(rows // SUBLANES + 1, SUBLANES, width)
    row_in_slab = lax.broadcasted_iota(jnp.int32, (rows // SUBLANES, SUBLANES, width), 1)
    acc = b + w_ref[taps - 1:taps, :] * x
    for back in range(1, taps):
        rot = pltpu.roll(ext, back, axis=1)
        shifted = jnp.where(row_in_slab < back, rot[:-1], rot[1:]).reshape(rows, width)
        acc = acc + w_ref[taps - 1 - back:taps - back, :] * shifted
    return acc


def _ada_kernel(ct_ref, w_ref, b_ref, o_ref, sb_ref):
    nb = sb_ref.shape[0]
    d = w_ref.shape[0]
    tn = o_ref.shape[-1]

    @pl.when(pl.program_id(0) == 0)
    def _():
        ct = ct_ref[...]
        st = _silu(ct)
        for b in range(nb):
            sb_ref[b] = jnp.broadcast_to(st[:, b:b + 1], (d, LANES))

    def body(kc, accs):
        r = pl.multiple_of(kc * SUBLANES, SUBLANES)
        w8 = w_ref[pl.ds(r, SUBLANES), :]
        out = []
        for b in range(nb):
            s8 = sb_ref[b, pl.ds(r, SUBLANES), :]
            out.append(accs[b] + w8 * jnp.tile(s8, (1, tn // LANES)))
        return tuple(out)

    accs = lax.fori_loop(0, d // SUBLANES, body,
                         tuple(jnp.zeros((SUBLANES, tn), F32) for _ in range(nb)), unroll=8)
    for b in range(nb):
        o_ref[b:b + 1, :] = jnp.sum(accs[b], axis=0, keepdims=True) + b_ref[...]


def _ada(c, w_ada, b_ada, tn=1024):
    nb, d = c.shape
    n = w_ada.shape[1]
    return pl.pallas_call(
        _ada_kernel,
        grid=(n // tn,),
        in_specs=[pl.BlockSpec((d, nb), lambda j: (0, 0)),
                  pl.BlockSpec((d, tn), lambda j: (0, j)),
                  pl.BlockSpec((1, tn), lambda j: (0, j))],
        out_specs=pl.BlockSpec((nb, tn), lambda j: (0, j)),
        out_shape=jax.ShapeDtypeStruct((nb, n), F32),
        scratch_shapes=[pltpu.VMEM((nb, d, LANES), F32)],
        compiler_params=_params(("arbitrary",)),
    )(c.T, w_ada, b_ada.reshape(1, n))


def _normproj_kernel(x_ref, g_ref, sc_ref, sh_ref, w_ref, wnext_ref, o_ref, wnext_bf_ref, *, sub):
    n = w_ref.shape[0]
    for r0 in range(0, x_ref.shape[0], sub):
        rows = slice(r0, r0 + sub)
        y = _rms(x_ref[rows, :]) * g_ref[...]
        h = y * (1.0 + sc_ref[0]) + sh_ref[0]
        o_ref[rows, 0:n] = _dot(h.astype(BF16), w_ref[...], NT)
        if n < o_ref.shape[1]:
            o_ref[rows, n:] = jnp.zeros((sub, o_ref.shape[1] - n), F32)
    wnext_bf_ref[...] = wnext_ref[...].astype(BF16)


def _resident(shape):
    return pl.BlockSpec(shape, lambda *_: (0,) * len(shape), pipeline_mode=pl.Buffered(1))


def _normproj(x2, g, scale, shift, w_t, n_out, w_next, seq, tm, sub):
    m, d = x2.shape
    per_b = seq // tm
    steps = m // tm
    assert w_next.shape[0] % (steps * 2 * SUBLANES) == 0
    ride = pl.BlockSpec((w_next.shape[0] // steps, w_next.shape[1]), lambda i: (i, 0))
    vec = pl.BlockSpec((1, 1, d), lambda i: (i // per_b, 0, 0))
    return pl.pallas_call(
        functools.partial(_normproj_kernel, sub=sub),
        grid=(m // tm,),
        in_specs=[pl.BlockSpec((tm, d), lambda i: (i, 0)),
                  pl.BlockSpec((1, d), lambda i: (0, 0)),
                  vec, vec, _resident(w_t.shape), ride],
        out_specs=[pl.BlockSpec((tm, n_out), lambda i: (i, 0)), ride],
        out_shape=[jax.ShapeDtypeStruct((m, n_out), F32), jax.ShapeDtypeStruct(w_next.shape, BF16)],
        compiler_params=_params(("parallel",)),
    )(x2, g.reshape(1, d), scale, shift, w_t, w_next)


def _rel_bucket(dist):
    n = jnp.maximum(dist, 0)
    max_exact = REL_BUCKETS // 2
    nf = jnp.maximum(n, max_exact).astype(F32)
    large = max_exact + (jnp.log(nf / max_exact) / math.log(REL_MAX_DIST / max_exact)
                         * (REL_BUCKETS - max_exact)).astype(jnp.int32)
    large = jnp.minimum(large, REL_BUCKETS - 1)
    return jnp.where(n < max_exact, n, large)


def _moba_units(nblk, grp):
    lead = grp - 1
    units = []
    for qi in range(nblk):
        units.append((qi, qi, 1))
        for t in range(-(-max(qi - lead, 0) // grp)):
            units.append((qi, lead + t * grp, 0))
    return units


def _moba_kernel(uq_ref, uk_ref, uf_ref, rb_ref, q_ref, k_ref, v_ref, o_ref,
                 qbf_ref, kbf_ref, vt_ref, tab_ref, mask_ref, s_buf, p_buf, alpha_buf, m_ref, l_ref, acc_ref,
                 *, units):
    h = pl.program_id(0)
    seq = k_ref.shape[1]
    blk = MOBA_BLOCK
    grp = MOBA_FAR_GROUP
    lead = grp - 1
    nblk = seq // blk
    nunits = len(units)
    scale = HEAD_DIM ** -0.5
    exp2_scale = scale * math.log2(math.e)

    kf = k_ref[0]
    qf = q_ref[0]
    qbf_ref[...] = qf.astype(BF16)
    kbf_ref[0:lead * blk, :] = jnp.zeros((lead * blk, HEAD_DIM), BF16)
    kbf_ref[lead * blk:, :] = kf.astype(BF16)
    kmean = jnp.mean(kf.reshape(nblk, blk, HEAD_DIM), axis=1)
    for j in range(lead):
        vt_ref[j] = jnp.zeros((HEAD_DIM, blk), BF16)
    for j in range(nblk):
        vt_ref[lead + j] = v_ref[0, j * blk:(j + 1) * blk, :].T.astype(BF16)

    @pl.when(pl.program_id(1) == 0)
    def _():
        b_far = rb_ref[REL_BUCKETS - 1, h]
        kk = lax.broadcasted_iota(jnp.int32, (blk, blk), 0)
        qq = lax.broadcasted_iota(jnp.int32, (blk, blk), 1)
        tab_ref[0] = jnp.zeros((2 * blk, blk), F32)
        for pos, dist in ((0, qq - kk + blk), (1, qq - kk)):
            bucket = _rel_bucket(dist)
            tab = jnp.zeros((blk, blk), F32)
            for b in range(REL_BUCKETS):
                tab = jnp.where(bucket == b, (rb_ref[b, h] - b_far) / scale, tab)
            tab_ref[1, pos * blk:(pos + 1) * blk, :] = jnp.where(dist >= 0, tab, NEG)

    q_hi, q_mid, _ = _split3(qf)
    k_hi, k_mid, _ = _split3(kmean)
    gate = _dot(k_hi, q_hi, NT) + _dot(k_hi, q_mid, NT) + _dot(k_mid, q_hi, NT)
    nidx = lax.broadcasted_iota(jnp.int32, (nblk, seq), 0)
    qblk = lax.broadcasted_iota(jnp.int32, (nblk, seq), 1) // blk
    nidx_f = nidx.astype(F32)
    avail = jnp.where(nidx < qblk, 1.0, 0.0)
    chosen = jnp.zeros((nblk, seq), F32)
    for _ in range(MOBA_TOPK):
        gm = jnp.where(avail > 0.0, gate, -jnp.inf)
        best = jnp.max(gm, axis=0, keepdims=True)
        first = jnp.min(jnp.where((gm == best) & (avail > 0.0), nidx_f, float(nblk)), axis=0, keepdims=True)
        hit = nidx_f == first
        chosen = jnp.where(hit, 1.0, chosen)
        avail = jnp.where(hit, 0.0, avail)
    add_all = jnp.where(chosen > 0.0, 0.0, NEG)
    for u, (qi, ks, is_first) in enumerate(units):
        cols = slice(qi * blk, (qi + 1) * blk)
        for n in range(grp):
            j = ks + n - lead
            if j == qi and is_first:
                row = jnp.zeros((1, blk), F32)
            elif 0 <= j < (qi if is_first else qi - lead):
                row = add_all[j:j + 1, cols]
            else:
                row = jnp.full((1, blk), NEG, F32)
            mask_ref[u * grp + n] = jnp.broadcast_to(row, (SUBLANES, blk))

    def unit(u):
        uc = jnp.clip(u, 0, nunits - 1)
        return uq_ref[uc], uk_ref[uc], uf_ref[uc], uc

    def scores(slot, u):
        qi, ks, is_first, uc = unit(u)
        qb = qbf_ref[pl.ds(pl.multiple_of(qi * blk, blk), blk), :]
        kg = kbf_ref[pl.ds(pl.multiple_of(ks * blk, blk), grp * blk), :]
        s = _dot(kg, qb, NT)
        tab = tab_ref[is_first]
        parts = []
        for n in range(grp):
            part = s[n * blk:(n + 1) * blk] + mask_ref[uc * grp + n][0:1, :]
            if n >= grp - 2:
                part = part + tab[(n - grp + 2) * blk:(n - grp + 3) * blk]
            parts.append(part)
        s_buf[slot] = jnp.concatenate(parts, axis=0)

    def softmax(slot, u):
        qi, _, is_first, _ = unit(u)
        st = qi % 2
        s = s_buf[slot]
        m_old = jnp.where(is_first == 1, NEG, m_ref[st])
        l_old = jnp.where(is_first == 1, 0.0, l_ref[st])
        m_new = jnp.maximum(m_old, jnp.max(s, axis=0, keepdims=True))
        alpha = jnp.exp2((m_old - m_new) * exp2_scale)
        p = jnp.exp2((s - m_new) * exp2_scale)
        m_ref[st] = m_new
        l_ref[st] = alpha * l_old + jnp.sum(p, axis=0, keepdims=True)
        alpha_buf[slot] = alpha
        p_buf[slot] = p.astype(BF16)

    def values(slot, u):
        qi, ks, _, _ = unit(u)
        st = qi % 2
        acc = alpha_buf[slot] * acc_ref[st]
        for n in range(grp):
            acc = acc + _dot(vt_ref[ks + n], p_buf[slot, n * blk:(n + 1) * blk, :])
        acc_ref[st] = acc
        o_ref[0, pl.ds(pl.multiple_of(qi * blk, blk), blk), :] = (acc * (1.0 / l_ref[st])).T

    m_ref[...] = jnp.full(m_ref.shape, NEG, F32)
    l_ref[...] = jnp.ones(l_ref.shape, F32)
    acc_ref[...] = jnp.zeros(acc_ref.shape, F32)
    alpha_buf[1] = jnp.ones((1, blk), F32)
    p_buf[1] = jnp.zeros((grp * blk, blk), BF16)
    scores(0, 0)

    def trip(u, slot):
        values(1 - slot, u - 1)
        softmax(slot, u)
        scores(1 - slot, u + 1)

    def trips(it, carry):
        for n in range(MOBA_UNROLL):
            trip(MOBA_UNROLL * it + n, n % 2)
        return carry

    lax.fori_loop(0, nunits // MOBA_UNROLL, trips, 0)
    for u in range(nunits - nunits % MOBA_UNROLL, nunits):
        trip(u, u % 2)
    values((nunits - 1) % 2, nunits - 1)


def _moba(rel_bias, proj3):
    bsz, seq, _ = proj3.shape
    blk = MOBA_BLOCK
    grp = MOBA_FAR_GROUP
    nblk = seq // blk
    assert seq % blk == 0 and grp >= 2
    units = _moba_units(nblk, grp)
    uq, uk, uf = (jnp.asarray([u[c] for u in units], jnp.int32) for c in range(3))
    smem = pl.BlockSpec(memory_space=pltpu.SMEM)
    head = lambda col0: pl.BlockSpec((1, seq, HEAD_DIM), lambda h, b: (b, 0, col0 + h))
    return pl.pallas_call(
        functools.partial(_moba_kernel, units=units),
        grid=(N_HEADS, bsz),
        in_specs=[smem, smem, smem, smem, head(0), head(N_HEADS), head(2 * N_HEADS)],
        out_specs=head(0),
        out_shape=jax.ShapeDtypeStruct((bsz, seq, ATTN_WIDTH), F32),
        scratch_shapes=[pltpu.VMEM((seq, HEAD_DIM), BF16),
                        pltpu.VMEM((seq + (grp - 1) * blk, HEAD_DIM), BF16),
                        pltpu.VMEM((nblk + grp - 1, HEAD_DIM, blk), BF16),
                        pltpu.VMEM((2, 2 * blk, blk), F32),
                        pltpu.VMEM((len(units) * grp, SUBLANES, blk), F32),
                        pltpu.VMEM((2, grp * blk, blk), F32),
                        pltpu.VMEM((2, grp * blk, blk), BF16),
                        pltpu.VMEM((2, 1, blk), F32),
                        pltpu.VMEM((2, 1, blk), F32),
                        pltpu.VMEM((2, 1, blk), F32),
                        pltpu.VMEM((2, HEAD_DIM, blk), F32)],
        compiler_params=_params(("arbitrary", "arbitrary")),
    )(uq, uk, uf, rel_bias, proj3, proj3, proj3)


def _ssd_kernel(xs_ref, z_ref, bc_ref, dt_ref, cwx_ref, cwbc_ref, cbx_ref, cbbc_ref,
                dtb_ref, alog_ref, dsk_ref, ng_ref, tri_ref, trit_ref, exp_ref, o_ref,
                xtail_ref, bctail_ref, state_ref):
    c = pl.program_id(1)
    t = CHUNK
    pad = SUBLANES

    @pl.when(c == 0)
    def _():
        xtail_ref[...] = jnp.zeros_like(xtail_ref)
        bctail_ref[...] = jnp.zeros_like(bctail_ref)
        state_ref[...] = jnp.zeros_like(state_ref)

    def conv_silu(tail_ref, src_ref, w_ref, b_ref):
        x = src_ref[0]
        acc = _causal_conv(tail_ref[...], x, w_ref, b_ref[...])
        tail_ref[...] = x[t - pad:t, :]
        return _silu(acc)

    xh = conv_silu(xtail_ref, xs_ref, cwx_ref, cbx_ref)
    bc = conv_silu(bctail_ref, bc_ref, cwbc_ref, cbbc_ref)

    dtr = dt_ref[0] + dtb_ref[...]
    dt = jnp.maximum(dtr, 0.0) + jnp.log1p(jnp.exp(-jnp.abs(dtr)))
    adt = dt * (-jnp.exp(alog_ref[...]))
    acs = _dot_exact_lhs(tri_ref[...], adt)
    acs_t = _dot_exact_rhs(adt.T, trit_ref[...])

    stack = jnp.concatenate([dt, jnp.exp(acs), jnp.exp(acs[t - 1:t, :] - acs)], axis=0)
    wide = _dot_exact_rhs(stack, exp_ref[...])
    dt_x, eacs_x, dst_x = wide[0:t], wide[t:2 * t], wide[2 * t:3 * t]

    xdt = xh * dt_x
    xdt_bf = xdt.astype(BF16)
    xdec_bf = (xdt * dst_x).astype(BF16)

    row = lax.broadcasted_iota(jnp.int32, (t, t), 0)
    col = lax.broadcasted_iota(jnp.int32, (t, t), 1)
    tril = row >= col
    lane = lax.broadcasted_iota(jnp.int32, (t, LANES), 1)
    heads_per_group = N_SSM_HEADS // N_GROUPS
    nt = (((1,), (1,)), ((), ()))
    tn = (((0,), (0,)), ((), ()))

    y_parts = []
    for g in range(N_GROUPS):
        bg = bc[:, g * D_STATE:(g + 1) * D_STATE].astype(BF16)
        cg = bc[:, (N_GROUPS + g) * D_STATE:(N_GROUPS + g + 1) * D_STATE].astype(BF16)
        cb = _dot(cg, bg, nt)
        for pair in range(heads_per_group // 2):
            ms = []
            for r in (g * heads_per_group + 2 * pair, g * heads_per_group + 2 * pair + 1):
                seg = acs[:, r:r + 1] - acs_t[r:r + 1, :]
                ms.append((cb * jnp.exp(jnp.where(tril, seg, NEG))).astype(BF16))
            q = g * (heads_per_group // 2) + pair
            y2 = _dot(jnp.concatenate(ms, axis=0), xdt_bf[:, q * LANES:(q + 1) * LANES])
            y_parts.append(jnp.where(lane < SSM_HEAD_DIM, y2[0:t], y2[t:2 * t]))
    y = jnp.concatenate(y_parts, axis=1)

    off_parts = []
    for g in range(N_GROUPS):
        sl = slice(g * GROUP_WIDTH, (g + 1) * GROUP_WIDTH)
        bg = bc[:, g * D_STATE:(g + 1) * D_STATE].astype(BF16)
        cg = bc[:, (N_GROUPS + g) * D_STATE:(N_GROUPS + g + 1) * D_STATE].astype(BF16)
        st = state_ref[:, sl]
        off_parts.append(_dot(cg, st.astype(BF16)))
        state_ref[:, sl] = eacs_x[t - 1:t, sl] * st + _dot(bg, xdec_bf[:, sl], tn)
    y = y + jnp.concatenate(off_parts, axis=1) * eacs_x + dsk_ref[...] * xh
    y = y * _silu(z_ref[0])

    outs = []
    for g in range(N_GROUPS):
        sl = slice(g * GROUP_WIDTH, (g + 1) * GROUP_WIDTH)
        outs.append(_rms(y[:, sl]) * ng_ref[:, sl])
    o_ref[0] = jnp.concatenate(outs, axis=1).astype(o_ref.dtype)


def _ssd(proj3, conv_w, conv_b, dt_bias, a_log, d_skip, norm_g):
    bsz, seq, _ = proj3.shape
    t = CHUNK
    gn2 = 2 * N_GROUPS * D_STATE
    pad_h = LANES - N_SSM_HEADS
    tri = jnp.tril(jnp.ones((t, t), F32)).astype(BF16)
    expand = jnp.pad(jnp.repeat(jnp.eye(N_SSM_HEADS, dtype=F32), SSM_HEAD_DIM, axis=1),
                     ((0, pad_h), (0, 0))).astype(BF16)
    const = lambda shape: pl.BlockSpec(shape, lambda b, c: (0,) * len(shape))
    return pl.pallas_call(
        _ssd_kernel,
        grid=(bsz, seq // t),
        in_specs=[pl.BlockSpec((1, t, SSM_WIDTH), lambda b, c: (b, c, COL_XS // SSM_WIDTH)),
                  pl.BlockSpec((1, t, SSM_WIDTH), lambda b, c: (b, c, COL_Z // SSM_WIDTH)),
                  pl.BlockSpec((1, t, gn2), lambda b, c: (b, c, COL_BC // gn2)),
                  pl.BlockSpec((1, t, LANES), lambda b, c: (b, c, COL_DT // LANES)),
                  const((SSM_CONV, SSM_WIDTH)), const((SSM_CONV, gn2)),
                  const((1, SSM_WIDTH)), const((1, gn2)),
                  const((1, LANES)), const((1, LANES)),
                  const((1, SSM_WIDTH)), const((1, SSM_WIDTH)),
                  const((t, t)), const((t, t)), const((LANES, SSM_WIDTH))],
        out_specs=pl.BlockSpec((1, t, SSM_WIDTH), lambda b, c: (b, c, 0)),
        out_shape=jax.ShapeDtypeStruct((bsz, seq, SSM_WIDTH), BF16),
        scratch_shapes=[pltpu.VMEM((SUBLANES, SSM_WIDTH), F32),
                        pltpu.VMEM((SUBLANES, gn2), F32),
                        pltpu.VMEM((D_STATE, SSM_WIDTH), F32)],
        compiler_params=_params(("arbitrary", "arbitrary")),
    )(proj3, proj3, proj3, proj3,
      conv_w[:, :SSM_WIDTH], conv_w[:, SSM_WIDTH:],
      conv_b[:SSM_WIDTH].reshape(1, -1), conv_b[SSM_WIDTH:].reshape(1, -1),
      jnp.pad(dt_bias, (0, pad_h)).reshape(1, LANES), jnp.pad(a_log, (0, pad_h)).reshape(1, LANES),
      jnp.repeat(d_skip, SSM_HEAD_DIM).reshape(1, SSM_WIDTH), norm_g.reshape(1, SSM_WIDTH),
      tri, tri.T, expand)


def _outproj_kernel(attn_ref, ssm_ref, x_ref, ag_ref, w_ref, gm_ref, g2_ref, sc_ref, sh_ref,
                    x1_ref, h2_ref, *, sub):
    for r0 in range(0, x_ref.shape[0], sub):
        rows = slice(r0, r0 + sub)
        a = _rms(attn_ref[rows, :]) * ag_ref[...]
        lhs = jnp.concatenate([a.astype(BF16), ssm_ref[rows, :]], axis=-1)
        x1 = x_ref[rows, :] + gm_ref[0] * _dot(lhs, w_ref[...])
        x1_ref[rows, :] = x1
        y = _rms(x1) * g2_ref[...]
        h2_ref[rows, :] = (y * (1.0 + sc_ref[0]) + sh_ref[0]).astype(BF16)


def _outproj(attn2, ssm2, x2, attn_g, w_out_bf, gate_m, g2, scale_f, shift_f, seq, tm, sub):
    m, d = x2.shape
    per_b = seq // tm
    vec = pl.BlockSpec((1, 1, d), lambda i: (i // per_b, 0, 0))
    half = pl.BlockSpec((tm, ATTN_WIDTH), lambda i: (i, 0))
    full = pl.BlockSpec((tm, d), lambda i: (i, 0))
    return pl.pallas_call(
        functools.partial(_outproj_kernel, sub=sub),
        grid=(m // tm,),
        in_specs=[half, half, full,
                  pl.BlockSpec((1, ATTN_WIDTH), lambda i: (0, 0)),
                  _resident((d, d)),
                  vec, pl.BlockSpec((1, d), lambda i: (0, 0)), vec, vec],
        out_specs=[full, full],
        out_shape=[jax.ShapeDtypeStruct((m, d), F32), jax.ShapeDtypeStruct((m, d), BF16)],
        compiler_params=_params(("parallel",)),
    )(attn2, ssm2, x2, attn_g.reshape(1, -1), w_out_bf, gate_m, g2.reshape(1, d), scale_f, shift_f)


def _ffn_up_kernel(h_ref, wg_ref, wv_ref, cw_ref, cb_ref, wnext_ref, o_ref, wnext_bf_ref, wb_ref, tail_ref,
                   *, tm, sub, seq):
    i = pl.program_id(1)
    pad = SUBLANES
    tn = o_ref.shape[1]

    @pl.when(i == 0)
    def _():
        wb_ref[:, 0:tn] = wg_ref[...].astype(BF16)
        wb_ref[:, tn:2 * tn] = wv_ref[...].astype(BF16)

    @pl.when((i * tm) % seq == 0)
    def _():
        tail_ref[...] = jnp.zeros((pad, 2 * tn), F32)

    tail = tail_ref[...]
    for r0 in range(0, tm, sub):
        u = _dot(h_ref[r0:r0 + sub, :], wb_ref[...])
        acc = _causal_conv(tail, u, cw_ref, cb_ref[...])
        tail = u[sub - pad:sub, :]
        o_ref[r0:r0 + sub, :] = (_silu(acc[:, 0:tn]) * acc[:, tn:2 * tn]).astype(o_ref.dtype)
    tail_ref[...] = tail
    wnext_bf_ref[...] = wnext_ref[...].astype(BF16)


def _ffn_up(h2, w_up, conv_w, conv_b, w_next, seq, tm, sub, tn):
    m, d = h2.shape
    nj = FFN_DIM // tn
    ni = m // tm
    ride = pl.BlockSpec((w_next.shape[0] // (nj * ni), w_next.shape[1]), lambda j, i: (j * ni + i, 0))
    assert w_next.shape[0] % (nj * ni * 2 * SUBLANES) == 0
    pair = lambda a: jnp.concatenate([a[:, :FFN_DIM].reshape(-1, nj, tn), a[:, FFN_DIM:].reshape(-1, nj, tn)],
                                     axis=2).reshape(-1, 2 * FFN_DIM)
    return pl.pallas_call(
        functools.partial(_ffn_up_kernel, tm=tm, sub=sub, seq=seq),
        grid=(nj, ni),
        in_specs=[pl.BlockSpec((tm, d), lambda j, i: (i, 0)),
                  pl.BlockSpec((d, tn), lambda j, i: (0, j)),
                  pl.BlockSpec((d, tn), lambda j, i: (0, j + nj)),
                  pl.BlockSpec((FFN_CONV, 2 * tn), lambda j, i: (0, j)),
                  pl.BlockSpec((1, 2 * tn), lambda j, i: (0, j)),
                  ride],
        out_specs=[pl.BlockSpec((tm, tn), lambda j, i: (i, j)), ride],
        out_shape=[jax.ShapeDtypeStruct((m, FFN_DIM), BF16), jax.ShapeDtypeStruct(w_next.shape, BF16)],
        scratch_shapes=[pltpu.VMEM((d, 2 * tn), BF16), pltpu.VMEM((SUBLANES, 2 * tn), F32)],
        compiler_params=_params(("arbitrary", "arbitrary")),
    )(h2, w_up, w_up, pair(conv_w), pair(conv_b.reshape(1, -1)), w_next)


def _ffn_down_kernel(a_ref, w_ref, x1_ref, gf_ref, fg_ref, o_ref):
    x2 = x1_ref[...] + gf_ref[0] * _dot(a_ref[...], w_ref[...])
    o_ref[...] = _rms(x2) * fg_ref[...]


def _ffn_down(act, w_down_bf, x1, gate_f, final_g, seq, tm):
    m, d = x1.shape
    per_b = seq // tm
    row = lambda width: pl.BlockSpec((tm, width), lambda i: (i, 0))
    return pl.pallas_call(
        _ffn_down_kernel,
        grid=(m // tm,),
        in_specs=[row(FFN_DIM), _resident((FFN_DIM, d)), row(d),
                  pl.BlockSpec((1, 1, d), lambda i: (i // per_b, 0, 0)),
                  pl.BlockSpec((1, d), lambda i: (0, 0))],
        out_specs=row(d),
        out_shape=jax.ShapeDtypeStruct((m, d), F32),
        compiler_params=_params(("parallel",)),
    )(act, w_down_bf, x1, gate_f, final_g.reshape(1, d))


def kernel(x, c, w_ada, b_ada, norm_mix_g, w_in, rel_bias, attn_norm_g, conv_ssm_w, conv_ssm_b, dt_bias,
           a_log, d_skip, ssm_norm_g, w_out, norm_ffn_g, w_up, conv_ffn_w, conv_ffn_b, w_down, final_norm_g):
    bsz, seq, d = x.shape
    m = bsz * seq
    assert w_ada.shape[0] == 1, "the final RMSNorm is fused into the (single) layer's ffn_down kernel"
    l = 0
    x2 = x.reshape(m, d)
    mod = _ada(c, w_ada[l], b_ada[l])
    shift_m, scale_m, gate_m, shift_f, scale_f, gate_f = [
        mod[:, k * d:(k + 1) * d].reshape(bsz, 1, d) for k in range(6)]

    w_in_t = w_in[l].T
    proj, w_out_bf = _normproj(x2, norm_mix_g[l], scale_m, shift_m, w_in_t.astype(BF16), PROJ_COLS, w_out[l],
                               seq, 256, 256)
    proj3 = proj.reshape(bsz, seq, PROJ_COLS)

    attn = _moba(rel_bias, proj3)
    ssm = _ssd(proj3, conv_ssm_w[l], conv_ssm_b[l], dt_bias[l], a_log[l], d_skip[l], ssm_norm_g[l])

    x1, h2 = _outproj(attn.reshape(m, ATTN_WIDTH), ssm.reshape(m, SSM_WIDTH), x2, attn_norm_g[l],
                      w_out_bf, gate_m, norm_ffn_g[l], scale_f, shift_f, seq, 512, 256)
    act, w_down_bf = _ffn_up(h2, w_up[l], conv_ffn_w[l], conv_ffn_b[l], w_down[l], seq, 1024, 256, 512)
    out = _ffn_down(act, w_down_bf, x1, gate_f, final_norm_g, seq, 256)
    return out.reshape(bsz, seq, d)
```

```python
import functools
import math

import jax
import jax.numpy as jnp
from jax import lax
from jax.experimental import pallas as pl
from jax.experimental.pallas import tpu as pltpu

F32 = jnp.float32
BF16 = jnp.bfloat16

D_MODEL = 2048
ATTN_WIDTH = 1024
HEAD_DIM = 128
N_HEADS = 8
SSM_WIDTH = 1024
SSM_HEAD_DIM = 64
N_SSM_HEADS = 16
N_GROUPS = 2
GROUP_WIDTH = SSM_WIDTH // N_GROUPS
D_STATE = 128
SSM_CONV = 4
CHUNK = 256
MOBA_BLOCK = 256
MOBA_TOPK = 3
MOBA_FAR_GROUP = 4
MOBA_UNROLL = 2
REL_BUCKETS = 32
REL_MAX_DIST = 128
FFN_DIM = 5632
FFN_CONV = 3
EPS = 1e-6
NEG = -1e30

LANES = 128
SUBLANES = 8
VMEM_LIMIT = 56 * 1024 * 1024

PROJ_COLS = 5760
COL_XS = 3 * ATTN_WIDTH
COL_Z = COL_XS + SSM_WIDTH
COL_BC = COL_Z + SSM_WIDTH
COL_DT = COL_BC + 2 * N_GROUPS * D_STATE


def _params(sem):
    return pltpu.CompilerParams(dimension_semantics=sem, vmem_limit_bytes=VMEM_LIMIT)


def _split3(x):
    hi = x.astype(BF16)
    r = x - hi.astype(F32)
    mid = r.astype(BF16)
    lo = (r - mid.astype(F32)).astype(BF16)
    return hi, mid, lo


NT = (((1,), (1,)), ((), ()))


def _dot(a, b, dims=(((1,), (0,)), ((), ()))):
    return lax.dot_general(a, b, dims, preferred_element_type=F32)


def _dot_exact_lhs(a_bf, x, dims=(((1,), (0,)), ((), ()))):
    hi, mid, lo = _split3(x)
    return _dot(a_bf, hi, dims) + _dot(a_bf, mid, dims) + _dot(a_bf, lo, dims)


def _dot_exact_rhs(x, b_bf, dims=(((1,), (0,)), ((), ()))):
    hi, mid, lo = _split3(x)
    return _dot(hi, b_bf, dims) + _dot(mid, b_bf, dims) + _dot(lo, b_bf, dims)


def _silu(x):
    return x * jax.nn.sigmoid(x)


def _rms(x):
    return x * lax.rsqrt(jnp.mean(x * x, axis=-1, keepdims=True) + EPS)


def _causal_conv(tail, x, w_ref, b):
    rows, width = x.shape
    taps = w_ref.shape[0]
    ext = jnp.concatenate([tail, x], axis=0).reshape(rows // SUBLANES + 1, SUBLANES, width)
    row_in_slab = lax.broadcasted_iota(jnp.int32, (rows // SUBLANES, SUBLANES, width), 1)
    acc = b + w_ref[taps - 1:taps, :] * x
    rot = ext
    for back in range(1, taps):
        rot = pltpu.roll(rot, 1, axis=1)
        shifted = jnp.where(row_in_slab < back, rot[:-1], rot[1:]).reshape(rows, width)
        acc = acc + w_ref[taps - 1 - back:taps - back, :] * shifted
    return acc


def _ada_kernel(ct_ref, w_ref, b_ref, o_ref, sb_ref):
    nb = sb_ref.shape[0]
    d = w_ref.shape[0]
    tn = o_ref.shape[-1]

    @pl.when(pl.program_id(0) == 0)
    def _():
        ct = ct_ref[...]
        st = _silu(ct)
        for b in range(nb):
            sb_ref[b] = jnp.broadcast_to(st[:, b:b + 1], (d, LANES))

    def body(kc, accs):
        r = pl.multiple_of(kc * SUBLANES, SUBLANES)
        w8 = w_ref[pl.ds(r, SUBLANES), :]
        out = []
        for b in range(nb):
            s8 = sb_ref[b, pl.ds(r, SUBLANES), :]
            out.append(accs[b] + w8 * jnp.tile(s8, (1, tn // LANES)))
        return tuple(out)

    accs = lax.fori_loop(0, d // SUBLANES, body,
                         tuple(jnp.zeros((SUBLANES, tn), F32) for _ in range(nb)), unroll=8)
    for b in range(nb):
        o_ref[b:b + 1, :] = jnp.sum(accs[b], axis=0, keepdims=True) + b_ref[...]


def _ada(c, w_ada, b_ada, tn=1024):
    nb, d = c.shape
    n = w_ada.shape[1]
    return pl.pallas_call(
        _ada_kernel,
        grid=(n // tn,),
        in_specs=[pl.BlockSpec((d, nb), lambda j: (0, 0)),
                  pl.BlockSpec((d, tn), lambda j: (0, j)),
                  pl.BlockSpec((1, tn), lambda j: (0, j))],
        out_specs=pl.BlockSpec((nb, tn), lambda j: (0, j)),
        out_shape=jax.ShapeDtypeStruct((nb, n), F32),
        scratch_shapes=[pltpu.VMEM((nb, d, LANES), F32)],
        compiler_params=_params(("arbitrary",)),
    )(c.T, w_ada, b_ada.reshape(1, n))


def _normproj_kernel(x_ref, g_ref, sc_ref, sh_ref, w_ref, wnext_ref, o_ref, wnext_bf_ref, *, sub):
    n = w_ref.shape[0]
    for r0 in range(0, x_ref.shape[0], sub):
        rows = slice(r0, r0 + sub)
        y = _rms(x_ref[rows, :]) * g_ref[...]
        h = y * (1.0 + sc_ref[0]) + sh_ref[0]
        o_ref[rows, 0:n] = _dot(h.astype(BF16), w_ref[...], NT)
        if n < o_ref.shape[1]:
            o_ref[rows, n:] = jnp.zeros((sub, o_ref.shape[1] - n), F32)
    wnext_bf_ref[...] = wnext_ref[...].astype(BF16)


def _resident(shape):
    return pl.BlockSpec(shape, lambda *_: (0,) * len(shape), pipeline_mode=pl.Buffered(1))


def _normproj(x2, g, scale, shift, w_t, n_out, w_next, seq, tm, sub):
    m, d = x2.shape
    per_b = seq // tm
    steps = m // tm
    assert w_next.shape[0] % (steps * 2 * SUBLANES) == 0
    ride = pl.BlockSpec((w_next.shape[0] // steps, w_next.shape[1]), lambda i: (i, 0))
    vec = pl.BlockSpec((1, 1, d), lambda i: (i // per_b, 0, 0))
    return pl.pallas_call(
        functools.partial(_normproj_kernel, sub=sub),
        grid=(m // tm,),
        in_specs=[pl.BlockSpec((tm, d), lambda i: (i, 0)),
                  pl.BlockSpec((1, d), lambda i: (0, 0)),
                  vec, vec, _resident(w_t.shape), ride],
        out_specs=[pl.BlockSpec((tm, n_out), lambda i: (i, 0)), ride],
        out_shape=[jax.ShapeDtypeStruct((m, n_out), F32), jax.ShapeDtypeStruct(w_next.shape, BF16)],
        compiler_params=_params(("parallel",)),
    )(x2, g.reshape(1, d), scale, shift, w_t, w_next)


def _rel_bucket(dist):
    n = jnp.maximum(dist, 0)
    max_exact = REL_BUCKETS // 2
    nf = jnp.maximum(n, max_exact).astype(F32)
    large = max_exact + (jnp.log(nf / max_exact) / math.log(REL_MAX_DIST / max_exact)
                         * (REL_BUCKETS - max_exact)).astype(jnp.int32)
    large = jnp.minimum(large, REL_BUCKETS - 1)
    return jnp.where(n < max_exact, n, large)


def _moba_units(nblk, grp):
    lead = grp - 1
    units = []
    for qi in range(nblk):
        units.append((qi, qi, 1))
        for t in range(-(-max(qi - lead, 0) // grp)):
            units.append((qi, lead + t * grp, 0))
    return units


def _moba_kernel(uq_ref, uk_ref, uf_ref, rb_ref, q_ref, k_ref, v_ref, o_ref,
                 qbf_ref, kbf_ref, vt_ref, tab_ref, mask_ref, s_buf, p_buf, alpha_buf, m_ref, l_ref, acc_ref,
                 *, units):
    h = pl.program_id(0)
    seq = k_ref.shape[1]
    blk = MOBA_BLOCK
    grp = MOBA_FAR_GROUP
    lead = grp - 1
    nblk = seq // blk
    nunits = len(units)
    scale = HEAD_DIM ** -0.5
    exp2_scale = scale * math.log2(math.e)

    kf = k_ref[0]
    qf = q_ref[0]
    qbf_ref[...] = qf.astype(BF16)
    kbf_ref[0:lead * blk, :] = jnp.zeros((lead * blk, HEAD_DIM), BF16)
    kbf_ref[lead * blk:, :] = kf.astype(BF16)
    kmean = jnp.mean(kf.reshape(nblk, blk, HEAD_DIM), axis=1)
    for j in range(lead):
        vt_ref[j] = jnp.zeros((HEAD_DIM, blk), BF16)
    for j in range(nblk):
        vt_ref[lead + j] = v_ref[0, j * blk:(j + 1) * blk, :].T.astype(BF16)

    @pl.when(pl.program_id(1) == 0)
    def _():
        b_far = rb_ref[REL_BUCKETS - 1, h]
        kk = lax.broadcasted_iota(jnp.int32, (blk, blk), 0)
        qq = lax.broadcasted_iota(jnp.int32, (blk, blk), 1)
        tab_ref[0] = jnp.zeros((2 * blk, blk), F32)
        for pos, dist in ((0, qq - kk + blk), (1, qq - kk)):
            bucket = _rel_bucket(dist)
            tab = jnp.zeros((blk, blk), F32)
            for b in range(REL_BUCKETS):
                tab = jnp.where(bucket == b, (rb_ref[b, h] - b_far) / scale, tab)
            tab_ref[1, pos * blk:(pos + 1) * blk, :] = jnp.where(dist >= 0, tab, NEG)

    q_hi, q_mid, _ = _split3(qf)
    k_hi, k_mid, _ = _split3(kmean)
    gate = _dot(k_hi, q_hi, NT) + _dot(k_hi, q_mid, NT) + _dot(k_mid, q_hi, NT)
    nidx = lax.broadcasted_iota(jnp.int32, (nblk, seq), 0)
    qblk = lax.broadcasted_iota(jnp.int32, (nblk, seq), 1) // blk
    nidx_f = nidx.astype(F32)
    avail = jnp.where(nidx < qblk, 1.0, 0.0)
    chosen = jnp.zeros((nblk, seq), F32)
    for _ in range(MOBA_TOPK):
        gm = jnp.where(avail > 0.0, gate, -jnp.inf)
        best = jnp.max(gm, axis=0, keepdims=True)
        first = jnp.min(jnp.where((gm == best) & (avail > 0.0), nidx_f, float(nblk)), axis=0, keepdims=True)
        hit = nidx_f == first
        chosen = jnp.where(hit, 1.0, chosen)
        avail = jnp.where(hit, 0.0, avail)
    add_all = jnp.where(chosen > 0.0, 0.0, NEG)
    for u, (qi, ks, is_first) in enumerate(units):
        cols = slice(qi * blk, (qi + 1) * blk)
        for n in range(grp):
            j = ks + n - lead
            if j == qi and is_first:
                row = jnp.zeros((1, blk), F32)
            elif 0 <= j < (qi if is_first else qi - lead):
                row = add_all[j:j + 1, cols]
            else:
                row = jnp.full((1, blk), NEG, F32)
            mask_ref[u * grp + n] = jnp.broadcast_to(row, (SUBLANES, blk))

    def unit(u):
        uc = jnp.clip(u, 0, nunits - 1)
        return uq_ref[uc], uk_ref[uc], uf_ref[uc], uc

    def scores(slot, u):
        qi, ks, is_first, uc = unit(u)
        qb = qbf_ref[pl.ds(pl.multiple_of(qi * blk, blk), blk), :]
        kg = kbf_ref[pl.ds(pl.multiple_of(ks * blk, blk), grp * blk), :]
        s = _dot(kg, qb, NT)
        tab = tab_ref[is_first]
        parts = []
        for n in range(grp):
            part = s[n * blk:(n + 1) * blk] + mask_ref[uc * grp + n][0:1, :]
            if n >= grp - 2:
                part = part + tab[(n - grp + 2) * blk:(n - grp + 3) * blk]
            parts.append(part)
        s_buf[slot] = jnp.concatenate(parts, axis=0)

    def softmax(slot, u):
        qi, _, is_first, _ = unit(u)
        st = qi % 2
        s = s_buf[slot]
        m_old = jnp.where(is_first == 1, NEG, m_ref[st])
        l_old = jnp.where(is_first == 1, 0.0, l_ref[st])
        m_new = jnp.maximum(m_old, jnp.max(s, axis=0, keepdims=True))
        alpha = jnp.exp2((m_old - m_new) * exp2_scale)
        p = jnp.exp2((s - m_new) * exp2_scale)
        m_ref[st] = m_new
        l_ref[st] = alpha * l_old + jnp.sum(p, axis=0, keepdims=True)
        alpha_buf[slot] = alpha
        p_buf[slot] = p.astype(BF16)

    def values(slot, u):
        qi, ks, _, _ = unit(u)
        st = qi % 2
        acc = alpha_buf[slot] * acc_ref[st]
        for n in range(grp):
            acc = acc + _dot(vt_ref[ks + n], p_buf[slot, n * blk:(n + 1) * blk, :])
        acc_ref[st] = acc
        o_ref[0, pl.ds(pl.multiple_of(qi * blk, blk), blk), :] = (acc * (1.0 / l_ref[st])).T

    m_ref[...] = jnp.full(m_ref.shape, NEG, F32)
    l_ref[...] = jnp.ones(l_ref.shape, F32)
    acc_ref[...] = jnp.zeros(acc_ref.shape, F32)
    alpha_buf[1] = jnp.ones((1, blk), F32)
    p_buf[1] = jnp.zeros((grp * blk, blk), BF16)
    scores(0, 0)

    def trip(u, slot):
        values(1 - slot, u - 1)
        softmax(slot, u)
        scores(1 - slot, u + 1)

    def trips(it, carry):
        for n in range(MOBA_UNROLL):
            trip(MOBA_UNROLL * it + n, n % 2)
        return carry

    lax.fori_loop(0, nunits // MOBA_UNROLL, trips, 0)
    for u in range(nunits - nunits % MOBA_UNROLL, nunits):
        trip(u, u % 2)
    values((nunits - 1) % 2, nunits - 1)


def _moba(rel_bias, proj3):
    bsz, seq, _ = proj3.shape
    blk = MOBA_BLOCK
    grp = MOBA_FAR_GROUP
    nblk = seq // blk
    assert seq % blk == 0 and grp >= 2
    units = _moba_units(nblk, grp)
    uq, uk, uf = (jnp.asarray([u[c] for u in units], jnp.int32) for c in range(3))
    smem = pl.BlockSpec(memory_space=pltpu.SMEM)
    head = lambda col0: pl.BlockSpec((1, seq, HEAD_DIM), lambda h, b: (b, 0, col0 + h))
    return pl.pallas_call(
        functools.partial(_moba_kernel, units=units),
        grid=(N_HEADS, bsz),
        in_specs=[smem, smem, smem, smem, head(0), head(N_HEADS), head(2 * N_HEADS)],
        out_specs=head(0),
        out_shape=jax.ShapeDtypeStruct((bsz, seq, ATTN_WIDTH), F32),
        scratch_shapes=[pltpu.VMEM((seq, HEAD_DIM), BF16),
                        pltpu.VMEM((seq + (grp - 1) * blk, HEAD_DIM), BF16),
                        pltpu.VMEM((nblk + grp - 1, HEAD_DIM, blk), BF16),
                        pltpu.VMEM((2, 2 * blk, blk), F32),
                        pltpu.VMEM((len(units) * grp, SUBLANES, blk), F32),
                        pltpu.VMEM((2, grp * blk, blk), F32),
                        pltpu.VMEM((2, grp * blk, blk), BF16),
                        pltpu.VMEM((2, 1, blk), F32),
                        pltpu.VMEM((2, 1, blk), F32),
                        pltpu.VMEM((2, 1, blk), F32),
                        pltpu.VMEM((2, HEAD_DIM, blk), F32)],
        compiler_params=_params(("arbitrary", "arbitrary")),
    )(uq, uk, uf, rel_bias, proj3, proj3, proj3)


def _ssd_kernel(xs_ref, z_ref, bc_ref, dt_ref, cwx_ref, cwbc_ref, cbx_ref, cbbc_ref,
                dtb_ref, alog_ref, dsk_ref, ng_ref, tri_ref, trit_ref, exp_ref, o_ref,
                xtail_ref, bctail_ref, state_ref):
    c = pl.program_id(1)
    t = CHUNK
    pad = SUBLANES

    @pl.when(c == 0)
    def _():
        xtail_ref[...] = jnp.zeros_like(xtail_ref)
        bctail_ref[...] = jnp.zeros_like(bctail_ref)
        state_ref[...] = jnp.zeros_like(state_ref)

    def conv_silu(tail_ref, src_ref, w_ref, b_ref):
        x = src_ref[0]
        acc = _causal_conv(tail_ref[...], x, w_ref, b_ref[...])
        tail_ref[...] = x[t - pad:t, :]
        return _silu(acc)

    xh = conv_silu(xtail_ref, xs_ref, cwx_ref, cbx_ref)
    bc = conv_silu(bctail_ref, bc_ref, cwbc_ref, cbbc_ref)

    dtr = dt_ref[0] + dtb_ref[...]
    dt = jnp.maximum(dtr, 0.0) + jnp.log1p(jnp.exp(-jnp.abs(dtr)))
    adt = dt * (-jnp.exp(alog_ref[...]))
    acs = _dot_exact_lhs(tri_ref[...], adt)
    acs_t = _dot_exact_rhs(adt.T, trit_ref[...])

    stack = jnp.concatenate([dt, jnp.exp(acs), jnp.exp(acs[t - 1:t, :] - acs)], axis=0)
    wide = _dot_exact_rhs(stack, exp_ref[...])
    dt_x, eacs_x, dst_x = wide[0:t], wide[t:2 * t], wide[2 * t:3 * t]

    xdt = xh * dt_x
    xdt_bf = xdt.astype(BF16)
    xdec_bf = (xdt * dst_x).astype(BF16)

    row = lax.broadcasted_iota(jnp.int32, (t, t), 0)
    col = lax.broadcasted_iota(jnp.int32, (t, t), 1)
    tril = row >= col
    lane = lax.broadcasted_iota(jnp.int32, (t, LANES), 1)
    heads_per_group = N_SSM_HEADS // N_GROUPS
    nt = (((1,), (1,)), ((), ()))
    tn = (((0,), (0,)), ((), ()))

    y_parts = []
    for g in range(N_GROUPS):
        bg = bc[:, g * D_STATE:(g + 1) * D_STATE].astype(BF16)
        cg = bc[:, (N_GROUPS + g) * D_STATE:(N_GROUPS + g + 1) * D_STATE].astype(BF16)
        cb = _dot(cg, bg, nt)
        for pair in range(heads_per_group // 2):
            ms = []
            for r in (g * heads_per_group + 2 * pair, g * heads_per_group + 2 * pair + 1):
                seg = acs[:, r:r + 1] - acs_t[r:r + 1, :]
                ms.append((cb * jnp.exp(jnp.where(tril, seg, NEG))).astype(BF16))
            q = g * (heads_per_group // 2) + pair
            y2 = _dot(jnp.concatenate(ms, axis=0), xdt_bf[:, q * LANES:(q + 1) * LANES])
            y_parts.append(jnp.where(lane < SSM_HEAD_DIM, y2[0:t], y2[t:2 * t]))
    y = jnp.concatenate(y_parts, axis=1)

    off_parts = []
    for g in range(N_GROUPS):
        sl = slice(g * GROUP_WIDTH, (g + 1) * GROUP_WIDTH)
        bg = bc[:, g * D_STATE:(g + 1) * D_STATE].astype(BF16)
        cg = bc[:, (N_GROUPS + g) * D_STATE:(N_GROUPS + g + 1) * D_STATE].astype(BF16)
        st = state_ref[:, sl]
        off_parts.append(_dot(cg, st.astype(BF16)))
        state_ref[:, sl] = eacs_x[t - 1:t, sl] * st + _dot(bg, xdec_bf[:, sl], tn)
    y = y + jnp.concatenate(off_parts, axis=1) * eacs_x + dsk_ref[...] * xh
    y = y * _silu(z_ref[0])

    outs = []
    for g in range(N_GROUPS):
        sl = slice(g * GROUP_WIDTH, (g + 1) * GROUP_WIDTH)
        outs.append(_rms(y[:, sl]) * ng_ref[:, sl])
    o_ref[0] = jnp.concatenate(outs, axis=1).astype(o_ref.dtype)


def _ssd(proj3, conv_w, conv_b, dt_bias, a_log, d_skip, norm_g):
    bsz, seq, _ = proj3.shape
    t = CHUNK
    gn2 = 2 * N_GROUPS * D_STATE
    pad_h = LANES - N_SSM_HEADS
    tri = jnp.tril(jnp.ones((t, t), F32)).astype(BF16)
    expand = jnp.pad(jnp.repeat(jnp.eye(N_SSM_HEADS, dtype=F32), SSM_HEAD_DIM, axis=1),
                     ((0, pad_h), (0, 0))).astype(BF16)
    const = lambda shape: pl.BlockSpec(shape, lambda b, c: (0,) * len(shape))
    return pl.pallas_call(
        _ssd_kernel,
        grid=(bsz, seq // t),
        in_specs=[pl.BlockSpec((1, t, SSM_WIDTH), lambda b, c: (b, c, COL_XS // SSM_WIDTH)),
                  pl.BlockSpec((1, t, SSM_WIDTH), lambda b, c: (b, c, COL_Z // SSM_WIDTH)),
                  pl.BlockSpec((1, t, gn2), lambda b, c: (b, c, COL_BC // gn2)),
                  pl.BlockSpec((1, t, LANES), lambda b, c: (b, c, COL_DT // LANES)),
                  const((SSM_CONV, SSM_WIDTH)), const((SSM_CONV, gn2)),
                  const((1, SSM_WIDTH)), const((1, gn2)),
                  const((1, LANES)), const((1, LANES)),
                  const((1, SSM_WIDTH)), const((1, SSM_WIDTH)),
                  const((t, t)), const((t, t)), const((LANES, SSM_WIDTH))],
        out_specs=pl.BlockSpec((1, t, SSM_WIDTH), lambda b, c: (b, c, 0)),
        out_shape=jax.ShapeDtypeStruct((bsz, seq, SSM_WIDTH), BF16),
        scratch_shapes=[pltpu.VMEM((SUBLANES, SSM_WIDTH), F32),
                        pltpu.VMEM((SUBLANES, gn2), F32),
                        pltpu.VMEM((D_STATE, SSM_WIDTH), F32)],
        compiler_params=_params(("arbitrary", "arbitrary")),
    )(proj3, proj3, proj3, proj3,
      conv_w[:, :SSM_WIDTH], conv_w[:, SSM_WIDTH:],
      conv_b[:SSM_WIDTH].reshape(1, -1), conv_b[SSM_WIDTH:].reshape(1, -1),
      jnp.pad(dt_bias, (0, pad_h)).reshape(1, LANES), jnp.pad(a_log, (0, pad_h)).reshape(1, LANES),
      jnp.repeat(d_skip, SSM_HEAD_DIM).reshape(1, SSM_WIDTH), norm_g.reshape(1, SSM_WIDTH),
      tri, tri.T, expand)


def _outproj_kernel(attn_ref, ssm_ref, x_ref, ag_ref, w_ref, gm_ref, g2_ref, sc_ref, sh_ref,
                    x1_ref, h2_ref, *, sub):
    for r0 in range(0, x_ref.shape[0], sub):
        rows = slice(r0, r0 + sub)
        a = _rms(attn_ref[rows, :]) * ag_ref[...]
        lhs = jnp.concatenate([a.astype(BF16), ssm_ref[rows, :]], axis=-1)
        x1 = x_ref[rows, :] + gm_ref[0] * _dot(lhs, w_ref[...])
        x1_ref[rows, :] = x1
        y = _rms(x1) * g2_ref[...]
        h2_ref[rows, :] = (y * (1.0 + sc_ref[0]) + sh_ref[0]).astype(BF16)


def _outproj(attn2, ssm2, x2, attn_g, w_out_bf, gate_m, g2, scale_f, shift_f, seq, tm, sub):
    m, d = x2.shape
    per_b = seq // tm
    vec = pl.BlockSpec((1, 1, d), lambda i: (i // per_b, 0, 0))
    half = pl.BlockSpec((tm, ATTN_WIDTH), lambda i: (i, 0))
    full = pl.BlockSpec((tm, d), lambda i: (i, 0))
    return pl.pallas_call(
        functools.partial(_outproj_kernel, sub=sub),
        grid=(m // tm,),
        in_specs=[half, half, full,
                  pl.BlockSpec((1, ATTN_WIDTH), lambda i: (0, 0)),
                  _resident((d, d)),
                  vec, pl.BlockSpec((1, d), lambda i: (0, 0)), vec, vec],
        out_specs=[full, full],
        out_shape=[jax.ShapeDtypeStruct((m, d), F32), jax.ShapeDtypeStruct((m, d), BF16)],
        compiler_params=_params(("parallel",)),
    )(attn2, ssm2, x2, attn_g.reshape(1, -1), w_out_bf, gate_m, g2.reshape(1, d), scale_f, shift_f)


def _ffn_up_kernel(h_ref, wg_ref, wv_ref, cw_ref, cb_ref, wnext_ref, o_ref, wnext_bf_ref, wb_ref, tail_ref,
                   *, tm, sub, seq):
    i = pl.program_id(1)
    pad = SUBLANES
    tn = o_ref.shape[1]

    @pl.when(i == 0)
    def _():
        wb_ref[:, 0:tn] = wg_ref[...].astype(BF16)
        wb_ref[:, tn:2 * tn] = wv_ref[...].astype(BF16)

    @pl.when((i * tm) % seq == 0)
    def _():
        tail_ref[...] = jnp.zeros((pad, 2 * tn), F32)

    tail = tail_ref[...]
    for r0 in range(0, tm, sub):
        u = _dot(h_ref[r0:r0 + sub, :], wb_ref[...])
        acc = _causal_conv(tail, u, cw_ref, cb_ref[...])
        tail = u[sub - pad:sub, :]
        o_ref[r0:r0 + sub, :] = (_silu(acc[:, 0:tn]) * acc[:, tn:2 * tn]).astype(o_ref.dtype)
    tail_ref[...] = tail
    wnext_bf_ref[...] = wnext_ref[...].astype(BF16)


def _ffn_up(h2, w_up, conv_w, conv_b, w_next, seq, tm, sub, tn):
    m, d = h2.shape
    nj = FFN_DIM // tn
    ni = m // tm
    ride = pl.BlockSpec((w_next.shape[0] // (nj * ni), w_next.shape[1]), lambda j, i: (j * ni + i, 0))
    assert w_next.shape[0] % (nj * ni * 2 * SUBLANES) == 0
    pair = lambda a: jnp.concatenate([a[:, :FFN_DIM].reshape(-1, nj, tn), a[:, FFN_DIM:].reshape(-1, nj, tn)],
                                     axis=2).reshape(-1, 2 * FFN_DIM)
    return pl.pallas_call(
        functools.partial(_ffn_up_kernel, tm=tm, sub=sub, seq=seq),
        grid=(nj, ni),
        in_specs=[pl.BlockSpec((tm, d), lambda j, i: (i, 0)),
                  pl.BlockSpec((d, tn), lambda j, i: (0, j)),
                  pl.BlockSpec((d, tn), lambda j, i: (0, j + nj)),
                  pl.BlockSpec((FFN_CONV, 2 * tn), lambda j, i: (0, j)),
                  pl.BlockSpec((1, 2 * tn), lambda j, i: (0, j)),
                  ride],
        out_specs=[pl.BlockSpec((tm, tn), lambda j, i: (i, j)), ride],
        out_shape=[jax.ShapeDtypeStruct((m, FFN_DIM), BF16), jax.ShapeDtypeStruct(w_next.shape, BF16)],
        scratch_shapes=[pltpu.VMEM((d, 2 * tn), BF16), pltpu.VMEM((SUBLANES, 2 * tn), F32)],
        compiler_params=_params(("arbitrary", "arbitrary")),
    )(h2, w_up, w_up, pair(conv_w), pair(conv_b.reshape(1, -1)), w_next)


def _ffn_down_kernel(a_ref, w_ref, x1_ref, gf_ref, fg_ref, o_ref):
    x2 = x1_ref[...] + gf_ref[0] * _dot(a_ref[...], w_ref[...])
    o_ref[...] = _rms(x2) * fg_ref[...]


def _ffn_down(act, w_down_bf, x1, gate_f, final_g, seq, tm):
    m, d = x1.shape
    per_b = seq // tm
    row = lambda width: pl.BlockSpec((tm, width), lambda i: (i, 0))
    return pl.pallas_call(
        _ffn_down_kernel,
        grid=(m // tm,),
        in_specs=[row(FFN_DIM), _resident((FFN_DIM, d)), row(d),
                  pl.BlockSpec((1, 1, d), lambda i: (i // per_b, 0, 0)),
                  pl.BlockSpec((1, d), lambda i: (0, 0))],
        out_specs=row(d),
        out_shape=jax.ShapeDtypeStruct((m, d), F32),
        compiler_params=_params(("parallel",)),
    )(act, w_down_bf, x1, gate_f, final_g.reshape(1, d))


def kernel(x, c, w_ada, b_ada, norm_mix_g, w_in, rel_bias, attn_norm_g, conv_ssm_w, conv_ssm_b, dt_bias,
           a_log, d_skip, ssm_norm_g, w_out, norm_ffn_g, w_up, conv_ffn_w, conv_ffn_b, w_down, final_norm_g):
    bsz, seq, d = x.shape
    m = bsz * seq
    assert w_ada.shape[0] == 1, "the final RMSNorm is fused into the (single) layer's ffn_down kernel"
    l = 0
    x2 = x.reshape(m, d)
    mod = _ada(c, w_ada[l], b_ada[l])
    shift_m, scale_m, gate_m, shift_f, scale_f, gate_f = [
        mod[:, k * d:(k + 1) * d].reshape(bsz, 1, d) for k in range(6)]

    w_in_t = w_in[l].T
    proj, w_out_bf = _normproj(x2, norm_mix_g[l], scale_m, shift_m, w_in_t.astype(BF16), PROJ_COLS, w_out[l],
                               seq, 256, 256)
    proj3 = proj.reshape(bsz, seq, PROJ_COLS)

    attn = _moba(rel_bias, proj3)
    ssm = _ssd(proj3, conv_ssm_w[l], conv_ssm_b[l], dt_bias[l], a_log[l], d_skip[l], ssm_norm_g[l])

    x1, h2 = _outproj(attn.reshape(m, ATTN_WIDTH), ssm.reshape(m, SSM_WIDTH), x2, attn_norm_g[l],
                      w_out_bf, gate_m, norm_ffn_g[l], scale_f, shift_f, seq, 512, 256)
    act, w_down_bf = _ffn_up(h2, w_up[l], conv_ffn_w[l], conv_ffn_b[l], w_down[l], seq, 1024, 256, 512)
    out = _ffn_down(act, w_down_bf, x1, gate_f, final_norm_g, seq, 256)
    return out.reshape(bsz, seq, d)
```

```python
import functools
import math

import jax
import jax.numpy as jnp
from jax import lax
from jax.experimental import pallas as pl
from jax.experimental.pallas import tpu as pltpu

F32 = jnp.float32
BF16 = jnp.bfloat16

D_MODEL = 2048
ATTN_WIDTH = 1024
HEAD_DIM = 128
N_HEADS = 8
SSM_WIDTH = 1024
SSM_HEAD_DIM = 64
N_SSM_HEADS = 16
N_GROUPS = 2
GROUP_WIDTH = SSM_WIDTH // N_GROUPS
D_STATE = 128
SSM_CONV = 4
CHUNK = 256
MOBA_BLOCK = 256
MOBA_TOPK = 3
MOBA_FAR_GROUP = 4
MOBA_UNROLL = 2
REL_BUCKETS = 32
REL_MAX_DIST = 128
FFN_DIM = 5632
FFN_CONV = 3
EPS = 1e-6
NEG = -1e30

LANES = 128
SUBLANES = 8
VMEM_LIMIT = 56 * 1024 * 1024

PROJ_COLS = 5760
COL_XS = 3 * ATTN_WIDTH
COL_Z = COL_XS + SSM_WIDTH
COL_BC = COL_Z + SSM_WIDTH
COL_DT = COL_BC + 2 * N_GROUPS * D_STATE


def _params(sem):
    return pltpu.CompilerParams(dimension_semantics=sem, vmem_limit_bytes=VMEM_LIMIT)


def _split3(x):
    hi = x.astype(BF16)
    r = x - hi.astype(F32)
    mid = r.astype(BF16)
    lo = (r - mid.astype(F32)).astype(BF16)
    return hi, mid, lo


NT = (((1,), (1,)), ((), ()))


def _dot(a, b, dims=(((1,), (0,)), ((), ()))):
    return lax.dot_general(a, b, dims, preferred_element_type=F32)


def _dot_exact_lhs(a_bf, x, dims=(((1,), (0,)), ((), ()))):
    hi, mid, lo = _split3(x)
    return _dot(a_bf, hi, dims) + _dot(a_bf, mid, dims) + _dot(a_bf, lo, dims)


def _dot_exact_rhs(x, b_bf, dims=(((1,), (0,)), ((), ()))):
    hi, mid, lo = _split3(x)
    return _dot(hi, b_bf, dims) + _dot(mid, b_bf, dims) + _dot(lo, b_bf, dims)


def _silu(x):
    return x * jax.nn.sigmoid(x)


def _rms(x):
    return x * lax.rsqrt(jnp.mean(x * x, axis=-1, keepdims=True) + EPS)


def _causal_conv(tail, x, w_ref, b):
    rows, width = x.shape
    taps = w_ref.shape[0]
    ext = jnp.concatenate([tail, x], axis=0).reshape(rows // SUBLANES + 1, SUBLANES, width)
    row_in_slab = lax.broadcasted_iota(jnp.int32, (rows // SUBLANES, SUBLANES, width), 1)
    acc = b + w_ref[taps - 1:taps, :] * x
    rot = ext
    for back in range(1, taps):
        rot = pltpu.roll(rot, 1, axis=1)
        shifted = jnp.where(row_in_slab < back, rot[:-1], rot[1:]).reshape(rows, width)
        acc = acc + w_ref[taps - 1 - back:taps - back, :] * shifted
    return acc


def _ada_kernel(ct_ref, w_ref, b_ref, o_ref, sb_ref):
    nb = sb_ref.shape[0]
    d = w_ref.shape[0]
    tn = o_ref.shape[-1]

    @pl.when(pl.program_id(0) == 0)
    def _():
        ct = ct_ref[...]
        st = _silu(ct)
        for b in range(nb):
            sb_ref[b] = jnp.broadcast_to(st[:, b:b + 1], (d, LANES))

    def body(kc, accs):
        r = pl.multiple_of(kc * SUBLANES, SUBLANES)
        w8 = w_ref[pl.ds(r, SUBLANES), :]
        out = []
        for b in range(nb):
            s8 = sb_ref[b, pl.ds(r, SUBLANES), :]
            out.append(accs[b] + w8 * jnp.tile(s8, (1, tn // LANES)))
        return tuple(out)

    accs = lax.fori_loop(0, d // SUBLANES, body,
                         tuple(jnp.zeros((SUBLANES, tn), F32) for _ in range(nb)), unroll=8)
    for b in range(nb):
        o_ref[b:b + 1, :] = jnp.sum(accs[b], axis=0, keepdims=True) + b_ref[...]


def _ada(c, w_ada, b_ada, tn=1024):
    nb, d = c.shape
    n = w_ada.shape[1]
    return pl.pallas_call(
        _ada_kernel,
        grid=(n // tn,),
        in_specs=[pl.BlockSpec((d, nb), lambda j: (0, 0)),
                  pl.BlockSpec((d, tn), lambda j: (0, j)),
                  pl.BlockSpec((1, tn), lambda j: (0, j))],
        out_specs=pl.BlockSpec((nb, tn), lambda j: (0, j)),
        out_shape=jax.ShapeDtypeStruct((nb, n), F32),
        scratch_shapes=[pltpu.VMEM((nb, d, LANES), F32)],
        compiler_params=_params(("arbitrary",)),
    )(c.T, w_ada, b_ada.reshape(1, n))


def _normproj_kernel(x_ref, g_ref, sc_ref, sh_ref, w_ref, wnext_ref, o_ref, wnext_bf_ref, *, sub):
    n = w_ref.shape[0]
    for r0 in range(0, x_ref.shape[0], sub):
        rows = slice(r0, r0 + sub)
        y = _rms(x_ref[rows, :]) * g_ref[...]
        h = y * (1.0 + sc_ref[0]) + sh_ref[0]
        o_ref[rows, 0:n] = _dot(h.astype(BF16), w_ref[...], NT)
        if n < o_ref.shape[1]:
            o_ref[rows, n:] = jnp.zeros((sub, o_ref.shape[1] - n), F32)
    wnext_bf_ref[...] = wnext_ref[...].astype(BF16)


def _resident(shape):
    return pl.BlockSpec(shape, lambda *_: (0,) * len(shape), pipeline_mode=pl.Buffered(1))


def _normproj(x2, g, scale, shift, w_t, n_out, w_next, seq, tm, sub):
    m, d = x2.shape
    per_b = seq // tm
    steps = m // tm
    assert w_next.shape[0] % (steps * 2 * SUBLANES) == 0
    ride = pl.BlockSpec((w_next.shape[0] // steps, w_next.shape[1]), lambda i: (i, 0))
    vec = pl.BlockSpec((1, 1, d), lambda i: (i // per_b, 0, 0))
    return pl.pallas_call(
        functools.partial(_normproj_kernel, sub=sub),
        grid=(m // tm,),
        in_specs=[pl.BlockSpec((tm, d), lambda i: (i, 0)),
                  pl.BlockSpec((1, d), lambda i: (0, 0)),
                  vec, vec, _resident(w_t.shape), ride],
        out_specs=[pl.BlockSpec((tm, n_out), lambda i: (i, 0)), ride],
        out_shape=[jax.ShapeDtypeStruct((m, n_out), F32), jax.ShapeDtypeStruct(w_next.shape, BF16)],
        compiler_params=_params(("parallel",)),
    )(x2, g.reshape(1, d), scale, shift, w_t, w_next)


def _rel_bucket(dist):
    n = jnp.maximum(dist, 0)
    max_exact = REL_BUCKETS // 2
    nf = jnp.maximum(n, max_exact).astype(F32)
    large = max_exact + (jnp.log(nf / max_exact) / math.log(REL_MAX_DIST / max_exact)
                         * (REL_BUCKETS - max_exact)).astype(jnp.int32)
    large = jnp.minimum(large, REL_BUCKETS - 1)
    return jnp.where(n < max_exact, n, large)


def _moba_units(nblk, grp):
    lead = grp - 1
    units = []
    for qi in range(nblk):
        units.append((qi, qi, 1))
        for t in range(-(-max(qi - lead, 0) // grp)):
            units.append((qi, lead + t * grp, 0))
    return units


def _moba_kernel(uq_ref, uk_ref, uf_ref, rb_ref, q_ref, k_ref, v_ref, o_ref,
                 qbf_ref, kbf_ref, vt_ref, tab_ref, mask_ref, s_buf, top_buf, p_buf, alpha_buf, m_ref, l_ref, acc_ref,
                 *, units):
    h = pl.program_id(0)
    seq = k_ref.shape[1]
    blk = MOBA_BLOCK
    grp = MOBA_FAR_GROUP
    lead = grp - 1
    nblk = seq // blk
    nunits = len(units)
    scale = HEAD_DIM ** -0.5
    exp2_scale = scale * math.log2(math.e)

    kf = k_ref[0]
    qf = q_ref[0]
    qbf_ref[...] = qf.astype(BF16)
    kbf_ref[0:lead * blk, :] = jnp.zeros((lead * blk, HEAD_DIM), BF16)
    kbf_ref[lead * blk:, :] = kf.astype(BF16)
    kmean = jnp.mean(kf.reshape(nblk, blk, HEAD_DIM), axis=1)
    for j in range(lead):
        vt_ref[j] = jnp.zeros((HEAD_DIM, blk), BF16)
    for j in range(nblk):
        vt_ref[lead + j] = v_ref[0, j * blk:(j + 1) * blk, :].T.astype(BF16)

    @pl.when(pl.program_id(1) == 0)
    def _():
        b_far = rb_ref[REL_BUCKETS - 1, h]
        kk = lax.broadcasted_iota(jnp.int32, (blk, blk), 0)
        qq = lax.broadcasted_iota(jnp.int32, (blk, blk), 1)
        tab_ref[0] = jnp.zeros((2 * blk, blk), F32)
        for pos, dist in ((0, qq - kk + blk), (1, qq - kk)):
            bucket = _rel_bucket(dist)
            tab = jnp.zeros((blk, blk), F32)
            for b in range(REL_BUCKETS):
                tab = jnp.where(bucket == b, (rb_ref[b, h] - b_far) / scale, tab)
            tab_ref[1, pos * blk:(pos + 1) * blk, :] = jnp.where(dist >= 0, tab, NEG)

    q_hi, q_mid, _ = _split3(qf)
    k_hi, k_mid, _ = _split3(kmean)
    gate = _dot(k_hi, q_hi, NT) + _dot(k_hi, q_mid, NT) + _dot(k_mid, q_hi, NT)
    nidx = lax.broadcasted_iota(jnp.int32, (nblk, seq), 0)
    qblk = lax.broadcasted_iota(jnp.int32, (nblk, seq), 1) // blk
    nidx_f = nidx.astype(F32)
    avail = jnp.where(nidx < qblk, 1.0, 0.0)
    chosen = jnp.zeros((nblk, seq), F32)
    for _ in range(MOBA_TOPK):
        gm = jnp.where(avail > 0.0, gate, -jnp.inf)
        best = jnp.max(gm, axis=0, keepdims=True)
        first = jnp.min(jnp.where((gm == best) & (avail > 0.0), nidx_f, float(nblk)), axis=0, keepdims=True)
        hit = nidx_f == first
        chosen = jnp.where(hit, 1.0, chosen)
        avail = jnp.where(hit, 0.0, avail)
    add_all = jnp.where(chosen > 0.0, 0.0, NEG)
    for u, (qi, ks, is_first) in enumerate(units):
        cols = slice(qi * blk, (qi + 1) * blk)
        for n in range(grp):
            j = ks + n - lead
            if j == qi and is_first:
                row = jnp.zeros((1, blk), F32)
            elif 0 <= j < (qi if is_first else qi - lead):
                row = add_all[j:j + 1, cols]
            else:
                row = jnp.full((1, blk), NEG, F32)
            mask_ref[u * grp + n] = jnp.broadcast_to(row, (SUBLANES, blk))

    def unit(u):
        uc = jnp.clip(u, 0, nunits - 1)
        return uq_ref[uc], uk_ref[uc], uf_ref[uc], uc

    def scores(slot, u):
        qi, ks, is_first, uc = unit(u)
        qb = qbf_ref[pl.ds(pl.multiple_of(qi * blk, blk), blk), :]
        kg = kbf_ref[pl.ds(pl.multiple_of(ks * blk, blk), grp * blk), :]
        s = _dot(kg, qb, NT)
        tab = tab_ref[is_first]
        parts, top = [], None
        for n in range(grp):
            part = s[n * blk:(n + 1) * blk]
            if n >= grp - 2:
                part = part + tab[(n - grp + 2) * blk:(n - grp + 3) * blk]
            parts.append(part)
            best = jnp.max(part, axis=0, keepdims=True) + mask_ref[uc * grp + n][0:1, :]
            top = best if top is None else jnp.maximum(top, best)
        s_buf[slot] = jnp.concatenate(parts, axis=0)
        top_buf[slot] = top

    def softmax(slot, u):
        qi, _, is_first, uc = unit(u)
        st = qi % 2
        m_old = jnp.where(is_first == 1, NEG, m_ref[st])
        l_old = jnp.where(is_first == 1, 0.0, l_ref[st])
        m_new = jnp.maximum(m_old, top_buf[slot])
        alpha = jnp.exp2((m_old - m_new) * exp2_scale)
        total = alpha * l_old
        for n in range(grp):
            offset = (m_new - mask_ref[uc * grp + n][0:1, :]) * exp2_scale
            p = jnp.exp2(s_buf[slot, n * blk:(n + 1) * blk, :] * exp2_scale - offset)
            total = total + jnp.sum(p, axis=0, keepdims=True)
            p_buf[slot, n * blk:(n + 1) * blk, :] = p.astype(BF16)
        m_ref[st] = m_new
        l_ref[st] = total
        alpha_buf[slot] = alpha

    def values(slot, u):
        qi, ks, _, _ = unit(u)
        st = qi % 2
        acc = alpha_buf[slot] * acc_ref[st]
        for n in range(grp):
            acc = acc + _dot(vt_ref[ks + n], p_buf[slot, n * blk:(n + 1) * blk, :])
        acc_ref[st] = acc
        o_ref[0, pl.ds(pl.multiple_of(qi * blk, blk), blk), :] = (acc * (1.0 / l_ref[st])).T

    m_ref[...] = jnp.full(m_ref.shape, NEG, F32)
    l_ref[...] = jnp.ones(l_ref.shape, F32)
    acc_ref[...] = jnp.zeros(acc_ref.shape, F32)
    alpha_buf[1] = jnp.ones((1, blk), F32)
    p_buf[1] = jnp.zeros((grp * blk, blk), BF16)
    scores(0, 0)

    def trip(u, slot):
        values(1 - slot, u - 1)
        softmax(slot, u)
        scores(1 - slot, u + 1)

    def trips(it, carry):
        for n in range(MOBA_UNROLL):
            trip(MOBA_UNROLL * it + n, n % 2)
        return carry

    lax.fori_loop(0, nunits // MOBA_UNROLL, trips, 0)
    for u in range(nunits - nunits % MOBA_UNROLL, nunits):
        trip(u, u % 2)
    values((nunits - 1) % 2, nunits - 1)


def _moba(rel_bias, proj3):
    bsz, seq, _ = proj3.shape
    blk = MOBA_BLOCK
    grp = MOBA_FAR_GROUP
    nblk = seq // blk
    assert seq % blk == 0 and grp >= 2
    units = _moba_units(nblk, grp)
    uq, uk, uf = (jnp.asarray([u[c] for u in units], jnp.int32) for c in range(3))
    smem = pl.BlockSpec(memory_space=pltpu.SMEM)
    head = lambda col0: pl.BlockSpec((1, seq, HEAD_DIM), lambda h, b: (b, 0, col0 + h))
    return pl.pallas_call(
        functools.partial(_moba_kernel, units=units),
        grid=(N_HEADS, bsz),
        in_specs=[smem, smem, smem, smem, head(0), head(N_HEADS), head(2 * N_HEADS)],
        out_specs=head(0),
        out_shape=jax.ShapeDtypeStruct((bsz, seq, ATTN_WIDTH), F32),
        scratch_shapes=[pltpu.VMEM((seq, HEAD_DIM), BF16),
                        pltpu.VMEM((seq + (grp - 1) * blk, HEAD_DIM), BF16),
                        pltpu.VMEM((nblk + grp - 1, HEAD_DIM, blk), BF16),
                        pltpu.VMEM((2, 2 * blk, blk), F32),
                        pltpu.VMEM((len(units) * grp, SUBLANES, blk), F32),
                        pltpu.VMEM((2, grp * blk, blk), F32),
                        pltpu.VMEM((2, 1, blk), F32),
                        pltpu.VMEM((2, grp * blk, blk), BF16),
                        pltpu.VMEM((2, 1, blk), F32),
                        pltpu.VMEM((2, 1, blk), F32),
                        pltpu.VMEM((2, 1, blk), F32),
                        pltpu.VMEM((2, HEAD_DIM, blk), F32)],
        compiler_params=_params(("arbitrary", "arbitrary")),
    )(uq, uk, uf, rel_bias, proj3, proj3, proj3)


def _ssd_kernel(xs_ref, z_ref, bc_ref, dt_ref, cwx_ref, cwbc_ref, cbx_ref, cbbc_ref,
                dtb_ref, alog_ref, dsk_ref, ng_ref, tri_ref, trit_ref, exp_ref, o_ref,
                xtail_ref, bctail_ref, state_ref):
    c = pl.program_id(1)
    t = CHUNK
    pad = SUBLANES

    @pl.when(c == 0)
    def _():
        xtail_ref[...] = jnp.zeros_like(xtail_ref)
        bctail_ref[...] = jnp.zeros_like(bctail_ref)
        state_ref[...] = jnp.zeros_like(state_ref)

    def conv_silu(tail_ref, src_ref, w_ref, b_ref):
        x = src_ref[0]
        acc = _causal_conv(tail_ref[...], x, w_ref, b_ref[...])
        tail_ref[...] = x[t - pad:t, :]
        return _silu(acc)

    xh = conv_silu(xtail_ref, xs_ref, cwx_ref, cbx_ref)
    bc = conv_silu(bctail_ref, bc_ref, cwbc_ref, cbbc_ref)

    dtr = dt_ref[0] + dtb_ref[...]
    dt = jnp.maximum(dtr, 0.0) + jnp.log1p(jnp.exp(-jnp.abs(dtr)))
    adt = dt * (-jnp.exp(alog_ref[...]))
    acs = _dot_exact_lhs(tri_ref[...], adt)
    acs_t = _dot_exact_rhs(adt.T, trit_ref[...])

    stack = jnp.concatenate([dt, jnp.exp(acs), jnp.exp(acs[t - 1:t, :] - acs)], axis=0)
    wide = _dot_exact_rhs(stack, exp_ref[...])
    dt_x, eacs_x, dst_x = wide[0:t], wide[t:2 * t], wide[2 * t:3 * t]

    xdt = xh * dt_x
    xdt_bf = xdt.astype(BF16)
    xdec_bf = (xdt * dst_x).astype(BF16)

    row = lax.broadcasted_iota(jnp.int32, (t, t), 0)
    col = lax.broadcasted_iota(jnp.int32, (t, t), 1)
    tril = row >= col
    lane = lax.broadcasted_iota(jnp.int32, (t, LANES), 1)
    heads_per_group = N_SSM_HEADS // N_GROUPS
    nt = (((1,), (1,)), ((), ()))
    tn = (((0,), (0,)), ((), ()))

    y_parts = []
    for g in range(N_GROUPS):
        bg = bc[:, g * D_STATE:(g + 1) * D_STATE].astype(BF16)
        cg = bc[:, (N_GROUPS + g) * D_STATE:(N_GROUPS + g + 1) * D_STATE].astype(BF16)
        cb = _dot(cg, bg, nt)
        for pair in range(heads_per_group // 2):
            ms = []
            for r in (g * heads_per_group + 2 * pair, g * heads_per_group + 2 * pair + 1):
                seg = acs[:, r:r + 1] - acs_t[r:r + 1, :]
                ms.append((cb * jnp.exp(jnp.where(tril, seg, NEG))).astype(BF16))
            q = g * (heads_per_group // 2) + pair
            y2 = _dot(jnp.concatenate(ms, axis=0), xdt_bf[:, q * LANES:(q + 1) * LANES])
            y_parts.append(jnp.where(lane < SSM_HEAD_DIM, y2[0:t], y2[t:2 * t]))
    y = jnp.concatenate(y_parts, axis=1)

    off_parts = []
    for g in range(N_GROUPS):
        sl = slice(g * GROUP_WIDTH, (g + 1) * GROUP_WIDTH)
        bg = bc[:, g * D_STATE:(g + 1) * D_STATE].astype(BF16)
        cg = bc[:, (N_GROUPS + g) * D_STATE:(N_GROUPS + g + 1) * D_STATE].astype(BF16)
        st = state_ref[:, sl]
        off_parts.append(_dot(cg, st.astype(BF16)))
        state_ref[:, sl] = eacs_x[t - 1:t, sl] * st + _dot(bg, xdec_bf[:, sl], tn)
    y = y + jnp.concatenate(off_parts, axis=1) * eacs_x + dsk_ref[...] * xh
    y = y * _silu(z_ref[0])

    outs = []
    for g in range(N_GROUPS):
        sl = slice(g * GROUP_WIDTH, (g + 1) * GROUP_WIDTH)
        outs.append(_rms(y[:, sl]) * ng_ref[:, sl])
    o_ref[0] = jnp.concatenate(outs, axis=1).astype(o_ref.dtype)


def _ssd(proj3, conv_w, conv_b, dt_bias, a_log, d_skip, norm_g):
    bsz, seq, _ = proj3.shape
    t = CHUNK
    gn2 = 2 * N_GROUPS * D_STATE
    pad_h = LANES - N_SSM_HEADS
    tri = jnp.tril(jnp.ones((t, t), F32)).astype(BF16)
    expand = jnp.pad(jnp.repeat(jnp.eye(N_SSM_HEADS, dtype=F32), SSM_HEAD_DIM, axis=1),
                     ((0, pad_h), (0, 0))).astype(BF16)
    const = lambda shape: pl.BlockSpec(shape, lambda b, c: (0,) * len(shape))
    return pl.pallas_call(
        _ssd_kernel,
        grid=(bsz, seq // t),
        in_specs=[pl.BlockSpec((1, t, SSM_WIDTH), lambda b, c: (b, c, COL_XS // SSM_WIDTH)),
                  pl.BlockSpec((1, t, SSM_WIDTH), lambda b, c: (b, c, COL_Z // SSM_WIDTH)),
                  pl.BlockSpec((1, t, gn2), lambda b, c: (b, c, COL_BC // gn2)),
                  pl.BlockSpec((1, t, LANES), lambda b, c: (b, c, COL_DT // LANES)),
                  const((SSM_CONV, SSM_WIDTH)), const((SSM_CONV, gn2)),
                  const((1, SSM_WIDTH)), const((1, gn2)),
                  const((1, LANES)), const((1, LANES)),
                  const((1, SSM_WIDTH)), const((1, SSM_WIDTH)),
                  const((t, t)), const((t, t)), const((LANES, SSM_WIDTH))],
        out_specs=pl.BlockSpec((1, t, SSM_WIDTH), lambda b, c: (b, c, 0)),
        out_shape=jax.ShapeDtypeStruct((bsz, seq, SSM_WIDTH), BF16),
        scratch_shapes=[pltpu.VMEM((SUBLANES, SSM_WIDTH), F32),
                        pltpu.VMEM((SUBLANES, gn2), F32),
                        pltpu.VMEM((D_STATE, SSM_WIDTH), F32)],
        compiler_params=_params(("arbitrary", "arbitrary")),
    )(proj3, proj3, proj3, proj3,
      conv_w[:, :SSM_WIDTH], conv_w[:, SSM_WIDTH:],
      conv_b[:SSM_WIDTH].reshape(1, -1), conv_b[SSM_WIDTH:].reshape(1, -1),
      jnp.pad(dt_bias, (0, pad_h)).reshape(1, LANES), jnp.pad(a_log, (0, pad_h)).reshape(1, LANES),
      jnp.repeat(d_skip, SSM_HEAD_DIM).reshape(1, SSM_WIDTH), norm_g.reshape(1, SSM_WIDTH),
      tri, tri.T, expand)


def _outproj_kernel(attn_ref, ssm_ref, x_ref, ag_ref, w_ref, gm_ref, g2_ref, sc_ref, sh_ref,
                    x1_ref, h2_ref, *, sub):
    for r0 in range(0, x_ref.shape[0], sub):
        rows = slice(r0, r0 + sub)
        a = _rms(attn_ref[rows, :]) * ag_ref[...]
        lhs = jnp.concatenate([a.astype(BF16), ssm_ref[rows, :]], axis=-1)
        x1 = x_ref[rows, :] + gm_ref[0] * _dot(lhs, w_ref[...])
        x1_ref[rows, :] = x1
        y = _rms(x1) * g2_ref[...]
        h2_ref[rows, :] = (y * (1.0 + sc_ref[0]) + sh_ref[0]).astype(BF16)


def _outproj(attn2, ssm2, x2, attn_g, w_out_bf, gate_m, g2, scale_f, shift_f, seq, tm, sub):
    m, d = x2.shape
    per_b = seq // tm
    vec = pl.BlockSpec((1, 1, d), lambda i: (i // per_b, 0, 0))
    half = pl.BlockSpec((tm, ATTN_WIDTH), lambda i: (i, 0))
    full = pl.BlockSpec((tm, d), lambda i: (i, 0))
    return pl.pallas_call(
        functools.partial(_outproj_kernel, sub=sub),
        grid=(m // tm,),
        in_specs=[half, half, full,
                  pl.BlockSpec((1, ATTN_WIDTH), lambda i: (0, 0)),
                  _resident((d, d)),
                  vec, pl.BlockSpec((1, d), lambda i: (0, 0)), vec, vec],
        out_specs=[full, full],
        out_shape=[jax.ShapeDtypeStruct((m, d), F32), jax.ShapeDtypeStruct((m, d), BF16)],
        compiler_params=_params(("parallel",)),
    )(attn2, ssm2, x2, attn_g.reshape(1, -1), w_out_bf, gate_m, g2.reshape(1, d), scale_f, shift_f)


def _ffn_up_kernel(h_ref, wg_ref, wv_ref, cw_ref, cb_ref, wnext_ref, o_ref, wnext_bf_ref, wb_ref, tail_ref,
                   *, tm, sub, seq):
    i = pl.program_id(1)
    pad = SUBLANES
    tn = o_ref.shape[1]

    @pl.when(i == 0)
    def _():
        wb_ref[:, 0:tn] = wg_ref[...].astype(BF16)
        wb_ref[:, tn:2 * tn] = wv_ref[...].astype(BF16)

    @pl.when((i * tm) % seq == 0)
    def _():
        tail_ref[...] = jnp.zeros((pad, 2 * tn), F32)

    tail = tail_ref[...]
    for r0 in range(0, tm, sub):
        u = _dot(h_ref[r0:r0 + sub, :], wb_ref[...])
        acc = _causal_conv(tail, u, cw_ref, cb_ref[...])
        tail = u[sub - pad:sub, :]
        o_ref[r0:r0 + sub, :] = (_silu(acc[:, 0:tn]) * acc[:, tn:2 * tn]).astype(o_ref.dtype)
    tail_ref[...] = tail
    wnext_bf_ref[...] = wnext_ref[...].astype(BF16)


def _ffn_up(h2, w_up, conv_w, conv_b, w_next, seq, tm, sub, tn):
    m, d = h2.shape
    nj = FFN_DIM // tn
    ni = m // tm
    ride = pl.BlockSpec((w_next.shape[0] // (nj * ni), w_next.shape[1]), lambda j, i: (j * ni + i, 0))
    assert w_next.shape[0] % (nj * ni * 2 * SUBLANES) == 0
    pair = lambda a: jnp.concatenate([a[:, :FFN_DIM].reshape(-1, nj, tn), a[:, FFN_DIM:].reshape(-1, nj, tn)],
                                     axis=2).reshape(-1, 2 * FFN_DIM)
    return pl.pallas_call(
        functools.partial(_ffn_up_kernel, tm=tm, sub=sub, seq=seq),
        grid=(nj, ni),
        in_specs=[pl.BlockSpec((tm, d), lambda j, i: (i, 0)),
                  pl.BlockSpec((d, tn), lambda j, i: (0, j)),
                  pl.BlockSpec((d, tn), lambda j, i: (0, j + nj)),
                  pl.BlockSpec((FFN_CONV, 2 * tn), lambda j, i: (0, j)),
                  pl.BlockSpec((1, 2 * tn), lambda j, i: (0, j)),
                  ride],
        out_specs=[pl.BlockSpec((tm, tn), lambda j, i: (i, j)), ride],
        out_shape=[jax.ShapeDtypeStruct((m, FFN_DIM), BF16), jax.ShapeDtypeStruct(w_next.shape, BF16)],
        scratch_shapes=[pltpu.VMEM((d, 2 * tn), BF16), pltpu.VMEM((SUBLANES, 2 * tn), F32)],
        compiler_params=_params(("arbitrary", "arbitrary")),
    )(h2, w_up, w_up, pair(conv_w), pair(conv_b.reshape(1, -1)), w_next)


def _ffn_down_kernel(a_ref, w_ref, x1_ref, gf_ref, fg_ref, o_ref):
    x2 = x1_ref[...] + gf_ref[0] * _dot(a_ref[...], w_ref[...])
    o_ref[...] = _rms(x2) * fg_ref[...]


def _ffn_down(act, w_down_bf, x1, gate_f, final_g, seq, tm):
    m, d = x1.shape
    per_b = seq // tm
    row = lambda width: pl.BlockSpec((tm, width), lambda i: (i, 0))
    return pl.pallas_call(
        _ffn_down_kernel,
        grid=(m // tm,),
        in_specs=[row(FFN_DIM), _resident((FFN_DIM, d)), row(d),
                  pl.BlockSpec((1, 1, d), lambda i: (i // per_b, 0, 0)),
                  pl.BlockSpec((1, d), lambda i: (0, 0))],
        out_specs=row(d),
        out_shape=jax.ShapeDtypeStruct((m, d), F32),
        compiler_params=_params(("parallel",)),
    )(act, w_down_bf, x1, gate_f, final_g.reshape(1, d))


def kernel(x, c, w_ada, b_ada, norm_mix_g, w_in, rel_bias, attn_norm_g, conv_ssm_w, conv_ssm_b, dt_bias,
           a_log, d_skip, ssm_norm_g, w_out, norm_ffn_g, w_up, conv_ffn_w, conv_ffn_b, w_down, final_norm_g):
    bsz, seq, d = x.shape
    m = bsz * seq
    assert w_ada.shape[0] == 1, "the final RMSNorm is fused into the (single) layer's ffn_down kernel"
    l = 0
    x2 = x.reshape(m, d)
    mod = _ada(c, w_ada[l], b_ada[l])
    shift_m, scale_m, gate_m, shift_f, scale_f, gate_f = [
        mod[:, k * d:(k + 1) * d].reshape(bsz, 1, d) for k in range(6)]

    w_in_t = w_in[l].T
    proj, w_out_bf = _normproj(x2, norm_mix_g[l], scale_m, shift_m, w_in_t.astype(BF16), PROJ_COLS, w_out[l],
                               seq, 256, 256)
    proj3 = proj.reshape(bsz, seq, PROJ_COLS)

    attn = _moba(rel_bias, proj3)
    ssm = _ssd(proj3, conv_ssm_w[l], conv_ssm_b[l], dt_bias[l], a_log[l], d_skip[l], ssm_norm_g[l])

    x1, h2 = _outproj(attn.reshape(m, ATTN_WIDTH), ssm.reshape(m, SSM_WIDTH), x2, attn_norm_g[l],
                      w_out_bf, gate_m, norm_ffn_g[l], scale_f, shift_f, seq, 512, 256)
    act, w_down_bf = _ffn_up(h2, w_up[l], conv_ffn_w[l], conv_ffn_b[l], w_down[l], seq, 1024, 256, 512)
    out = _ffn_down(act, w_down_bf, x1, gate_f, final_norm_g, seq, 256)
    return out.reshape(bsz, seq, d)
```

```python
import functools
import math

import jax
import jax.numpy as jnp
from jax import lax
from jax.experimental import pallas as pl
from jax.experimental.pallas import tpu as pltpu

F32 = jnp.float32
BF16 = jnp.bfloat16

D_MODEL = 2048
ATTN_WIDTH = 1024
HEAD_DIM = 128
N_HEADS = 8
SSM_WIDTH = 1024
SSM_HEAD_DIM = 64
N_SSM_HEADS = 16
N_GROUPS = 2
GROUP_WIDTH = SSM_WIDTH // N_GROUPS
D_STATE = 128
SSM_CONV = 4
CHUNK = 256
MOBA_BLOCK = 256
MOBA_TOPK = 3
MOBA_FAR_GROUP = 4
MOBA_UNROLL = 2
REL_BUCKETS = 32
REL_MAX_DIST = 128
FFN_DIM = 5632
FFN_CONV = 3
EPS = 1e-6
NEG = -1e30

LANES = 128
SUBLANES = 8
VMEM_LIMIT = 56 * 1024 * 1024

PROJ_COLS = 5760
COL_XS = 3 * ATTN_WIDTH
COL_Z = COL_XS + SSM_WIDTH
COL_BC = COL_Z + SSM_WIDTH
COL_DT = COL_BC + 2 * N_GROUPS * D_STATE


def _params(sem):
    return pltpu.CompilerParams(dimension_semantics=sem, vmem_limit_bytes=VMEM_LIMIT)


def _split3(x):
    hi = x.astype(BF16)
    r = x - hi.astype(F32)
    mid = r.astype(BF16)
    lo = (r - mid.astype(F32)).astype(BF16)
    return hi, mid, lo


NT = (((1,), (1,)), ((), ()))


def _dot(a, b, dims=(((1,), (0,)), ((), ()))):
    return lax.dot_general(a, b, dims, preferred_element_type=F32)


def _dot_exact_lhs(a_bf, x, dims=(((1,), (0,)), ((), ()))):
    hi, mid, lo = _split3(x)
    return _dot(a_bf, hi, dims) + _dot(a_bf, mid, dims) + _dot(a_bf, lo, dims)


def _dot_exact_rhs(x, b_bf, dims=(((1,), (0,)), ((), ()))):
    hi, mid, lo = _split3(x)
    return _dot(hi, b_bf, dims) + _dot(mid, b_bf, dims) + _dot(lo, b_bf, dims)


def _silu(x):
    return x * jax.nn.sigmoid(x)


def _rms(x):
    return x * lax.rsqrt(jnp.mean(x * x, axis=-1, keepdims=True) + EPS)


def _causal_conv(tail, x, w_ref, b):
    rows, width = x.shape
    taps = w_ref.shape[0]
    ext = jnp.concatenate([tail, x], axis=0).reshape(rows // SUBLANES + 1, SUBLANES, width)
    row_in_slab = lax.broadcasted_iota(jnp.int32, (rows // SUBLANES, SUBLANES, width), 1)
    acc = b + w_ref[taps - 1:taps, :] * x
    rot = ext
    for back in range(1, taps):
        rot = pltpu.roll(rot, 1, axis=1)
        shifted = jnp.where(row_in_slab < back, rot[:-1], rot[1:]).reshape(rows, width)
        acc = acc + w_ref[taps - 1 - back:taps - back, :] * shifted
    return acc


def _ada_kernel(ct_ref, w_ref, b_ref, o_ref, sb_ref):
    nb = sb_ref.shape[0]
    d = w_ref.shape[0]
    tn = o_ref.shape[-1]

    @pl.when(pl.program_id(0) == 0)
    def _():
        ct = ct_ref[...]
        st = _silu(ct)
        for b in range(nb):
            sb_ref[b] = jnp.broadcast_to(st[:, b:b + 1], (d, LANES))

    def body(kc, accs):
        r = pl.multiple_of(kc * SUBLANES, SUBLANES)
        w8 = w_ref[pl.ds(r, SUBLANES), :]
        out = []
        for b in range(nb):
            s8 = sb_ref[b, pl.ds(r, SUBLANES), :]
            out.append(accs[b] + w8 * jnp.tile(s8, (1, tn // LANES)))
        return tuple(out)

    accs = lax.fori_loop(0, d // SUBLANES, body,
                         tuple(jnp.zeros((SUBLANES, tn), F32) for _ in range(nb)), unroll=8)
    for b in range(nb):
        o_ref[b:b + 1, :] = jnp.sum(accs[b], axis=0, keepdims=True) + b_ref[...]


def _ada(c, w_ada, b_ada, tn=1024):
    nb, d = c.shape
    n = w_ada.shape[1]
    return pl.pallas_call(
        _ada_kernel,
        grid=(n // tn,),
        in_specs=[pl.BlockSpec((d, nb), lambda j: (0, 0)),
                  pl.BlockSpec((d, tn), lambda j: (0, j)),
                  pl.BlockSpec((1, tn), lambda j: (0, j))],
        out_specs=pl.BlockSpec((nb, tn), lambda j: (0, j)),
        out_shape=jax.ShapeDtypeStruct((nb, n), F32),
        scratch_shapes=[pltpu.VMEM((nb, d, LANES), F32)],
        compiler_params=_params(("arbitrary",)),
    )(c.T, w_ada, b_ada.reshape(1, n))


def _normproj_kernel(x_ref, g_ref, sc_ref, sh_ref, w_ref, wnext_ref, o_ref, wnext_bf_ref, *, sub):
    n = w_ref.shape[0]
    for r0 in range(0, x_ref.shape[0], sub):
        rows = slice(r0, r0 + sub)
        y = _rms(x_ref[rows, :]) * g_ref[...]
        h = y * (1.0 + sc_ref[0]) + sh_ref[0]
        o_ref[rows, 0:n] = _dot(h.astype(BF16), w_ref[...], NT)
        if n < o_ref.shape[1]:
            o_ref[rows, n:] = jnp.zeros((sub, o_ref.shape[1] - n), F32)
    wnext_bf_ref[...] = wnext_ref[...].astype(BF16)


def _resident(shape):
    return pl.BlockSpec(shape, lambda *_: (0,) * len(shape), pipeline_mode=pl.Buffered(1))


def _normproj(x2, g, scale, shift, w_t, n_out, w_next, seq, tm, sub):
    m, d = x2.shape
    per_b = seq // tm
    steps = m // tm
    assert w_next.shape[0] % (steps * 2 * SUBLANES) == 0
    ride = pl.BlockSpec((w_next.shape[0] // steps, w_next.shape[1]), lambda i: (i, 0))
    vec = pl.BlockSpec((1, 1, d), lambda i: (i // per_b, 0, 0))
    return pl.pallas_call(
        functools.partial(_normproj_kernel, sub=sub),
        grid=(m // tm,),
        in_specs=[pl.BlockSpec((tm, d), lambda i: (i, 0)),
                  pl.BlockSpec((1, d), lambda i: (0, 0)),
                  vec, vec, _resident(w_t.shape), ride],
        out_specs=[pl.BlockSpec((tm, n_out), lambda i: (i, 0)), ride],
        out_shape=[jax.ShapeDtypeStruct((m, n_out), F32), jax.ShapeDtypeStruct(w_next.shape, BF16)],
        compiler_params=_params(("parallel",)),
    )(x2, g.reshape(1, d), scale, shift, w_t, w_next)


def _rel_bucket(dist):
    n = jnp.maximum(dist, 0)
    max_exact = REL_BUCKETS // 2
    nf = jnp.maximum(n, max_exact).astype(F32)
    large = max_exact + (jnp.log(nf / max_exact) / math.log(REL_MAX_DIST / max_exact)
                         * (REL_BUCKETS - max_exact)).astype(jnp.int32)
    large = jnp.minimum(large, REL_BUCKETS - 1)
    return jnp.where(n < max_exact, n, large)


def _moba_units(nblk, grp):
    lead = grp - 1
    units = []
    for qi in range(nblk):
        units.append((qi, qi, 1))
        for t in range(-(-max(qi - lead, 0) // grp)):
            units.append((qi, lead + t * grp, 0))
    return units


def _moba_kernel(uq_ref, uk_ref, uf_ref, rb_ref, q_ref, k_ref, v_ref, o_ref,
                 qbf_ref, kbf_ref, vt_ref, tab_ref, mask_ref, s_buf, top_buf, p_buf, alpha_buf, m_ref, l_ref, acc_ref,
                 *, units):
    h = pl.program_id(0)
    seq = k_ref.shape[1]
    blk = MOBA_BLOCK
    grp = MOBA_FAR_GROUP
    lead = grp - 1
    nblk = seq // blk
    nunits = len(units)
    scale = HEAD_DIM ** -0.5
    exp2_scale = scale * math.log2(math.e)

    kf = k_ref[0]
    qf = q_ref[0]
    qbf_ref[...] = (qf * exp2_scale).astype(BF16)
    kbf_ref[0:lead * blk, :] = jnp.zeros((lead * blk, HEAD_DIM), BF16)
    kbf_ref[lead * blk:, :] = kf.astype(BF16)
    kmean = jnp.mean(kf.reshape(nblk, blk, HEAD_DIM), axis=1)
    for j in range(lead):
        vt_ref[j] = jnp.zeros((HEAD_DIM, blk), BF16)
    for j in range(nblk):
        vt_ref[lead + j] = v_ref[0, j * blk:(j + 1) * blk, :].T.astype(BF16)

    @pl.when(pl.program_id(1) == 0)
    def _():
        b_far = rb_ref[REL_BUCKETS - 1, h]
        kk = lax.broadcasted_iota(jnp.int32, (blk, blk), 0)
        qq = lax.broadcasted_iota(jnp.int32, (blk, blk), 1)
        tab_ref[0] = jnp.zeros((2 * blk, blk), F32)
        for pos, dist in ((0, qq - kk + blk), (1, qq - kk)):
            bucket = _rel_bucket(dist)
            tab = jnp.zeros((blk, blk), F32)
            for b in range(REL_BUCKETS):
                tab = jnp.where(bucket == b, (rb_ref[b, h] - b_far) * math.log2(math.e), tab)
            tab_ref[1, pos * blk:(pos + 1) * blk, :] = jnp.where(dist >= 0, tab, NEG)

    q_hi, q_mid, _ = _split3(qf)
    k_hi, k_mid, _ = _split3(kmean)
    gate = _dot(k_hi, q_hi, NT) + _dot(k_hi, q_mid, NT) + _dot(k_mid, q_hi, NT)
    nidx = lax.broadcasted_iota(jnp.int32, (nblk, seq), 0)
    qblk = lax.broadcasted_iota(jnp.int32, (nblk, seq), 1) // blk
    nidx_f = nidx.astype(F32)
    avail = jnp.where(nidx < qblk, 1.0, 0.0)
    chosen = jnp.zeros((nblk, seq), F32)
    for _ in range(MOBA_TOPK):
        gm = jnp.where(avail > 0.0, gate, -jnp.inf)
        best = jnp.max(gm, axis=0, keepdims=True)
        first = jnp.min(jnp.where((gm == best) & (avail > 0.0), nidx_f, float(nblk)), axis=0, keepdims=True)
        hit = nidx_f == first
        chosen = jnp.where(hit, 1.0, chosen)
        avail = jnp.where(hit, 0.0, avail)
    add_all = jnp.where(chosen > 0.0, 0.0, NEG)
    for u, (qi, ks, is_first) in enumerate(units):
        cols = slice(qi * blk, (qi + 1) * blk)
        for n in range(grp):
            j = ks + n - lead
            if j == qi and is_first:
                row = jnp.zeros((1, blk), F32)
            elif 0 <= j < (qi if is_first else qi - lead):
                row = add_all[j:j + 1, cols]
            else:
                row = jnp.full((1, blk), NEG, F32)
            mask_ref[u * grp + n] = jnp.broadcast_to(row, (SUBLANES, blk))

    def unit(u):
        uc = jnp.clip(u, 0, nunits - 1)
        return uq_ref[uc], uk_ref[uc], uf_ref[uc], uc

    def scores(slot, u):
        qi, ks, is_first, uc = unit(u)
        qb = qbf_ref[pl.ds(pl.multiple_of(qi * blk, blk), blk), :]
        kg = kbf_ref[pl.ds(pl.multiple_of(ks * blk, blk), grp * blk), :]
        s = _dot(kg, qb, NT)
        tab = tab_ref[is_first]
        parts, top = [], None
        for n in range(grp):
            part = s[n * blk:(n + 1) * blk]
            if n >= grp - 2:
                part = part + tab[(n - grp + 2) * blk:(n - grp + 3) * blk]
            parts.append(part)
            best = jnp.max(part, axis=0, keepdims=True) + mask_ref[uc * grp + n][0:1, :]
            top = best if top is None else jnp.maximum(top, best)
        s_buf[slot] = jnp.concatenate(parts, axis=0)
        top_buf[slot] = top

    def softmax(slot, u):
        qi, _, is_first, uc = unit(u)
        st = qi % 2
        m_old = jnp.where(is_first == 1, NEG, m_ref[st])
        l_old = jnp.where(is_first == 1, 0.0, l_ref[st])
        m_new = jnp.maximum(m_old, top_buf[slot])
        alpha = jnp.exp2(m_old - m_new)
        total = alpha * l_old
        for n in range(grp):
            offset = m_new - mask_ref[uc * grp + n][0:1, :]
            p = jnp.exp2(s_buf[slot, n * blk:(n + 1) * blk, :] - offset)
            total = total + jnp.sum(p, axis=0, keepdims=True)
            p_buf[slot, n * blk:(n + 1) * blk, :] = p.astype(BF16)
        m_ref[st] = m_new
        l_ref[st] = total
        alpha_buf[slot] = alpha

    def values(slot, u):
        qi, ks, _, _ = unit(u)
        st = qi % 2
        acc = alpha_buf[slot] * acc_ref[st]
        for n in range(grp):
            acc = acc + _dot(vt_ref[ks + n], p_buf[slot, n * blk:(n + 1) * blk, :])
        acc_ref[st] = acc
        o_ref[0, pl.ds(pl.multiple_of(qi * blk, blk), blk), :] = (acc * (1.0 / l_ref[st])).T

    m_ref[...] = jnp.full(m_ref.shape, NEG, F32)
    l_ref[...] = jnp.ones(l_ref.shape, F32)
    acc_ref[...] = jnp.zeros(acc_ref.shape, F32)
    alpha_buf[1] = jnp.ones((1, blk), F32)
    p_buf[1] = jnp.zeros((grp * blk, blk), BF16)
    scores(0, 0)

    def trip(u, slot):
        values(1 - slot, u - 1)
        softmax(slot, u)
        scores(1 - slot, u + 1)

    def trips(it, carry):
        for n in range(MOBA_UNROLL):
            trip(MOBA_UNROLL * it + n, n % 2)
        return carry

    lax.fori_loop(0, nunits // MOBA_UNROLL, trips, 0)
    for u in range(nunits - nunits % MOBA_UNROLL, nunits):
        trip(u, u % 2)
    values((nunits - 1) % 2, nunits - 1)


def _moba(rel_bias, proj3):
    bsz, seq, _ = proj3.shape
    blk = MOBA_BLOCK
    grp = MOBA_FAR_GROUP
    nblk = seq // blk
    assert seq % blk == 0 and grp >= 2
    units = _moba_units(nblk, grp)
    uq, uk, uf = (jnp.asarray([u[c] for u in units], jnp.int32) for c in range(3))
    smem = pl.BlockSpec(memory_space=pltpu.SMEM)
    head = lambda col0: pl.BlockSpec((1, seq, HEAD_DIM), lambda h, b: (b, 0, col0 + h))
    return pl.pallas_call(
        functools.partial(_moba_kernel, units=units),
        grid=(N_HEADS, bsz),
        in_specs=[smem, smem, smem, smem, head(0), head(N_HEADS), head(2 * N_HEADS)],
        out_specs=head(0),
        out_shape=jax.ShapeDtypeStruct((bsz, seq, ATTN_WIDTH), F32),
        scratch_shapes=[pltpu.VMEM((seq, HEAD_DIM), BF16),
                        pltpu.VMEM((seq + (grp - 1) * blk, HEAD_DIM), BF16),
                        pltpu.VMEM((nblk + grp - 1, HEAD_DIM, blk), BF16),
                        pltpu.VMEM((2, 2 * blk, blk), F32),
                        pltpu.VMEM((len(units) * grp, SUBLANES, blk), F32),
                        pltpu.VMEM((2, grp * blk, blk), F32),
                        pltpu.VMEM((2, 1, blk), F32),
                        pltpu.VMEM((2, grp * blk, blk), BF16),
                        pltpu.VMEM((2, 1, blk), F32),
                        pltpu.VMEM((2, 1, blk), F32),
                        pltpu.VMEM((2, 1, blk), F32),
                        pltpu.VMEM((2, HEAD_DIM, blk), F32)],
        compiler_params=_params(("arbitrary", "arbitrary")),
    )(uq, uk, uf, rel_bias, proj3, proj3, proj3)


def _ssd_kernel(xs_ref, z_ref, bc_ref, dt_ref, cwx_ref, cwbc_ref, cbx_ref, cbbc_ref,
                dtb_ref, alog_ref, dsk_ref, ng_ref, tri_ref, trit_ref, exp_ref, o_ref,
                xtail_ref, bctail_ref, state_ref):
    c = pl.program_id(1)
    t = CHUNK
    pad = SUBLANES

    @pl.when(c == 0)
    def _():
        xtail_ref[...] = jnp.zeros_like(xtail_ref)
        bctail_ref[...] = jnp.zeros_like(bctail_ref)
        state_ref[...] = jnp.zeros_like(state_ref)

    def conv_silu(tail_ref, src_ref, w_ref, b_ref):
        x = src_ref[0]
        acc = _causal_conv(tail_ref[...], x, w_ref, b_ref[...])
        tail_ref[...] = x[t - pad:t, :]
        return _silu(acc)

    xh = conv_silu(xtail_ref, xs_ref, cwx_ref, cbx_ref)
    bc = conv_silu(bctail_ref, bc_ref, cwbc_ref, cbbc_ref)

    dtr = dt_ref[0] + dtb_ref[...]
    dt = jnp.maximum(dtr, 0.0) + jnp.log1p(jnp.exp(-jnp.abs(dtr)))
    adt = dt * (-jnp.exp(alog_ref[...]))
    acs = _dot_exact_lhs(tri_ref[...], adt)
    acs_t = _dot_exact_rhs(adt.T, trit_ref[...])

    stack = jnp.concatenate([dt, jnp.exp(acs), jnp.exp(acs[t - 1:t, :] - acs)], axis=0)
    wide = _dot_exact_rhs(stack, exp_ref[...])
    dt_x, eacs_x, dst_x = wide[0:t], wide[t:2 * t], wide[2 * t:3 * t]

    xdt = xh * dt_x
    xdt_bf = xdt.astype(BF16)
    xdec_bf = (xdt * dst_x).astype(BF16)

    row = lax.broadcasted_iota(jnp.int32, (t, t), 0)
    col = lax.broadcasted_iota(jnp.int32, (t, t), 1)
    tril = row >= col
    lane = lax.broadcasted_iota(jnp.int32, (t, LANES), 1)
    heads_per_group = N_SSM_HEADS // N_GROUPS
    nt = (((1,), (1,)), ((), ()))
    tn = (((0,), (0,)), ((), ()))

    y_parts = []
    for g in range(N_GROUPS):
        bg = bc[:, g * D_STATE:(g + 1) * D_STATE].astype(BF16)
        cg = bc[:, (N_GROUPS + g) * D_STATE:(N_GROUPS + g + 1) * D_STATE].astype(BF16)
        cb = _dot(cg, bg, nt)
        for pair in range(heads_per_group // 2):
            ms = []
            for r in (g * heads_per_group + 2 * pair, g * heads_per_group + 2 * pair + 1):
                seg = acs[:, r:r + 1] - acs_t[r:r + 1, :]
                ms.append((cb * jnp.exp(jnp.where(tril, seg, NEG))).astype(BF16))
            q = g * (heads_per_group // 2) + pair
            y2 = _dot(jnp.concatenate(ms, axis=0), xdt_bf[:, q * LANES:(q + 1) * LANES])
            y_parts.append(jnp.where(lane < SSM_HEAD_DIM, y2[0:t], y2[t:2 * t]))
    y = jnp.concatenate(y_parts, axis=1)

    off_parts = []
    for g in range(N_GROUPS):
        sl = slice(g * GROUP_WIDTH, (g + 1) * GROUP_WIDTH)
        bg = bc[:, g * D_STATE:(g + 1) * D_STATE].astype(BF16)
        cg = bc[:, (N_GROUPS + g) * D_STATE:(N_GROUPS + g + 1) * D_STATE].astype(BF16)
        st = state_ref[:, sl]
        off_parts.append(_dot(cg, st.astype(BF16)))
        state_ref[:, sl] = eacs_x[t - 1:t, sl] * st + _dot(bg, xdec_bf[:, sl], tn)
    y = y + jnp.concatenate(off_parts, axis=1) * eacs_x + dsk_ref[...] * xh
    y = y * _silu(z_ref[0])

    outs = []
    for g in range(N_GROUPS):
        sl = slice(g * GROUP_WIDTH, (g + 1) * GROUP_WIDTH)
        outs.append(_rms(y[:, sl]) * ng_ref[:, sl])
    o_ref[0] = jnp.concatenate(outs, axis=1).astype(o_ref.dtype)


def _ssd(proj3, conv_w, conv_b, dt_bias, a_log, d_skip, norm_g):
    bsz, seq, _ = proj3.shape
    t = CHUNK
    gn2 = 2 * N_GROUPS * D_STATE
    pad_h = LANES - N_SSM_HEADS
    tri = jnp.tril(jnp.ones((t, t), F32)).astype(BF16)
    expand = jnp.pad(jnp.repeat(jnp.eye(N_SSM_HEADS, dtype=F32), SSM_HEAD_DIM, axis=1),
                     ((0, pad_h), (0, 0))).astype(BF16)
    const = lambda shape: pl.BlockSpec(shape, lambda b, c: (0,) * len(shape))
    return pl.pallas_call(
        _ssd_kernel,
        grid=(bsz, seq // t),
        in_specs=[pl.BlockSpec((1, t, SSM_WIDTH), lambda b, c: (b, c, COL_XS // SSM_WIDTH)),
                  pl.BlockSpec((1, t, SSM_WIDTH), lambda b, c: (b, c, COL_Z // SSM_WIDTH)),
                  pl.BlockSpec((1, t, gn2), lambda b, c: (b, c, COL_BC // gn2)),
                  pl.BlockSpec((1, t, LANES), lambda b, c: (b, c, COL_DT // LANES)),
                  const((SSM_CONV, SSM_WIDTH)), const((SSM_CONV, gn2)),
                  const((1, SSM_WIDTH)), const((1, gn2)),
                  const((1, LANES)), const((1, LANES)),
                  const((1, SSM_WIDTH)), const((1, SSM_WIDTH)),
                  const((t, t)), const((t, t)), const((LANES, SSM_WIDTH))],
        out_specs=pl.BlockSpec((1, t, SSM_WIDTH), lambda b, c: (b, c, 0)),
        out_shape=jax.ShapeDtypeStruct((bsz, seq, SSM_WIDTH), BF16),
        scratch_shapes=[pltpu.VMEM((SUBLANES, SSM_WIDTH), F32),
                        pltpu.VMEM((SUBLANES, gn2), F32),
                        pltpu.VMEM((D_STATE, SSM_WIDTH), F32)],
        compiler_params=_params(("arbitrary", "arbitrary")),
    )(proj3, proj3, proj3, proj3,
      conv_w[:, :SSM_WIDTH], conv_w[:, SSM_WIDTH:],
      conv_b[:SSM_WIDTH].reshape(1, -1), conv_b[SSM_WIDTH:].reshape(1, -1),
      jnp.pad(dt_bias, (0, pad_h)).reshape(1, LANES), jnp.pad(a_log, (0, pad_h)).reshape(1, LANES),
      jnp.repeat(d_skip, SSM_HEAD_DIM).reshape(1, SSM_WIDTH), norm_g.reshape(1, SSM_WIDTH),
      tri, tri.T, expand)


def _outproj_kernel(attn_ref, ssm_ref, x_ref, ag_ref, w_ref, gm_ref, g2_ref, sc_ref, sh_ref,
                    x1_ref, h2_ref, *, sub):
    for r0 in range(0, x_ref.shape[0], sub):
        rows = slice(r0, r0 + sub)
        a = _rms(attn_ref[rows, :]) * ag_ref[...]
        lhs = jnp.concatenate([a.astype(BF16), ssm_ref[rows, :]], axis=-1)
        x1 = x_ref[rows, :] + gm_ref[0] * _dot(lhs, w_ref[...])
        x1_ref[rows, :] = x1
        y = _rms(x1) * g2_ref[...]
        h2_ref[rows, :] = (y * (1.0 + sc_ref[0]) + sh_ref[0]).astype(BF16)


def _outproj(attn2, ssm2, x2, attn_g, w_out_bf, gate_m, g2, scale_f, shift_f, seq, tm, sub):
    m, d = x2.shape
    per_b = seq // tm
    vec = pl.BlockSpec((1, 1, d), lambda i: (i // per_b, 0, 0))
    half = pl.BlockSpec((tm, ATTN_WIDTH), lambda i: (i, 0))
    full = pl.BlockSpec((tm, d), lambda i: (i, 0))
    return pl.pallas_call(
        functools.partial(_outproj_kernel, sub=sub),
        grid=(m // tm,),
        in_specs=[half, half, full,
                  pl.BlockSpec((1, ATTN_WIDTH), lambda i: (0, 0)),
                  _resident((d, d)),
                  vec, pl.BlockSpec((1, d), lambda i: (0, 0)), vec, vec],
        out_specs=[full, full],
        out_shape=[jax.ShapeDtypeStruct((m, d), F32), jax.ShapeDtypeStruct((m, d), BF16)],
        compiler_params=_params(("parallel",)),
    )(attn2, ssm2, x2, attn_g.reshape(1, -1), w_out_bf, gate_m, g2.reshape(1, d), scale_f, shift_f)


def _ffn_up_kernel(h_ref, wg_ref, wv_ref, cw_ref, cb_ref, wnext_ref, o_ref, wnext_bf_ref, wb_ref, tail_ref,
                   *, tm, sub, seq):
    i = pl.program_id(1)
    pad = SUBLANES
    tn = o_ref.shape[1]

    @pl.when(i == 0)
    def _():
        wb_ref[:, 0:tn] = wg_ref[...].astype(BF16)
        wb_ref[:, tn:2 * tn] = wv_ref[...].astype(BF16)

    @pl.when((i * tm) % seq == 0)
    def _():
        tail_ref[...] = jnp.zeros((pad, 2 * tn), F32)

    tail = tail_ref[...]
    for r0 in range(0, tm, sub):
        u = _dot(h_ref[r0:r0 + sub, :], wb_ref[...])
        acc = _causal_conv(tail, u, cw_ref, cb_ref[...])
        tail = u[sub - pad:sub, :]
        o_ref[r0:r0 + sub, :] = (_silu(acc[:, 0:tn]) * acc[:, tn:2 * tn]).astype(o_ref.dtype)
    tail_ref[...] = tail
    wnext_bf_ref[...] = wnext_ref[...].astype(BF16)


def _ffn_up(h2, w_up, conv_w, conv_b, w_next, seq, tm, sub, tn):
    m, d = h2.shape
    nj = FFN_DIM // tn
    ni = m // tm
    ride = pl.BlockSpec((w_next.shape[0] // (nj * ni), w_next.shape[1]), lambda j, i: (j * ni + i, 0))
    assert w_next.shape[0] % (nj * ni * 2 * SUBLANES) == 0
    pair = lambda a: jnp.concatenate([a[:, :FFN_DIM].reshape(-1, nj, tn), a[:, FFN_DIM:].reshape(-1, nj, tn)],
                                     axis=2).reshape(-1, 2 * FFN_DIM)
    return pl.pallas_call(
        functools.partial(_ffn_up_kernel, tm=tm, sub=sub, seq=seq),
        grid=(nj, ni),
        in_specs=[pl.BlockSpec((tm, d), lambda j, i: (i, 0)),
                  pl.BlockSpec((d, tn), lambda j, i: (0, j)),
                  pl.BlockSpec((d, tn), lambda j, i: (0, j + nj)),
                  pl.BlockSpec((FFN_CONV, 2 * tn), lambda j, i: (0, j)),
                  pl.BlockSpec((1, 2 * tn), lambda j, i: (0, j)),
                  ride],
        out_specs=[pl.BlockSpec((tm, tn), lambda j, i: (i, j)), ride],
        out_shape=[jax.ShapeDtypeStruct((m, FFN_DIM), BF16), jax.ShapeDtypeStruct(w_next.shape, BF16)],
        scratch_shapes=[pltpu.VMEM((d, 2 * tn), BF16), pltpu.VMEM((SUBLANES, 2 * tn), F32)],
        compiler_params=_params(("arbitrary", "arbitrary")),
    )(h2, w_up, w_up, pair(conv_w), pair(conv_b.reshape(1, -1)), w_next)


def _ffn_down_kernel(a_ref, w_ref, x1_ref, gf_ref, fg_ref, o_ref):
    x2 = x1_ref[...] + gf_ref[0] * _dot(a_ref[...], w_ref[...])
    o_ref[...] = _rms(x2) * fg_ref[...]


def _ffn_down(act, w_down_bf, x1, gate_f, final_g, seq, tm):
    m, d = x1.shape
    per_b = seq // tm
    row = lambda width: pl.BlockSpec((tm, width), lambda i: (i, 0))
    return pl.pallas_call(
        _ffn_down_kernel,
        grid=(m // tm,),
        in_specs=[row(FFN_DIM), _resident((FFN_DIM, d)), row(d),
                  pl.BlockSpec((1, 1, d), lambda i: (i // per_b, 0, 0)),
                  pl.BlockSpec((1, d), lambda i: (0, 0))],
        out_specs=row(d),
        out_shape=jax.ShapeDtypeStruct((m, d), F32),
        compiler_params=_params(("parallel",)),
    )(act, w_down_bf, x1, gate_f, final_g.reshape(1, d))


def kernel(x, c, w_ada, b_ada, norm_mix_g, w_in, rel_bias, attn_norm_g, conv_ssm_w, conv_ssm_b, dt_bias,
           a_log, d_skip, ssm_norm_g, w_out, norm_ffn_g, w_up, conv_ffn_w, conv_ffn_b, w_down, final_norm_g):
    bsz, seq, d = x.shape
    m = bsz * seq
    assert w_ada.shape[0] == 1, "the final RMSNorm is fused into the (single) layer's ffn_down kernel"
    l = 0
    x2 = x.reshape(m, d)
    mod = _ada(c, w_ada[l], b_ada[l])
    shift_m, scale_m, gate_m, shift_f, scale_f, gate_f = [
        mod[:, k * d:(k + 1) * d].reshape(bsz, 1, d) for k in range(6)]

    w_in_t = w_in[l].T
    proj, w_out_bf = _normproj(x2, norm_mix_g[l], scale_m, shift_m, w_in_t.astype(BF16), PROJ_COLS, w_out[l],
                               seq, 256, 256)
    proj3 = proj.reshape(bsz, seq, PROJ_COLS)

    attn = _moba(rel_bias, proj3)
    ssm = _ssd(proj3, conv_ssm_w[l], conv_ssm_b[l], dt_bias[l], a_log[l], d_skip[l], ssm_norm_g[l])

    x1, h2 = _outproj(attn.reshape(m, ATTN_WIDTH), ssm.reshape(m, SSM_WIDTH), x2, attn_norm_g[l],
                      w_out_bf, gate_m, norm_ffn_g[l], scale_f, shift_f, seq, 512, 256)
    act, w_down_bf = _ffn_up(h2, w_up[l], conv_ffn_w[l], conv_ffn_b[l], w_down[l], seq, 1024, 256, 512)
    out = _ffn_down(act, w_down_bf, x1, gate_f, final_norm_g, seq, 256)
    return out.reshape(bsz, seq, d)
```

```python
import functools
import math

import jax
import jax.numpy as jnp
from jax import lax
from jax.experimental import pallas as pl
from jax.experimental.pallas import tpu as pltpu

F32 = jnp.float32
BF16 = jnp.bfloat16

ATTN_WIDTH = 1024
HEAD_DIM = 128
N_HEADS = 8
SSM_WIDTH = 1024
SSM_HEAD_DIM = 64
N_SSM_HEADS = 16
N_GROUPS = 2
GROUP_WIDTH = SSM_WIDTH // N_GROUPS
D_STATE = 128
SSM_CONV = 4
CHUNK = 256
MOBA_BLOCK = 256
MOBA_TOPK = 3
MOBA_FAR_GROUP = 4
MOBA_UNROLL = 2
REL_BUCKETS = 32
REL_MAX_DIST = 128
FFN_DIM = 5632
FFN_CONV = 3
EPS = 1e-6
NEG = -1e30

LANES = 128
SUBLANES = 8
VMEM_LIMIT = 56 * 1024 * 1024

ADA_TN = 1024
PROJ_TM, PROJ_SUB = 256, 256
OUT_TM, OUT_SUB = 512, 256
UP_TM, UP_SUB, UP_TN = 1024, 256, 512
DOWN_TM, DOWN_SUB = 512, 256

PROJ_COLS = 5760
COL_XS = 3 * ATTN_WIDTH
COL_Z = COL_XS + SSM_WIDTH
COL_BC = COL_Z + SSM_WIDTH
COL_DT = COL_BC + 2 * N_GROUPS * D_STATE


def _params(sem):
    return pltpu.CompilerParams(dimension_semantics=sem, vmem_limit_bytes=VMEM_LIMIT)


def _split3(x):
    hi = x.astype(BF16)
    r = x - hi.astype(F32)
    mid = r.astype(BF16)
    lo = (r - mid.astype(F32)).astype(BF16)
    return hi, mid, lo


NT = (((1,), (1,)), ((), ()))


def _dot(a, b, dims=(((1,), (0,)), ((), ()))):
    return lax.dot_general(a, b, dims, preferred_element_type=F32)


def _dot_exact_lhs(a_bf, x, dims=(((1,), (0,)), ((), ()))):
    hi, mid, lo = _split3(x)
    return _dot(a_bf, hi, dims) + _dot(a_bf, mid, dims) + _dot(a_bf, lo, dims)


def _dot_exact_rhs(x, b_bf, dims=(((1,), (0,)), ((), ()))):
    hi, mid, lo = _split3(x)
    return _dot(hi, b_bf, dims) + _dot(mid, b_bf, dims) + _dot(lo, b_bf, dims)


def _silu(x):
    return x * jax.nn.sigmoid(x)


def _rms(x):
    return x * lax.rsqrt(jnp.mean(x * x, axis=-1, keepdims=True) + EPS)


def _causal_conv(tail, x, w_ref, b):
    rows, width = x.shape
    taps = w_ref.shape[0]
    ext = jnp.concatenate([tail, x], axis=0).reshape(rows // SUBLANES + 1, SUBLANES, width)
    row_in_slab = lax.broadcasted_iota(jnp.int32, (rows // SUBLANES, SUBLANES, width), 1)
    acc = b + w_ref[taps - 1:taps, :] * x
    rot = ext
    for back in range(1, taps):
        rot = pltpu.roll(rot, 1, axis=1)
        shifted = jnp.where(row_in_slab < back, rot[:-1], rot[1:]).reshape(rows, width)
        acc = acc + w_ref[taps - 1 - back:taps - back, :] * shifted
    return acc


def _ada_kernel(ct_ref, w_ref, b_ref, o_ref, sb_ref):
    nb = sb_ref.shape[0]
    d = w_ref.shape[0]
    tn = o_ref.shape[-1]

    @pl.when(pl.program_id(0) == 0)
    def _():
        ct = ct_ref[...]
        st = _silu(ct)
        for b in range(nb):
            sb_ref[b] = jnp.broadcast_to(st[:, b:b + 1], (d, LANES))

    def body(kc, accs):
        r = pl.multiple_of(kc * SUBLANES, SUBLANES)
        w8 = w_ref[pl.ds(r, SUBLANES), :]
        out = []
        for b in range(nb):
            s8 = sb_ref[b, pl.ds(r, SUBLANES), :]
            out.append(accs[b] + w8 * jnp.tile(s8, (1, tn // LANES)))
        return tuple(out)

    accs = lax.fori_loop(0, d // SUBLANES, body,
                         tuple(jnp.zeros((SUBLANES, tn), F32) for _ in range(nb)), unroll=8)
    for b in range(nb):
        o_ref[b:b + 1, :] = jnp.sum(accs[b], axis=0, keepdims=True) + b_ref[...]


def _ada(c, w_ada, b_ada, tn=ADA_TN):
    nb, d = c.shape
    n = w_ada.shape[1]
    return pl.pallas_call(
        _ada_kernel,
        grid=(n // tn,),
        in_specs=[pl.BlockSpec((d, nb), lambda j: (0, 0)),
                  pl.BlockSpec((d, tn), lambda j: (0, j)),
                  pl.BlockSpec((1, tn), lambda j: (0, j))],
        out_specs=pl.BlockSpec((nb, tn), lambda j: (0, j)),
        out_shape=jax.ShapeDtypeStruct((nb, n), F32),
        scratch_shapes=[pltpu.VMEM((nb, d, LANES), F32)],
        compiler_params=_params(("arbitrary",)),
    )(c.T, w_ada, b_ada.reshape(1, n))


def _normproj_kernel(x_ref, g_ref, sc_ref, sh_ref, w_ref, wnext_ref, o_ref, wnext_bf_ref, *, sub):
    n = w_ref.shape[0]
    for r0 in range(0, x_ref.shape[0], sub):
        rows = slice(r0, r0 + sub)
        y = _rms(x_ref[rows, :]) * g_ref[...]
        h = y * (1.0 + sc_ref[0]) + sh_ref[0]
        o_ref[rows, 0:n] = _dot(h.astype(BF16), w_ref[...], NT)
        if n < o_ref.shape[1]:
            o_ref[rows, n:] = jnp.zeros((sub, o_ref.shape[1] - n), F32)
    wnext_bf_ref[...] = wnext_ref[...].astype(BF16)


def _resident(shape):
    return pl.BlockSpec(shape, lambda *_: (0,) * len(shape), pipeline_mode=pl.Buffered(1))


def _normproj(x2, g, scale, shift, w_t, n_out, w_next, seq, tm, sub):
    m, d = x2.shape
    per_b = seq // tm
    steps = m // tm
    assert w_next.shape[0] % (steps * 2 * SUBLANES) == 0
    ride = pl.BlockSpec((w_next.shape[0] // steps, w_next.shape[1]), lambda i: (i, 0))
    vec = pl.BlockSpec((1, 1, d), lambda i: (i // per_b, 0, 0))
    return pl.pallas_call(
        functools.partial(_normproj_kernel, sub=sub),
        grid=(m // tm,),
        in_specs=[pl.BlockSpec((tm, d), lambda i: (i, 0)),
                  pl.BlockSpec((1, d), lambda i: (0, 0)),
                  vec, vec, _resident(w_t.shape), ride],
        out_specs=[pl.BlockSpec((tm, n_out), lambda i: (i, 0)), ride],
        out_shape=[jax.ShapeDtypeStruct((m, n_out), F32), jax.ShapeDtypeStruct(w_next.shape, BF16)],
        compiler_params=_params(("parallel",)),
    )(x2, g.reshape(1, d), scale, shift, w_t, w_next)


def _rel_bucket(dist):
    n = jnp.maximum(dist, 0)
    max_exact = REL_BUCKETS // 2
    nf = jnp.maximum(n, max_exact).astype(F32)
    large = max_exact + (jnp.log(nf / max_exact) / math.log(REL_MAX_DIST / max_exact)
                         * (REL_BUCKETS - max_exact)).astype(jnp.int32)
    large = jnp.minimum(large, REL_BUCKETS - 1)
    return jnp.where(n < max_exact, n, large)


def _moba_units(nblk, grp):
    lead = grp - 1
    units = []
    for qi in range(nblk):
        units.append((qi, qi, 1))
        for t in range(-(-max(qi - lead, 0) // grp)):
            units.append((qi, lead + t * grp, 0))
    return units


def _moba_kernel(uq_ref, uk_ref, uf_ref, rb_ref, q_ref, k_ref, v_ref, o_ref,
                 qbf_ref, kbf_ref, vt_ref, tab_ref, mask_ref, s_buf, top_buf, p_buf, alpha_buf, m_ref, l_ref, acc_ref,
                 *, units):
    h = pl.program_id(0)
    seq = k_ref.shape[1]
    blk = MOBA_BLOCK
    grp = MOBA_FAR_GROUP
    lead = grp - 1
    nblk = seq // blk
    nunits = len(units)
    scale = HEAD_DIM ** -0.5
    exp2_scale = scale * math.log2(math.e)

    kf = k_ref[0]
    qf = q_ref[0]
    qbf_ref[...] = (qf * exp2_scale).astype(BF16)
    kbf_ref[0:lead * blk, :] = jnp.zeros((lead * blk, HEAD_DIM), BF16)
    kbf_ref[lead * blk:, :] = kf.astype(BF16)
    kmean = jnp.mean(kf.reshape(nblk, blk, HEAD_DIM), axis=1)
    for j in range(lead):
        vt_ref[j] = jnp.zeros((HEAD_DIM, blk), BF16)
    for j in range(nblk):
        vt_ref[lead + j] = v_ref[0, j * blk:(j + 1) * blk, :].T.astype(BF16)

    @pl.when(pl.program_id(1) == 0)
    def _():
        b_far = rb_ref[REL_BUCKETS - 1, h]
        kk = lax.broadcasted_iota(jnp.int32, (blk, blk), 0)
        qq = lax.broadcasted_iota(jnp.int32, (blk, blk), 1)
        tab_ref[0] = jnp.zeros((2 * blk, blk), F32)
        for pos, dist in ((0, qq - kk + blk), (1, qq - kk)):
            bucket = _rel_bucket(dist)
            tab = jnp.zeros((blk, blk), F32)
            for b in range(REL_BUCKETS):
                tab = jnp.where(bucket == b, (rb_ref[b, h] - b_far) * math.log2(math.e), tab)
            tab_ref[1, pos * blk:(pos + 1) * blk, :] = jnp.where(dist >= 0, tab, NEG)

    q_hi, q_mid, _ = _split3(qf)
    k_hi, k_mid, _ = _split3(kmean)
    gate = _dot(k_hi, q_hi, NT) + _dot(k_hi, q_mid, NT) + _dot(k_mid, q_hi, NT)
    nidx = lax.broadcasted_iota(jnp.int32, (nblk, seq), 0)
    qblk = lax.broadcasted_iota(jnp.int32, (nblk, seq), 1) // blk
    nidx_f = nidx.astype(F32)
    avail = jnp.where(nidx < qblk, 1.0, 0.0)
    chosen = jnp.zeros((nblk, seq), F32)
    for _ in range(MOBA_TOPK):
        gm = jnp.where(avail > 0.0, gate, -jnp.inf)
        best = jnp.max(gm, axis=0, keepdims=True)
        first = jnp.min(jnp.where((gm == best) & (avail > 0.0), nidx_f, float(nblk)), axis=0, keepdims=True)
        hit = nidx_f == first
        chosen = jnp.where(hit, 1.0, chosen)
        avail = jnp.where(hit, 0.0, avail)
    add_all = jnp.where(chosen > 0.0, 0.0, NEG)
    for u, (qi, ks, is_first) in enumerate(units):
        cols = slice(qi * blk, (qi + 1) * blk)
        for n in range(grp):
            j = ks + n - lead
            if j == qi and is_first:
                row = jnp.zeros((1, blk), F32)
            elif 0 <= j < (qi if is_first else qi - lead):
                row = add_all[j:j + 1, cols]
            else:
                row = jnp.full((1, blk), NEG, F32)
            mask_ref[u * grp + n] = jnp.broadcast_to(row, (SUBLANES, blk))

    def unit(u):
        uc = jnp.clip(u, 0, nunits - 1)
        return uq_ref[uc], uk_ref[uc], uf_ref[uc], uc

    def scores(slot, u):
        qi, ks, is_first, uc = unit(u)
        qb = qbf_ref[pl.ds(pl.multiple_of(qi * blk, blk), blk), :]
        kg = kbf_ref[pl.ds(pl.multiple_of(ks * blk, blk), grp * blk), :]
        s = _dot(kg, qb, NT)
        tab = tab_ref[is_first]
        parts, top = [], None
        for n in range(grp):
            part = s[n * blk:(n + 1) * blk]
            if n >= grp - 2:
                part = part + tab[(n - grp + 2) * blk:(n - grp + 3) * blk]
            parts.append(part)
            best = jnp.max(part, axis=0, keepdims=True) + mask_ref[uc * grp + n][0:1, :]
            top = best if top is None else jnp.maximum(top, best)
        s_buf[slot] = jnp.concatenate(parts, axis=0)
        top_buf[slot] = top

    def softmax(slot, u):
        qi, _, is_first, uc = unit(u)
        st = qi % 2
        m_old = jnp.where(is_first == 1, NEG, m_ref[st])
        l_old = jnp.where(is_first == 1, 0.0, l_ref[st])
        m_new = jnp.maximum(m_old, top_buf[slot])
        alpha = jnp.exp2(m_old - m_new)
        total = alpha * l_old
        for n in range(grp):
            offset = m_new - mask_ref[uc * grp + n][0:1, :]
            p = jnp.exp2(s_buf[slot, n * blk:(n + 1) * blk, :] - offset)
            total = total + jnp.sum(p, axis=0, keepdims=True)
            p_buf[slot, n * blk:(n + 1) * blk, :] = p.astype(BF16)
        m_ref[st] = m_new
        l_ref[st] = total
        alpha_buf[slot] = alpha

    def values(slot, u):
        qi, ks, _, _ = unit(u)
        st = qi % 2
        acc = alpha_buf[slot] * acc_ref[st]
        for n in range(grp):
            acc = acc + _dot(vt_ref[ks + n], p_buf[slot, n * blk:(n + 1) * blk, :])
        acc_ref[st] = acc
        o_ref[0, pl.ds(pl.multiple_of(qi * blk, blk), blk), :] = (acc * (1.0 / l_ref[st])).T

    m_ref[...] = jnp.full(m_ref.shape, NEG, F32)
    l_ref[...] = jnp.ones(l_ref.shape, F32)
    acc_ref[...] = jnp.zeros(acc_ref.shape, F32)
    alpha_buf[1] = jnp.ones((1, blk), F32)
    p_buf[1] = jnp.zeros((grp * blk, blk), BF16)
    scores(0, 0)

    def trip(u, slot):
        values(1 - slot, u - 1)
        softmax(slot, u)
        scores(1 - slot, u + 1)

    def trips(it, carry):
        for n in range(MOBA_UNROLL):
            trip(MOBA_UNROLL * it + n, n % 2)
        return carry

    lax.fori_loop(0, nunits // MOBA_UNROLL, trips, 0)
    for u in range(nunits - nunits % MOBA_UNROLL, nunits):
        trip(u, u % 2)
    values((nunits - 1) % 2, nunits - 1)


def _moba(rel_bias, proj3):
    bsz, seq, _ = proj3.shape
    blk = MOBA_BLOCK
    grp = MOBA_FAR_GROUP
    nblk = seq // blk
    assert seq % blk == 0 and grp >= 2
    units = _moba_units(nblk, grp)
    uq, uk, uf = (jnp.asarray([u[c] for u in units], jnp.int32) for c in range(3))
    smem = pl.BlockSpec(memory_space=pltpu.SMEM)
    head = lambda col0: pl.BlockSpec((1, seq, HEAD_DIM), lambda h, b: (b, 0, col0 + h))
    return pl.pallas_call(
        functools.partial(_moba_kernel, units=units),
        grid=(N_HEADS, bsz),
        in_specs=[smem, smem, smem, smem, head(0), head(N_HEADS), head(2 * N_HEADS)],
        out_specs=head(0),
        out_shape=jax.ShapeDtypeStruct((bsz, seq, ATTN_WIDTH), F32),
        scratch_shapes=[pltpu.VMEM((seq, HEAD_DIM), BF16),
                        pltpu.VMEM((seq + (grp - 1) * blk, HEAD_DIM), BF16),
                        pltpu.VMEM((nblk + grp - 1, HEAD_DIM, blk), BF16),
                        pltpu.VMEM((2, 2 * blk, blk), F32),
                        pltpu.VMEM((len(units) * grp, SUBLANES, blk), F32),
                        pltpu.VMEM((2, grp * blk, blk), F32),
                        pltpu.VMEM((2, 1, blk), F32),
                        pltpu.VMEM((2, grp * blk, blk), BF16),
                        pltpu.VMEM((2, 1, blk), F32),
                        pltpu.VMEM((2, 1, blk), F32),
                        pltpu.VMEM((2, 1, blk), F32),
                        pltpu.VMEM((2, HEAD_DIM, blk), F32)],
        compiler_params=_params(("arbitrary", "arbitrary")),
    )(uq, uk, uf, rel_bias, proj3, proj3, proj3)


def _ssd_kernel(xs_ref, z_ref, bc_ref, dt_ref, cwx_ref, cwbc_ref, cbx_ref, cbbc_ref,
                dtb_ref, alog_ref, dsk_ref, ng_ref, tri_ref, trit_ref, exp_ref, o_ref,
                xtail_ref, bctail_ref, state_ref):
    c = pl.program_id(1)
    t = CHUNK
    pad = SUBLANES

    @pl.when(c == 0)
    def _():
        xtail_ref[...] = jnp.zeros_like(xtail_ref)
        bctail_ref[...] = jnp.zeros_like(bctail_ref)
        state_ref[...] = jnp.zeros_like(state_ref)

    def conv_silu(tail_ref, src_ref, w_ref, b_ref):
        x = src_ref[0]
        acc = _causal_conv(tail_ref[...], x, w_ref, b_ref[...])
        tail_ref[...] = x[t - pad:t, :]
        return _silu(acc)

    xh = conv_silu(xtail_ref, xs_ref, cwx_ref, cbx_ref)
    bc = conv_silu(bctail_ref, bc_ref, cwbc_ref, cbbc_ref)

    dtr = dt_ref[0] + dtb_ref[...]
    dt = jnp.maximum(dtr, 0.0) + jnp.log1p(jnp.exp(-jnp.abs(dtr)))
    adt = dt * (-jnp.exp(alog_ref[...]))
    acs = _dot_exact_lhs(tri_ref[...], adt)
    acs_t = _dot_exact_rhs(adt.T, trit_ref[...])

    stack = jnp.concatenate([dt, jnp.exp(acs), jnp.exp(acs[t - 1:t, :] - acs)], axis=0)
    wide = _dot_exact_rhs(stack, exp_ref[...])
    dt_x, eacs_x, dst_x = wide[0:t], wide[t:2 * t], wide[2 * t:3 * t]

    xdt = xh * dt_x
    xdt_bf = xdt.astype(BF16)
    xdec_bf = (xdt * dst_x).astype(BF16)

    row = lax.broadcasted_iota(jnp.int32, (t, t), 0)
    col = lax.broadcasted_iota(jnp.int32, (t, t), 1)
    tril = row >= col
    lane = lax.broadcasted_iota(jnp.int32, (t, LANES), 1)
    heads_per_group = N_SSM_HEADS // N_GROUPS
    nt = (((1,), (1,)), ((), ()))
    tn = (((0,), (0,)), ((), ()))

    y_parts = []
    for g in range(N_GROUPS):
        bg = bc[:, g * D_STATE:(g + 1) * D_STATE].astype(BF16)
        cg = bc[:, (N_GROUPS + g) * D_STATE:(N_GROUPS + g + 1) * D_STATE].astype(BF16)
        cb = _dot(cg, bg, nt)
        for pair in range(heads_per_group // 2):
            ms = []
            for r in (g * heads_per_group + 2 * pair, g * heads_per_group + 2 * pair + 1):
                seg = acs[:, r:r + 1] - acs_t[r:r + 1, :]
                ms.append((cb * jnp.exp(jnp.where(tril, seg, NEG))).astype(BF16))
            q = g * (heads_per_group // 2) + pair
            y2 = _dot(jnp.concatenate(ms, axis=0), xdt_bf[:, q * LANES:(q + 1) * LANES])
            y_parts.append(jnp.where(lane < SSM_HEAD_DIM, y2[0:t], y2[t:2 * t]))
    y = jnp.concatenate(y_parts, axis=1)

    off_parts = []
    for g in range(N_GROUPS):
        sl = slice(g * GROUP_WIDTH, (g + 1) * GROUP_WIDTH)
        bg = bc[:, g * D_STATE:(g + 1) * D_STATE].astype(BF16)
        cg = bc[:, (N_GROUPS + g) * D_STATE:(N_GROUPS + g + 1) * D_STATE].astype(BF16)
        st = state_ref[:, sl]
        off_parts.append(_dot(cg, st.astype(BF16)))
        state_ref[:, sl] = eacs_x[t - 1:t, sl] * st + _dot(bg, xdec_bf[:, sl], tn)
    y = y + jnp.concatenate(off_parts, axis=1) * eacs_x + dsk_ref[...] * xh
    y = y * _silu(z_ref[0])

    outs = []
    for g in range(N_GROUPS):
        sl = slice(g * GROUP_WIDTH, (g + 1) * GROUP_WIDTH)
        outs.append(_rms(y[:, sl]) * ng_ref[:, sl])
    o_ref[0] = jnp.concatenate(outs, axis=1).astype(o_ref.dtype)


def _ssd(proj3, conv_w, conv_b, dt_bias, a_log, d_skip, norm_g):
    bsz, seq, _ = proj3.shape
    t = CHUNK
    gn2 = 2 * N_GROUPS * D_STATE
    pad_h = LANES - N_SSM_HEADS
    tri = jnp.tril(jnp.ones((t, t), F32)).astype(BF16)
    expand = jnp.pad(jnp.repeat(jnp.eye(N_SSM_HEADS, dtype=F32), SSM_HEAD_DIM, axis=1),
                     ((0, pad_h), (0, 0))).astype(BF16)
    const = lambda shape: pl.BlockSpec(shape, lambda b, c: (0,) * len(shape))
    return pl.pallas_call(
        _ssd_kernel,
        grid=(bsz, seq // t),
        in_specs=[pl.BlockSpec((1, t, SSM_WIDTH), lambda b, c: (b, c, COL_XS // SSM_WIDTH)),
                  pl.BlockSpec((1, t, SSM_WIDTH), lambda b, c: (b, c, COL_Z // SSM_WIDTH)),
                  pl.BlockSpec((1, t, gn2), lambda b, c: (b, c, COL_BC // gn2)),
                  pl.BlockSpec((1, t, LANES), lambda b, c: (b, c, COL_DT // LANES)),
                  const((SSM_CONV, SSM_WIDTH)), const((SSM_CONV, gn2)),
                  const((1, SSM_WIDTH)), const((1, gn2)),
                  const((1, LANES)), const((1, LANES)),
                  const((1, SSM_WIDTH)), const((1, SSM_WIDTH)),
                  const((t, t)), const((t, t)), const((LANES, SSM_WIDTH))],
        out_specs=pl.BlockSpec((1, t, SSM_WIDTH), lambda b, c: (b, c, 0)),
        out_shape=jax.ShapeDtypeStruct((bsz, seq, SSM_WIDTH), BF16),
        scratch_shapes=[pltpu.VMEM((SUBLANES, SSM_WIDTH), F32),
                        pltpu.VMEM((SUBLANES, gn2), F32),
                        pltpu.VMEM((D_STATE, SSM_WIDTH), F32)],
        compiler_params=_params(("arbitrary", "arbitrary")),
    )(proj3, proj3, proj3, proj3,
      conv_w[:, :SSM_WIDTH], conv_w[:, SSM_WIDTH:],
      conv_b[:SSM_WIDTH].reshape(1, -1), conv_b[SSM_WIDTH:].reshape(1, -1),
      jnp.pad(dt_bias, (0, pad_h)).reshape(1, LANES), jnp.pad(a_log, (0, pad_h)).reshape(1, LANES),
      jnp.repeat(d_skip, SSM_HEAD_DIM).reshape(1, SSM_WIDTH), norm_g.reshape(1, SSM_WIDTH),
      tri, tri.T, expand)


def _outproj_kernel(attn_ref, ssm_ref, x_ref, ag_ref, w_ref, gm_ref, g2_ref, sc_ref, sh_ref,
                    x1_ref, h2_ref, *, sub):
    for r0 in range(0, x_ref.shape[0], sub):
        rows = slice(r0, r0 + sub)
        a = _rms(attn_ref[rows, :]) * ag_ref[...]
        lhs = jnp.concatenate([a.astype(BF16), ssm_ref[rows, :]], axis=-1)
        x1 = x_ref[rows, :] + gm_ref[0] * _dot(lhs, w_ref[...])
        x1_ref[rows, :] = x1
        y = _rms(x1) * g2_ref[...]
        h2_ref[rows, :] = (y * (1.0 + sc_ref[0]) + sh_ref[0]).astype(BF16)


def _outproj(attn2, ssm2, x2, attn_g, w_out_bf, gate_m, g2, scale_f, shift_f, seq, tm, sub):
    m, d = x2.shape
    per_b = seq // tm
    vec = pl.BlockSpec((1, 1, d), lambda i: (i // per_b, 0, 0))
    half = pl.BlockSpec((tm, ATTN_WIDTH), lambda i: (i, 0))
    full = pl.BlockSpec((tm, d), lambda i: (i, 0))
    return pl.pallas_call(
        functools.partial(_outproj_kernel, sub=sub),
        grid=(m // tm,),
        in_specs=[half, half, full,
                  pl.BlockSpec((1, ATTN_WIDTH), lambda i: (0, 0)),
                  _resident((d, d)),
                  vec, pl.BlockSpec((1, d), lambda i: (0, 0)), vec, vec],
        out_specs=[full, full],
        out_shape=[jax.ShapeDtypeStruct((m, d), F32), jax.ShapeDtypeStruct((m, d), BF16)],
        compiler_params=_params(("parallel",)),
    )(attn2, ssm2, x2, attn_g.reshape(1, -1), w_out_bf, gate_m, g2.reshape(1, d), scale_f, shift_f)


def _ffn_up_kernel(h_ref, wg_ref, wv_ref, cw_ref, cb_ref, wnext_ref, o_ref, wnext_bf_ref, wb_ref, tail_ref,
                   *, tm, sub, seq):
    i = pl.program_id(1)
    pad = SUBLANES
    tn = o_ref.shape[1]

    @pl.when(i == 0)
    def _():
        wb_ref[:, 0:tn] = wg_ref[...].astype(BF16)
        wb_ref[:, tn:2 * tn] = wv_ref[...].astype(BF16)

    @pl.when((i * tm) % seq == 0)
    def _():
        tail_ref[...] = jnp.zeros((pad, 2 * tn), F32)

    tail = tail_ref[...]
    for r0 in range(0, tm, sub):
        u = _dot(h_ref[r0:r0 + sub, :], wb_ref[...])
        acc = _causal_conv(tail, u, cw_ref, cb_ref[...])
        tail = u[sub - pad:sub, :]
        o_ref[r0:r0 + sub, :] = (_silu(acc[:, 0:tn]) * acc[:, tn:2 * tn]).astype(o_ref.dtype)
    tail_ref[...] = tail
    wnext_bf_ref[...] = wnext_ref[...].astype(BF16)


def _ffn_up(h2, w_up, conv_w, conv_b, w_next, seq, tm, sub, tn):
    m, d = h2.shape
    nj = FFN_DIM // tn
    ni = m // tm
    ride = pl.BlockSpec((w_next.shape[0] // (nj * ni), w_next.shape[1]), lambda j, i: (j * ni + i, 0))
    assert w_next.shape[0] % (nj * ni * 2 * SUBLANES) == 0
    pair = lambda a: jnp.concatenate([a[:, :FFN_DIM].reshape(-1, nj, tn), a[:, FFN_DIM:].reshape(-1, nj, tn)],
                                     axis=2).reshape(-1, 2 * FFN_DIM)
    return pl.pallas_call(
        functools.partial(_ffn_up_kernel, tm=tm, sub=sub, seq=seq),
        grid=(nj, ni),
        in_specs=[pl.BlockSpec((tm, d), lambda j, i: (i, 0)),
                  pl.BlockSpec((d, tn), lambda j, i: (0, j)),
                  pl.BlockSpec((d, tn), lambda j, i: (0, j + nj)),
                  pl.BlockSpec((FFN_CONV, 2 * tn), lambda j, i: (0, j)),
                  pl.BlockSpec((1, 2 * tn), lambda j, i: (0, j)),
                  ride],
        out_specs=[pl.BlockSpec((tm, tn), lambda j, i: (i, j)), ride],
        out_shape=[jax.ShapeDtypeStruct((m, FFN_DIM), BF16), jax.ShapeDtypeStruct(w_next.shape, BF16)],
        scratch_shapes=[pltpu.VMEM((d, 2 * tn), BF16), pltpu.VMEM((SUBLANES, 2 * tn), F32)],
        compiler_params=_params(("arbitrary", "arbitrary")),
    )(h2, w_up, w_up, pair(conv_w), pair(conv_b.reshape(1, -1)), w_next)


def _ffn_down_kernel(a_ref, w_ref, x1_ref, gf_ref, fg_ref, o_ref, *, sub):
    for r0 in range(0, a_ref.shape[0], sub):
        rows = slice(r0, r0 + sub)
        x2 = x1_ref[rows, :] + gf_ref[0] * _dot(a_ref[rows, :], w_ref[...])
        o_ref[rows, :] = _rms(x2) * fg_ref[...]


def _ffn_down(act, w_down_bf, x1, gate_f, final_g, seq, tm, sub):
    m, d = x1.shape
    per_b = seq // tm
    row = lambda width: pl.BlockSpec((tm, width), lambda i: (i, 0))
    return pl.pallas_call(
        functools.partial(_ffn_down_kernel, sub=sub),
        grid=(m // tm,),
        in_specs=[row(FFN_DIM), _resident((FFN_DIM, d)), row(d),
                  pl.BlockSpec((1, 1, d), lambda i: (i // per_b, 0, 0)),
                  pl.BlockSpec((1, d), lambda i: (0, 0))],
        out_specs=row(d),
        out_shape=jax.ShapeDtypeStruct((m, d), F32),
        compiler_params=_params(("parallel",)),
    )(act, w_down_bf, x1, gate_f, final_g.reshape(1, d))


def kernel(x, c, w_ada, b_ada, norm_mix_g, w_in, rel_bias, attn_norm_g, conv_ssm_w, conv_ssm_b, dt_bias,
           a_log, d_skip, ssm_norm_g, w_out, norm_ffn_g, w_up, conv_ffn_w, conv_ffn_b, w_down, final_norm_g):
    bsz, seq, d = x.shape
    m = bsz * seq
    assert w_ada.shape[0] == 1, "the final RMSNorm is fused into the (single) layer's ffn_down kernel"
    l = 0
    x2 = x.reshape(m, d)
    mod = _ada(c, w_ada[l], b_ada[l])
    shift_m, scale_m, gate_m, shift_f, scale_f, gate_f = [
        mod[:, k * d:(k + 1) * d].reshape(bsz, 1, d) for k in range(6)]

    w_in_t = w_in[l].T
    proj, w_out_bf = _normproj(x2, norm_mix_g[l], scale_m, shift_m, w_in_t.astype(BF16), PROJ_COLS, w_out[l],
                               seq, PROJ_TM, PROJ_SUB)
    proj3 = proj.reshape(bsz, seq, PROJ_COLS)

    attn = _moba(rel_bias, proj3)
    ssm = _ssd(proj3, conv_ssm_w[l], conv_ssm_b[l], dt_bias[l], a_log[l], d_skip[l], ssm_norm_g[l])

    x1, h2 = _outproj(attn.reshape(m, ATTN_WIDTH), ssm.reshape(m, SSM_WIDTH), x2, attn_norm_g[l],
                      w_out_bf, gate_m, norm_ffn_g[l], scale_f, shift_f, seq, OUT_TM, OUT_SUB)
    act, w_down_bf = _ffn_up(h2, w_up[l], conv_ffn_w[l], conv_ffn_b[l], w_down[l], seq, UP_TM, UP_SUB, UP_TN)
    out = _ffn_down(act, w_down_bf, x1, gate_f, final_norm_g, seq, DOWN_TM, DOWN_SUB)
    return out.reshape(bsz, seq, d)
```

```python
import functools
import math

import jax
import jax.numpy as jnp
from jax import lax
from jax.experimental import pallas as pl
from jax.experimental.pallas import tpu as pltpu

F32 = jnp.float32
BF16 = jnp.bfloat16

ATTN_WIDTH = 1024
HEAD_DIM = 128
N_HEADS = 8
SSM_WIDTH = 1024
SSM_HEAD_DIM = 64
N_SSM_HEADS = 16
N_GROUPS = 2
GROUP_WIDTH = SSM_WIDTH // N_GROUPS
D_STATE = 128
SSM_CONV = 4
CHUNK = 256
MOBA_BLOCK = 256
MOBA_TOPK = 3
MOBA_FAR_GROUP = 4
MOBA_UNROLL = 2
REL_BUCKETS = 32
REL_MAX_DIST = 128
FFN_DIM = 5632
FFN_CONV = 3
EPS = 1e-6
NEG = -1e30

LANES = 128
SUBLANES = 8
VMEM_LIMIT = 56 * 1024 * 1024

ADA_TN = 1024
PROJ_TM = 256
OUT_TM = 512
UP_TM, UP_TN = 1024, 512
DOWN_TM = 512

PROJ_COLS = 5760
COL_XS = 3 * ATTN_WIDTH
COL_Z = COL_XS + SSM_WIDTH
COL_BC = COL_Z + SSM_WIDTH
COL_DT = COL_BC + 2 * N_GROUPS * D_STATE


def _params(sem):
    return pltpu.CompilerParams(dimension_semantics=sem, vmem_limit_bytes=VMEM_LIMIT)


def _split3(x):
    hi = x.astype(BF16)
    r = x - hi.astype(F32)
    mid = r.astype(BF16)
    lo = (r - mid.astype(F32)).astype(BF16)
    return hi, mid, lo


NT = (((1,), (1,)), ((), ()))


def _dot(a, b, dims=(((1,), (0,)), ((), ()))):
    return lax.dot_general(a, b, dims, preferred_element_type=F32)


def _dot_exact_lhs(a_bf, x, dims=(((1,), (0,)), ((), ()))):
    hi, mid, lo = _split3(x)
    return _dot(a_bf, hi, dims) + _dot(a_bf, mid, dims) + _dot(a_bf, lo, dims)


def _dot_exact_rhs(x, b_bf, dims=(((1,), (0,)), ((), ()))):
    hi, mid, lo = _split3(x)
    return _dot(hi, b_bf, dims) + _dot(mid, b_bf, dims) + _dot(lo, b_bf, dims)


def _silu(x):
    return x * jax.nn.sigmoid(x)


def _rms(x):
    return x * lax.rsqrt(jnp.mean(x * x, axis=-1, keepdims=True) + EPS)


def _causal_conv(tail, x, w_ref, b):
    rows, width = x.shape
    taps = w_ref.shape[0]
    ext = jnp.concatenate([tail, x], axis=0).reshape(rows // SUBLANES + 1, SUBLANES, width)
    row_in_slab = lax.broadcasted_iota(jnp.int32, (rows // SUBLANES, SUBLANES, width), 1)
    acc = b + w_ref[taps - 1:taps, :] * x
    rot = ext
    for back in range(1, taps):
        rot = pltpu.roll(rot, 1, axis=1)
        shifted = jnp.where(row_in_slab < back, rot[:-1], rot[1:]).reshape(rows, width)
        acc = acc + w_ref[taps - 1 - back:taps - back, :] * shifted
    return acc


def _ada_kernel(ct_ref, w_ref, b_ref, o_ref, sb_ref):
    nb = sb_ref.shape[0]
    d = w_ref.shape[0]
    tn = o_ref.shape[-1]

    @pl.when(pl.program_id(0) == 0)
    def _():
        ct = ct_ref[...]
        st = _silu(ct)
        for b in range(nb):
            sb_ref[b] = jnp.broadcast_to(st[:, b:b + 1], (d, LANES))

    def body(kc, accs):
        r = pl.multiple_of(kc * SUBLANES, SUBLANES)
        w8 = w_ref[pl.ds(r, SUBLANES), :]
        out = []
        for b in range(nb):
            s8 = sb_ref[b, pl.ds(r, SUBLANES), :]
            out.append(accs[b] + w8 * jnp.tile(s8, (1, tn // LANES)))
        return tuple(out)

    accs = lax.fori_loop(0, d // SUBLANES, body,
                         tuple(jnp.zeros((SUBLANES, tn), F32) for _ in range(nb)), unroll=8)
    for b in range(nb):
        o_ref[b:b + 1, :] = jnp.sum(accs[b], axis=0, keepdims=True) + b_ref[...]


def _ada(c, w_ada, b_ada, tn=ADA_TN):
    nb, d = c.shape
    n = w_ada.shape[1]
    return pl.pallas_call(
        _ada_kernel,
        grid=(n // tn,),
        in_specs=[pl.BlockSpec((d, nb), lambda j: (0, 0)),
                  pl.BlockSpec((d, tn), lambda j: (0, j)),
                  pl.BlockSpec((1, tn), lambda j: (0, j))],
        out_specs=pl.BlockSpec((nb, tn), lambda j: (0, j)),
        out_shape=jax.ShapeDtypeStruct((nb, n), F32),
        scratch_shapes=[pltpu.VMEM((nb, d, LANES), F32)],
        compiler_params=_params(("arbitrary",)),
    )(c.T, w_ada, b_ada.reshape(1, n))


def _normproj_kernel(x_ref, g_ref, sc_ref, sh_ref, w_ref, wnext_ref, o_ref, wnext_bf_ref):
    tm, n_out = o_ref.shape
    n = w_ref.shape[0]
    y = _rms(x_ref[...]) * g_ref[...]
    h = y * (1.0 + sc_ref[0]) + sh_ref[0]
    o_ref[:, 0:n] = _dot(h.astype(BF16), w_ref[...], NT)
    if n < n_out:
        o_ref[:, n:] = jnp.zeros((tm, n_out - n), F32)
    wnext_bf_ref[...] = wnext_ref[...].astype(BF16)


def _resident(shape):
    return pl.BlockSpec(shape, lambda *_: (0,) * len(shape), pipeline_mode=pl.Buffered(1))


def _normproj(x2, g, scale, shift, w_t, n_out, w_next, seq, tm):
    m, d = x2.shape
    per_b = seq // tm
    steps = m // tm
    assert w_next.shape[0] % (steps * 2 * SUBLANES) == 0
    ride = pl.BlockSpec((w_next.shape[0] // steps, w_next.shape[1]), lambda i: (i, 0))
    vec = pl.BlockSpec((1, 1, d), lambda i: (i // per_b, 0, 0))
    return pl.pallas_call(
        _normproj_kernel,
        grid=(m // tm,),
        in_specs=[pl.BlockSpec((tm, d), lambda i: (i, 0)),
                  pl.BlockSpec((1, d), lambda i: (0, 0)),
                  vec, vec, _resident(w_t.shape), ride],
        out_specs=[pl.BlockSpec((tm, n_out), lambda i: (i, 0)), ride],
        out_shape=[jax.ShapeDtypeStruct((m, n_out), F32), jax.ShapeDtypeStruct(w_next.shape, BF16)],
        compiler_params=_params(("parallel",)),
    )(x2, g.reshape(1, d), scale, shift, w_t, w_next)


def _rel_bucket(dist):
    n = jnp.maximum(dist, 0)
    max_exact = REL_BUCKETS // 2
    nf = jnp.maximum(n, max_exact).astype(F32)
    large = max_exact + (jnp.log(nf / max_exact) / math.log(REL_MAX_DIST / max_exact)
                         * (REL_BUCKETS - max_exact)).astype(jnp.int32)
    large = jnp.minimum(large, REL_BUCKETS - 1)
    return jnp.where(n < max_exact, n, large)


def _moba_units(nblk, grp):
    lead = grp - 1
    units = []
    for qi in range(nblk):
        units.append((qi, qi, 1))
        for t in range(-(-max(qi - lead, 0) // grp)):
            units.append((qi, lead + t * grp, 0))
    return units


def _moba_kernel(uq_ref, uk_ref, uf_ref, rb_ref, q_ref, k_ref, v_ref, o_ref,
                 qbf_ref, kbf_ref, vt_ref, tab_ref, mask_ref, s_buf, top_buf, p_buf, alpha_buf, m_ref, l_ref, acc_ref,
                 *, units):
    h = pl.program_id(0)
    seq = k_ref.shape[1]
    blk = MOBA_BLOCK
    grp = MOBA_FAR_GROUP
    lead = grp - 1
    nblk = seq // blk
    nunits = len(units)
    scale = HEAD_DIM ** -0.5
    exp2_scale = scale * math.log2(math.e)

    kf = k_ref[0]
    qf = q_ref[0]
    qbf_ref[...] = (qf * exp2_scale).astype(BF16)
    kbf_ref[0:lead * blk, :] = jnp.zeros((lead * blk, HEAD_DIM), BF16)
    kbf_ref[lead * blk:, :] = kf.astype(BF16)
    kmean = jnp.mean(kf.reshape(nblk, blk, HEAD_DIM), axis=1)
    for j in range(lead):
        vt_ref[j] = jnp.zeros((HEAD_DIM, blk), BF16)
    for j in range(nblk):
        vt_ref[lead + j] = v_ref[0, j * blk:(j + 1) * blk, :].T.astype(BF16)

    @pl.when(pl.program_id(1) == 0)
    def _():
        b_far = rb_ref[REL_BUCKETS - 1, h]
        kk = lax.broadcasted_iota(jnp.int32, (blk, blk), 0)
        qq = lax.broadcasted_iota(jnp.int32, (blk, blk), 1)
        tab_ref[0] = jnp.zeros((2 * blk, blk), F32)
        for pos, dist in ((0, qq - kk + blk), (1, qq - kk)):
            bucket = _rel_bucket(dist)
            tab = jnp.zeros((blk, blk), F32)
            for b in range(REL_BUCKETS):
                tab = jnp.where(bucket == b, (rb_ref[b, h] - b_far) * math.log2(math.e), tab)
            tab_ref[1, pos * blk:(pos + 1) * blk, :] = jnp.where(dist >= 0, tab, NEG)

    q_hi, q_mid, _ = _split3(qf)
    k_hi, k_mid, _ = _split3(kmean)
    gate = _dot(k_hi, q_hi, NT) + _dot(k_hi, q_mid, NT) + _dot(k_mid, q_hi, NT)
    nidx = lax.broadcasted_iota(jnp.int32, (nblk, seq), 0)
    qblk = lax.broadcasted_iota(jnp.int32, (nblk, seq), 1) // blk
    nidx_f = nidx.astype(F32)
    avail = jnp.where(nidx < qblk, 1.0, 0.0)
    chosen = jnp.zeros((nblk, seq), F32)
    for _ in range(MOBA_TOPK):
        gm = jnp.where(avail > 0.0, gate, -jnp.inf)
        best = jnp.max(gm, axis=0, keepdims=True)
        first = jnp.min(jnp.where((gm == best) & (avail > 0.0), nidx_f, float(nblk)), axis=0, keepdims=True)
        hit = nidx_f == first
        chosen = jnp.where(hit, 1.0, chosen)
        avail = jnp.where(hit, 0.0, avail)
    add_all = jnp.where(chosen > 0.0, 0.0, NEG)
    for u, (qi, ks, is_first) in enumerate(units):
        cols = slice(qi * blk, (qi + 1) * blk)
        for n in range(grp):
            j = ks + n - lead
            if j == qi and is_first:
                row = jnp.zeros((1, blk), F32)
            elif 0 <= j < (qi if is_first else qi - lead):
                row = add_all[j:j + 1, cols]
            else:
                row = jnp.full((1, blk), NEG, F32)
            mask_ref[u * grp + n] = jnp.broadcast_to(row, (SUBLANES, blk))

    def unit(u):
        uc = jnp.clip(u, 0, nunits - 1)
        return uq_ref[uc], uk_ref[uc], uf_ref[uc], uc

    def scores(slot, u):
        qi, ks, is_first, uc = unit(u)
        qb = qbf_ref[pl.ds(pl.multiple_of(qi * blk, blk), blk), :]
        kg = kbf_ref[pl.ds(pl.multiple_of(ks * blk, blk), grp * blk), :]
        s = _dot(kg, qb, NT)
        tab = tab_ref[is_first]
        parts, top = [], None
        for n in range(grp):
            part = s[n * blk:(n + 1) * blk]
            if n >= grp - 2:
                part = part + tab[(n - grp + 2) * blk:(n - grp + 3) * blk]
            parts.append(part)
            best = jnp.max(part, axis=0, keepdims=True) + mask_ref[uc * grp + n][0:1, :]
            top = best if top is None else jnp.maximum(top, best)
        s_buf[slot] = jnp.concatenate(parts, axis=0)
        top_buf[slot] = top

    def softmax(slot, u):
        qi, _, is_first, uc = unit(u)
        st = qi % 2
        m_old = jnp.where(is_first == 1, NEG, m_ref[st])
        l_old = jnp.where(is_first == 1, 0.0, l_ref[st])
        m_new = jnp.maximum(m_old, top_buf[slot])
        alpha = jnp.exp2(m_old - m_new)
        total = alpha * l_old
        for n in range(grp):
            offset = m_new - mask_ref[uc * grp + n][0:1, :]
            p = jnp.exp2(s_buf[slot, n * blk:(n + 1) * blk, :] - offset)
            total = total + jnp.sum(p, axis=0, keepdims=True)
            p_buf[slot, n * blk:(n + 1) * blk, :] = p.astype(BF16)
        m_ref[st] = m_new
        l_ref[st] = total
        alpha_buf[slot] = alpha

    def values(slot, u):
        qi, ks, _, _ = unit(u)
        st = qi % 2
        acc = alpha_buf[slot] * acc_ref[st]
        for n in range(grp):
            acc = acc + _dot(vt_ref[ks + n], p_buf[slot, n * blk:(n + 1) * blk, :])
        acc_ref[st] = acc
        o_ref[0, pl.ds(pl.multiple_of(qi * blk, blk), blk), :] = (acc * (1.0 / l_ref[st])).T

    m_ref[...] = jnp.full(m_ref.shape, NEG, F32)
    l_ref[...] = jnp.ones(l_ref.shape, F32)
    acc_ref[...] = jnp.zeros(acc_ref.shape, F32)
    alpha_buf[1] = jnp.ones((1, blk), F32)
    p_buf[1] = jnp.zeros((grp * blk, blk), BF16)
    scores(0, 0)

    def trip(u, slot):
        values(1 - slot, u - 1)
        softmax(slot, u)
        scores(1 - slot, u + 1)

    def trips(it, carry):
        for n in range(MOBA_UNROLL):
            trip(MOBA_UNROLL * it + n, n % 2)
        return carry

    lax.fori_loop(0, nunits // MOBA_UNROLL, trips, 0)
    for u in range(nunits - nunits % MOBA_UNROLL, nunits):
        trip(u, u % 2)
    values((nunits - 1) % 2, nunits - 1)


def _moba(rel_bias, proj3):
    bsz, seq, _ = proj3.shape
    blk = MOBA_BLOCK
    grp = MOBA_FAR_GROUP
    nblk = seq // blk
    assert seq % blk == 0 and grp >= 2
    units = _moba_units(nblk, grp)
    uq, uk, uf = (jnp.asarray([u[c] for u in units], jnp.int32) for c in range(3))
    smem = pl.BlockSpec(memory_space=pltpu.SMEM)
    head = lambda col0: pl.BlockSpec((1, seq, HEAD_DIM), lambda h, b: (b, 0, col0 + h))
    return pl.pallas_call(
        functools.partial(_moba_kernel, units=units),
        grid=(N_HEADS, bsz),
        in_specs=[smem, smem, smem, smem, head(0), head(N_HEADS), head(2 * N_HEADS)],
        out_specs=head(0),
        out_shape=jax.ShapeDtypeStruct((bsz, seq, ATTN_WIDTH), F32),
        scratch_shapes=[pltpu.VMEM((seq, HEAD_DIM), BF16),
                        pltpu.VMEM((seq + (grp - 1) * blk, HEAD_DIM), BF16),
                        pltpu.VMEM((nblk + grp - 1, HEAD_DIM, blk), BF16),
                        pltpu.VMEM((2, 2 * blk, blk), F32),
                        pltpu.VMEM((len(units) * grp, SUBLANES, blk), F32),
                        pltpu.VMEM((2, grp * blk, blk), F32),
                        pltpu.VMEM((2, 1, blk), F32),
                        pltpu.VMEM((2, grp * blk, blk), BF16),
                        pltpu.VMEM((2, 1, blk), F32),
                        pltpu.VMEM((2, 1, blk), F32),
                        pltpu.VMEM((2, 1, blk), F32),
                        pltpu.VMEM((2, HEAD_DIM, blk), F32)],
        compiler_params=_params(("arbitrary", "arbitrary")),
    )(uq, uk, uf, rel_bias, proj3, proj3, proj3)


def _ssd_kernel(xs_ref, z_ref, bc_ref, dt_ref, cwx_ref, cwbc_ref, cbx_ref, cbbc_ref,
                dtb_ref, alog_ref, dsk_ref, ng_ref, tri_ref, trit_ref, exp_ref, o_ref,
                xtail_ref, bctail_ref, state_ref):
    c = pl.program_id(1)
    t = CHUNK
    pad = SUBLANES

    @pl.when(c == 0)
    def _():
        xtail_ref[...] = jnp.zeros_like(xtail_ref)
        bctail_ref[...] = jnp.zeros_like(bctail_ref)
        state_ref[...] = jnp.zeros_like(state_ref)

    def conv_silu(tail_ref, src_ref, w_ref, b_ref):
        x = src_ref[0]
        acc = _causal_conv(tail_ref[...], x, w_ref, b_ref[...])
        tail_ref[...] = x[t - pad:t, :]
        return _silu(acc)

    xh = conv_silu(xtail_ref, xs_ref, cwx_ref, cbx_ref)
    bc = conv_silu(bctail_ref, bc_ref, cwbc_ref, cbbc_ref)

    dtr = dt_ref[0] + dtb_ref[...]
    dt = jnp.maximum(dtr, 0.0) + jnp.log1p(jnp.exp(-jnp.abs(dtr)))
    adt = dt * (-jnp.exp(alog_ref[...]))
    acs = _dot_exact_lhs(tri_ref[...], adt)
    acs_t = _dot_exact_rhs(adt.T, trit_ref[...])

    stack = jnp.concatenate([dt, jnp.exp(acs), jnp.exp(acs[t - 1:t, :] - acs)], axis=0)
    wide = _dot_exact_rhs(stack, exp_ref[...])
    dt_x, eacs_x, dst_x = wide[0:t], wide[t:2 * t], wide[2 * t:3 * t]

    xdt = xh * dt_x
    xdt_bf = xdt.astype(BF16)
    xdec_bf = (xdt * dst_x).astype(BF16)

    row = lax.broadcasted_iota(jnp.int32, (t, t), 0)
    col = lax.broadcasted_iota(jnp.int32, (t, t), 1)
    tril = row >= col
    lane = lax.broadcasted_iota(jnp.int32, (t, LANES), 1)
    heads_per_group = N_SSM_HEADS // N_GROUPS
    nt = (((1,), (1,)), ((), ()))
    tn = (((0,), (0,)), ((), ()))

    y_parts = []
    for g in range(N_GROUPS):
        bg = bc[:, g * D_STATE:(g + 1) * D_STATE].astype(BF16)
        cg = bc[:, (N_GROUPS + g) * D_STATE:(N_GROUPS + g + 1) * D_STATE].astype(BF16)
        cb = _dot(cg, bg, nt)
        for pair in range(heads_per_group // 2):
            ms = []
            for r in (g * heads_per_group + 2 * pair, g * heads_per_group + 2 * pair + 1):
                seg = acs[:, r:r + 1] - acs_t[r:r + 1, :]
                ms.append((cb * jnp.exp(jnp.where(tril, seg, NEG))).astype(BF16))
            q = g * (heads_per_group // 2) + pair
            y2 = _dot(jnp.concatenate(ms, axis=0), xdt_bf[:, q * LANES:(q + 1) * LANES])
            y_parts.append(jnp.where(lane < SSM_HEAD_DIM, y2[0:t], y2[t:2 * t]))
    y = jnp.concatenate(y_parts, axis=1)

    off_parts = []
    for g in range(N_GROUPS):
        sl = slice(g * GROUP_WIDTH, (g + 1) * GROUP_WIDTH)
        bg = bc[:, g * D_STATE:(g + 1) * D_STATE].astype(BF16)
        cg = bc[:, (N_GROUPS + g) * D_STATE:(N_GROUPS + g + 1) * D_STATE].astype(BF16)
        st = state_ref[:, sl]
        off_parts.append(_dot(cg, st.astype(BF16)))
        state_ref[:, sl] = eacs_x[t - 1:t, sl] * st + _dot(bg, xdec_bf[:, sl], tn)
    y = y + jnp.concatenate(off_parts, axis=1) * eacs_x + dsk_ref[...] * xh
    y = y * _silu(z_ref[0])

    outs = []
    for g in range(N_GROUPS):
        sl = slice(g * GROUP_WIDTH, (g + 1) * GROUP_WIDTH)
        outs.append(_rms(y[:, sl]) * ng_ref[:, sl])
    o_ref[0] = jnp.concatenate(outs, axis=1).astype(o_ref.dtype)


def _ssd(proj3, conv_w, conv_b, dt_bias, a_log, d_skip, norm_g):
    bsz, seq, _ = proj3.shape
    t = CHUNK
    gn2 = 2 * N_GROUPS * D_STATE
    pad_h = LANES - N_SSM_HEADS
    tri = jnp.tril(jnp.ones((t, t), F32)).astype(BF16)
    expand = jnp.pad(jnp.repeat(jnp.eye(N_SSM_HEADS, dtype=F32), SSM_HEAD_DIM, axis=1),
                     ((0, pad_h), (0, 0))).astype(BF16)
    const = lambda shape: pl.BlockSpec(shape, lambda b, c: (0,) * len(shape))
    return pl.pallas_call(
        _ssd_kernel,
        grid=(bsz, seq // t),
        in_specs=[pl.BlockSpec((1, t, SSM_WIDTH), lambda b, c: (b, c, COL_XS // SSM_WIDTH)),
                  pl.BlockSpec((1, t, SSM_WIDTH), lambda b, c: (b, c, COL_Z // SSM_WIDTH)),
                  pl.BlockSpec((1, t, gn2), lambda b, c: (b, c, COL_BC // gn2)),
                  pl.BlockSpec((1, t, LANES), lambda b, c: (b, c, COL_DT // LANES)),
                  const((SSM_CONV, SSM_WIDTH)), const((SSM_CONV, gn2)),
                  const((1, SSM_WIDTH)), const((1, gn2)),
                  const((1, LANES)), const((1, LANES)),
                  const((1, SSM_WIDTH)), const((1, SSM_WIDTH)),
                  const((t, t)), const((t, t)), const((LANES, SSM_WIDTH))],
        out_specs=pl.BlockSpec((1, t, SSM_WIDTH), lambda b, c: (b, c, 0)),
        out_shape=jax.ShapeDtypeStruct((bsz, seq, SSM_WIDTH), BF16),
        scratch_shapes=[pltpu.VMEM((SUBLANES, SSM_WIDTH), F32),
                        pltpu.VMEM((SUBLANES, gn2), F32),
                        pltpu.VMEM((D_STATE, SSM_WIDTH), F32)],
        compiler_params=_params(("arbitrary", "arbitrary")),
    )(proj3, proj3, proj3, proj3,
      conv_w[:, :SSM_WIDTH], conv_w[:, SSM_WIDTH:],
      conv_b[:SSM_WIDTH].reshape(1, -1), conv_b[SSM_WIDTH:].reshape(1, -1),
      jnp.pad(dt_bias, (0, pad_h)).reshape(1, LANES), jnp.pad(a_log, (0, pad_h)).reshape(1, LANES),
      jnp.repeat(d_skip, SSM_HEAD_DIM).reshape(1, SSM_WIDTH), norm_g.reshape(1, SSM_WIDTH),
      tri, tri.T, expand)


def _outproj_kernel(attn_ref, ssm_ref, x_ref, ag_ref, w_ref, gm_ref, g2_ref, sc_ref, sh_ref,
                    x1_ref, h2_ref):
    a = _rms(attn_ref[...]) * ag_ref[...]
    lhs = jnp.concatenate([a.astype(BF16), ssm_ref[...]], axis=-1)
    x1 = x_ref[...] + gm_ref[0] * _dot(lhs, w_ref[...])
    x1_ref[...] = x1
    y = _rms(x1) * g2_ref[...]
    h2_ref[...] = (y * (1.0 + sc_ref[0]) + sh_ref[0]).astype(BF16)


def _outproj(attn2, ssm2, x2, attn_g, w_out_bf, gate_m, g2, scale_f, shift_f, seq, tm):
    m, d = x2.shape
    per_b = seq // tm
    vec = pl.BlockSpec((1, 1, d), lambda i: (i // per_b, 0, 0))
    half = pl.BlockSpec((tm, ATTN_WIDTH), lambda i: (i, 0))
    full = pl.BlockSpec((tm, d), lambda i: (i, 0))
    return pl.pallas_call(
        _outproj_kernel,
        grid=(m // tm,),
        in_specs=[half, half, full,
                  pl.BlockSpec((1, ATTN_WIDTH), lambda i: (0, 0)),
                  _resident((d, d)),
                  vec, pl.BlockSpec((1, d), lambda i: (0, 0)), vec, vec],
        out_specs=[full, full],
        out_shape=[jax.ShapeDtypeStruct((m, d), F32), jax.ShapeDtypeStruct((m, d), BF16)],
        compiler_params=_params(("parallel",)),
    )(attn2, ssm2, x2, attn_g.reshape(1, -1), w_out_bf, gate_m, g2.reshape(1, d), scale_f, shift_f)


def _ffn_up_kernel(h_ref, wg_ref, wv_ref, cw_ref, cb_ref, wnext_ref, o_ref, wnext_bf_ref, wb_ref, tail_ref,
                   *, tm, seq):
    i = pl.program_id(1)
    pad = SUBLANES
    tn = o_ref.shape[1]

    @pl.when(i == 0)
    def _():
        wb_ref[:, 0:tn] = wg_ref[...].astype(BF16)
        wb_ref[:, tn:2 * tn] = wv_ref[...].astype(BF16)

    @pl.when((i * tm) % seq == 0)
    def _():
        tail_ref[...] = jnp.zeros((pad, 2 * tn), F32)

    u = _dot(h_ref[...], wb_ref[...])
    acc = _causal_conv(tail_ref[...], u, cw_ref, cb_ref[...])
    tail_ref[...] = u[tm - pad:tm, :]
    o_ref[...] = (_silu(acc[:, 0:tn]) * acc[:, tn:2 * tn]).astype(o_ref.dtype)
    wnext_bf_ref[...] = wnext_ref[...].astype(BF16)


def _ffn_up(h2, w_up, conv_w, conv_b, w_next, seq, tm, tn):
    m, d = h2.shape
    nj = FFN_DIM // tn
    ni = m // tm
    ride = pl.BlockSpec((w_next.shape[0] // (nj * ni), w_next.shape[1]), lambda j, i: (j * ni + i, 0))
    assert w_next.shape[0] % (nj * ni * 2 * SUBLANES) == 0
    pair = lambda a: jnp.concatenate([a[:, :FFN_DIM].reshape(-1, nj, tn), a[:, FFN_DIM:].reshape(-1, nj, tn)],
                                     axis=2).reshape(-1, 2 * FFN_DIM)
    return pl.pallas_call(
        functools.partial(_ffn_up_kernel, tm=tm, seq=seq),
        grid=(nj, ni),
        in_specs=[pl.BlockSpec((tm, d), lambda j, i: (i, 0)),
                  pl.BlockSpec((d, tn), lambda j, i: (0, j)),
                  pl.BlockSpec((d, tn), lambda j, i: (0, j + nj)),
                  pl.BlockSpec((FFN_CONV, 2 * tn), lambda j, i: (0, j)),
                  pl.BlockSpec((1, 2 * tn), lambda j, i: (0, j)),
                  ride],
        out_specs=[pl.BlockSpec((tm, tn), lambda j, i: (i, j)), ride],
        out_shape=[jax.ShapeDtypeStruct((m, FFN_DIM), BF16), jax.ShapeDtypeStruct(w_next.shape, BF16)],
        scratch_shapes=[pltpu.VMEM((d, 2 * tn), BF16), pltpu.VMEM((SUBLANES, 2 * tn), F32)],
        compiler_params=_params(("arbitrary", "arbitrary")),
    )(h2, w_up, w_up, pair(conv_w), pair(conv_b.reshape(1, -1)), w_next)


def _ffn_down_kernel(a_ref, w_ref, x1_ref, gf_ref, fg_ref, o_ref):
    x2 = x1_ref[...] + gf_ref[0] * _dot(a_ref[...], w_ref[...])
    o_ref[...] = _rms(x2) * fg_ref[...]


def _ffn_down(act, w_down_bf, x1, gate_f, final_g, seq, tm):
    m, d = x1.shape
    per_b = seq // tm
    row = lambda width: pl.BlockSpec((tm, width), lambda i: (i, 0))
    return pl.pallas_call(
        _ffn_down_kernel,
        grid=(m // tm,),
        in_specs=[row(FFN_DIM), _resident((FFN_DIM, d)), row(d),
                  pl.BlockSpec((1, 1, d), lambda i: (i // per_b, 0, 0)),
                  pl.BlockSpec((1, d), lambda i: (0, 0))],
        out_specs=row(d),
        out_shape=jax.ShapeDtypeStruct((m, d), F32),
        compiler_params=_params(("parallel",)),
    )(act, w_down_bf, x1, gate_f, final_g.reshape(1, d))


def kernel(x, c, w_ada, b_ada, norm_mix_g, w_in, rel_bias, attn_norm_g, conv_ssm_w, conv_ssm_b, dt_bias,
           a_log, d_skip, ssm_norm_g, w_out, norm_ffn_g, w_up, conv_ffn_w, conv_ffn_b, w_down, final_norm_g):
    bsz, seq, d = x.shape
    m = bsz * seq
    assert w_ada.shape[0] == 1, "the final RMSNorm is fused into the (single) layer's ffn_down kernel"
    l = 0
    x2 = x.reshape(m, d)
    mod = _ada(c, w_ada[l], b_ada[l])
    shift_m, scale_m, gate_m, shift_f, scale_f, gate_f = [
        mod[:, k * d:(k + 1) * d].reshape(bsz, 1, d) for k in range(6)]

    w_in_t = w_in[l].T
    proj, w_out_bf = _normproj(x2, norm_mix_g[l], scale_m, shift_m, w_in_t.astype(BF16), PROJ_COLS, w_out[l],
                               seq, PROJ_TM)
    proj3 = proj.reshape(bsz, seq, PROJ_COLS)

    attn = _moba(rel_bias, proj3)
    ssm = _ssd(proj3, conv_ssm_w[l], conv_ssm_b[l], dt_bias[l], a_log[l], d_skip[l], ssm_norm_g[l])

    x1, h2 = _outproj(attn.reshape(m, ATTN_WIDTH), ssm.reshape(m, SSM_WIDTH), x2, attn_norm_g[l],
                      w_out_bf, gate_m, norm_ffn_g[l], scale_f, shift_f, seq, OUT_TM)
    act, w_down_bf = _ffn_up(h2, w_up[l], conv_ffn_w[l], conv_ffn_b[l], w_down[l], seq, UP_TM, UP_TN)
    out = _ffn_down(act, w_down_bf, x1, gate_f, final_norm_g, seq, DOWN_TM)
    return out.reshape(bsz, seq, d)
```

```python
import functools
import math

import jax
import jax.numpy as jnp
from jax import lax
from jax.experimental import pallas as pl
from jax.experimental.pallas import tpu as pltpu

F32 = jnp.float32
BF16 = jnp.bfloat16

ATTN_WIDTH = 1024
HEAD_DIM = 128
N_HEADS = 8
SSM_WIDTH = 1024
SSM_HEAD_DIM = 64
N_SSM_HEADS = 16
N_GROUPS = 2
GROUP_WIDTH = SSM_WIDTH // N_GROUPS
D_STATE = 128
SSM_CONV = 4
CHUNK = 256
MOBA_BLOCK = 256
MOBA_TOPK = 3
MOBA_UNIT_BLOCKS = 4
MOBA_UNROLL = 2
REL_BUCKETS = 32
REL_MAX_DIST = 128
FFN_DIM = 5632
FFN_CONV = 3
EPS = 1e-6
NEG = -1e30

LANES = 128
SUBLANES = 8
VMEM_LIMIT = 56 * 1024 * 1024

ADA_TN = 1024
PROJ_TM = 256
OUT_TM = 512
UP_TM, UP_TN = 1024, 512
DOWN_TM = 512

PROJ_COLS = 5760
COL_XS = 3 * ATTN_WIDTH
COL_Z = COL_XS + SSM_WIDTH
COL_BC = COL_Z + SSM_WIDTH
COL_DT = COL_BC + 2 * N_GROUPS * D_STATE


def _params(sem):
    return pltpu.CompilerParams(dimension_semantics=sem, vmem_limit_bytes=VMEM_LIMIT)


def _split3(x):
    hi = x.astype(BF16)
    r = x - hi.astype(F32)
    mid = r.astype(BF16)
    lo = (r - mid.astype(F32)).astype(BF16)
    return hi, mid, lo


NT = (((1,), (1,)), ((), ()))


def _dot(a, b, dims=(((1,), (0,)), ((), ()))):
    return lax.dot_general(a, b, dims, preferred_element_type=F32)


def _dot_exact_lhs(a_bf, x):
    n = x.shape[1]
    r = _dot(a_bf, jnp.concatenate(_split3(x), axis=1))
    return r[:, 0:n] + r[:, n:2 * n] + r[:, 2 * n:3 * n]


def _dot_exact_rhs(x, b_bf):
    m = x.shape[0]
    r = _dot(jnp.concatenate(_split3(x), axis=0), b_bf)
    return r[0:m] + r[m:2 * m] + r[2 * m:3 * m]


def _silu(x):
    return x * jax.nn.sigmoid(x)


def _rms(x):
    return x * lax.rsqrt(jnp.mean(x * x, axis=-1, keepdims=True) + EPS)


def _causal_conv(tail, x, w_ref, b):
    rows, width = x.shape
    taps = w_ref.shape[0]
    ext = jnp.concatenate([tail, x], axis=0).reshape(rows // SUBLANES + 1, SUBLANES, width)
    row_in_slab = lax.broadcasted_iota(jnp.int32, (rows // SUBLANES, SUBLANES, width), 1)
    acc = b + w_ref[taps - 1:taps, :] * x
    rot = ext
    for back in range(1, taps):
        rot = pltpu.roll(rot, 1, axis=1)
        shifted = jnp.where(row_in_slab < back, rot[:-1], rot[1:]).reshape(rows, width)
        acc = acc + w_ref[taps - 1 - back:taps - back, :] * shifted
    return acc


def _ada_kernel(ct_ref, w_ref, b_ref, o_ref, sb_ref):
    nb = sb_ref.shape[0]
    d = w_ref.shape[0]
    tn = o_ref.shape[-1]

    @pl.when(pl.program_id(0) == 0)
    def _():
        ct = ct_ref[...]
        st = _silu(ct)
        for b in range(nb):
            sb_ref[b] = jnp.broadcast_to(st[:, b:b + 1], (d, LANES))

    def body(kc, accs):
        r = pl.multiple_of(kc * SUBLANES, SUBLANES)
        w8 = w_ref[pl.ds(r, SUBLANES), :]
        out = []
        for b in range(nb):
            s8 = sb_ref[b, pl.ds(r, SUBLANES), :]
            out.append(accs[b] + w8 * jnp.tile(s8, (1, tn // LANES)))
        return tuple(out)

    accs = lax.fori_loop(0, d // SUBLANES, body,
                         tuple(jnp.zeros((SUBLANES, tn), F32) for _ in range(nb)), unroll=8)
    for b in range(nb):
        o_ref[b:b + 1, :] = jnp.sum(accs[b], axis=0, keepdims=True) + b_ref[...]


def _ada(c, w_ada, b_ada, tn=ADA_TN):
    nb, d = c.shape
    n = w_ada.shape[1]
    return pl.pallas_call(
        _ada_kernel,
        grid=(n // tn,),
        in_specs=[pl.BlockSpec((d, nb), lambda j: (0, 0)),
                  pl.BlockSpec((d, tn), lambda j: (0, j)),
                  pl.BlockSpec((1, tn), lambda j: (0, j))],
        out_specs=pl.BlockSpec((nb, tn), lambda j: (0, j)),
        out_shape=jax.ShapeDtypeStruct((nb, n), F32),
        scratch_shapes=[pltpu.VMEM((nb, d, LANES), F32)],
        compiler_params=_params(("arbitrary",)),
    )(c.T, w_ada, b_ada.reshape(1, n))


def _normproj_kernel(x_ref, g_ref, sc_ref, sh_ref, w_ref, wnext_ref, o_ref, wnext_bf_ref):
    tm, n_out = o_ref.shape
    n = w_ref.shape[0]
    h = _rms(x_ref[...]) * (g_ref[...] * (1.0 + sc_ref[0])) + sh_ref[0]
    o_ref[:, 0:n] = _dot(h.astype(BF16), w_ref[...], NT)
    if n < n_out:
        o_ref[:, n:] = jnp.zeros((tm, n_out - n), F32)
    wnext_bf_ref[...] = wnext_ref[...].astype(BF16)


def _resident(shape):
    return pl.BlockSpec(shape, lambda *_: (0,) * len(shape), pipeline_mode=pl.Buffered(1))


def _normproj(x2, g, scale, shift, w_t, n_out, w_next, seq, tm):
    m, d = x2.shape
    per_b = seq // tm
    steps = m // tm
    assert w_next.shape[0] % (steps * 2 * SUBLANES) == 0
    ride = pl.BlockSpec((w_next.shape[0] // steps, w_next.shape[1]), lambda i: (i, 0))
    vec = pl.BlockSpec((1, 1, d), lambda i: (i // per_b, 0, 0))
    return pl.pallas_call(
        _normproj_kernel,
        grid=(m // tm,),
        in_specs=[pl.BlockSpec((tm, d), lambda i: (i, 0)),
                  pl.BlockSpec((1, d), lambda i: (0, 0)),
                  vec, vec, _resident(w_t.shape), ride],
        out_specs=[pl.BlockSpec((tm, n_out), lambda i: (i, 0)), ride],
        out_shape=[jax.ShapeDtypeStruct((m, n_out), F32), jax.ShapeDtypeStruct(w_next.shape, BF16)],
        compiler_params=_params(("parallel",)),
    )(x2, g.reshape(1, d), scale, shift, w_t, w_next)


def _rel_bucket(dist):
    n = jnp.maximum(dist, 0)
    max_exact = REL_BUCKETS // 2
    nf = jnp.maximum(n, max_exact).astype(F32)
    large = max_exact + (jnp.log(nf / max_exact) / math.log(REL_MAX_DIST / max_exact)
                         * (REL_BUCKETS - max_exact)).astype(jnp.int32)
    large = jnp.minimum(large, REL_BUCKETS - 1)
    return jnp.where(n < max_exact, n, large)


def _moba_units(nblk, grp):
    lead = grp - 1
    units = []
    for qi in range(nblk):
        units.append((qi, qi, 1))
        for t in range(-(-max(qi - lead, 0) // grp)):
            units.append((qi, lead + t * grp, 0))
    return units


def _moba_kernel(uq_ref, uk_ref, uf_ref, rb_ref, q_ref, k_ref, v_ref, o_ref,
                 qbf_ref, kbf_ref, vt_ref, tab_ref, mask_ref, s_buf, top_buf, p_buf, alpha_buf, m_ref, l_ref, acc_ref,
                 *, units):
    h = pl.program_id(0)
    seq = k_ref.shape[1]
    blk = MOBA_BLOCK
    grp = MOBA_UNIT_BLOCKS
    lead = grp - 1
    nblk = seq // blk
    nunits = len(units)
    scale = HEAD_DIM ** -0.5
    exp2_scale = scale * math.log2(math.e)

    kf = k_ref[0]
    qf = q_ref[0]
    qbf_ref[...] = (qf * exp2_scale).astype(BF16)
    kbf_ref[0:lead * blk, :] = jnp.zeros((lead * blk, HEAD_DIM), BF16)
    kbf_ref[lead * blk:, :] = kf.astype(BF16)
    kmean = jnp.mean(kf.reshape(nblk, blk, HEAD_DIM), axis=1)
    for j in range(lead):
        vt_ref[j] = jnp.zeros((HEAD_DIM, blk), BF16)
    for j in range(nblk):
        vt_ref[lead + j] = v_ref[0, j * blk:(j + 1) * blk, :].T.astype(BF16)

    @pl.when(pl.program_id(1) == 0)
    def _():
        b_far = rb_ref[REL_BUCKETS - 1, h]
        kk = lax.broadcasted_iota(jnp.int32, (blk, blk), 0)
        qq = lax.broadcasted_iota(jnp.int32, (blk, blk), 1)
        tab_ref[0] = jnp.zeros((2 * blk, blk), F32)
        for pos, dist in ((0, qq - kk + blk), (1, qq - kk)):
            bucket = _rel_bucket(dist)
            tab = jnp.zeros((blk, blk), F32)
            for b in range(REL_BUCKETS):
                tab = jnp.where(bucket == b, (rb_ref[b, h] - b_far) * math.log2(math.e), tab)
            tab_ref[1, pos * blk:(pos + 1) * blk, :] = jnp.where(dist >= 0, tab, NEG)

    q_hi, q_mid, _ = _split3(qf)
    k_hi, k_mid, _ = _split3(kmean)
    gate = _dot(k_hi, q_hi, NT) + _dot(k_hi, q_mid, NT) + _dot(k_mid, q_hi, NT)
    nidx = lax.broadcasted_iota(jnp.int32, (nblk, seq), 0)
    qblk = lax.broadcasted_iota(jnp.int32, (nblk, seq), 1) // blk
    nidx_f = nidx.astype(F32)
    avail = jnp.where(nidx < qblk, 1.0, 0.0)
    chosen = jnp.zeros((nblk, seq), F32)
    for _ in range(MOBA_TOPK):
        gm = jnp.where(avail > 0.0, gate, -jnp.inf)
        best = jnp.max(gm, axis=0, keepdims=True)
        first = jnp.min(jnp.where((gm == best) & (avail > 0.0), nidx_f, float(nblk)), axis=0, keepdims=True)
        hit = nidx_f == first
        chosen = jnp.where(hit, 1.0, chosen)
        avail = jnp.where(hit, 0.0, avail)
    add_all = jnp.where(chosen > 0.0, 0.0, NEG)
    for u, (qi, ks, is_first) in enumerate(units):
        cols = slice(qi * blk, (qi + 1) * blk)
        for n in range(grp):
            j = ks + n - lead
            if j == qi and is_first:
                row = jnp.zeros((1, blk), F32)
            elif 0 <= j < (qi if is_first else qi - lead):
                row = add_all[j:j + 1, cols]
            else:
                row = jnp.full((1, blk), NEG, F32)
            mask_ref[u * grp + n] = jnp.broadcast_to(row, (SUBLANES, blk))

    def unit(u):
        uc = jnp.clip(u, 0, nunits - 1)
        return uq_ref[uc], uk_ref[uc], uf_ref[uc], uc

    def scores(slot, u):
        qi, ks, is_first, uc = unit(u)
        qb = qbf_ref[pl.ds(pl.multiple_of(qi * blk, blk), blk), :]
        kg = kbf_ref[pl.ds(pl.multiple_of(ks * blk, blk), grp * blk), :]
        s = _dot(kg, qb, NT)
        tab = tab_ref[is_first]
        parts, top = [], None
        for n in range(grp):
            part = s[n * blk:(n + 1) * blk]
            if n >= grp - 2:
                part = part + tab[(n - grp + 2) * blk:(n - grp + 3) * blk]
            parts.append(part)
            best = jnp.max(part, axis=0, keepdims=True) + mask_ref[uc * grp + n][0:1, :]
            top = best if top is None else jnp.maximum(top, best)
        s_buf[slot] = jnp.concatenate(parts, axis=0)
        top_buf[slot] = top

    def softmax(slot, u):
        qi, _, is_first, uc = unit(u)
        st = qi % 2
        m_old = jnp.where(is_first == 1, NEG, m_ref[st])
        l_old = jnp.where(is_first == 1, 0.0, l_ref[st])
        m_new = jnp.maximum(m_old, top_buf[slot])
        alpha = jnp.exp2(m_old - m_new)
        total = alpha * l_old
        for n in range(grp):
            offset = m_new - mask_ref[uc * grp + n][0:1, :]
            p = jnp.exp2(s_buf[slot, n * blk:(n + 1) * blk, :] - offset)
            total = total + jnp.sum(p, axis=0, keepdims=True)
            p_buf[slot, n * blk:(n + 1) * blk, :] = p.astype(BF16)
        m_ref[st] = m_new
        l_ref[st] = total
        alpha_buf[slot] = alpha

    def values(slot, u):
        qi, ks, _, _ = unit(u)
        st = qi % 2
        acc = alpha_buf[slot] * acc_ref[st]
        for n in range(grp):
            acc = acc + _dot(vt_ref[ks + n], p_buf[slot, n * blk:(n + 1) * blk, :])
        acc_ref[st] = acc
        o_ref[0, pl.ds(pl.multiple_of(qi * blk, blk), blk), :] = (acc * (1.0 / l_ref[st])).T

    m_ref[...] = jnp.full(m_ref.shape, NEG, F32)
    l_ref[...] = jnp.ones(l_ref.shape, F32)
    acc_ref[...] = jnp.zeros(acc_ref.shape, F32)
    alpha_buf[1] = jnp.ones((1, blk), F32)
    p_buf[1] = jnp.zeros((grp * blk, blk), BF16)
    scores(0, 0)

    def trip(u, slot):
        values(1 - slot, u - 1)
        softmax(slot, u)
        scores(1 - slot, u + 1)

    def trips(it, carry):
        for n in range(MOBA_UNROLL):
            trip(MOBA_UNROLL * it + n, n % 2)
        return carry

    lax.fori_loop(0, nunits // MOBA_UNROLL, trips, 0)
    for u in range(nunits - nunits % MOBA_UNROLL, nunits):
        trip(u, u % 2)
    values((nunits - 1) % 2, nunits - 1)


def _moba(rel_bias, proj3):
    bsz, seq, _ = proj3.shape
    blk = MOBA_BLOCK
    grp = MOBA_UNIT_BLOCKS
    nblk = seq // blk
    assert seq % blk == 0 and grp >= 2
    units = _moba_units(nblk, grp)
    uq, uk, uf = (jnp.asarray([u[c] for u in units], jnp.int32) for c in range(3))
    smem = pl.BlockSpec(memory_space=pltpu.SMEM)
    head = lambda col0: pl.BlockSpec((1, seq, HEAD_DIM), lambda h, b: (b, 0, col0 + h))
    return pl.pallas_call(
        functools.partial(_moba_kernel, units=units),
        grid=(N_HEADS, bsz),
        in_specs=[smem, smem, smem, smem, head(0), head(N_HEADS), head(2 * N_HEADS)],
        out_specs=head(0),
        out_shape=jax.ShapeDtypeStruct((bsz, seq, ATTN_WIDTH), F32),
        scratch_shapes=[pltpu.VMEM((seq, HEAD_DIM), BF16),
                        pltpu.VMEM((seq + (grp - 1) * blk, HEAD_DIM), BF16),
                        pltpu.VMEM((nblk + grp - 1, HEAD_DIM, blk), BF16),
                        pltpu.VMEM((2, 2 * blk, blk), F32),
                        pltpu.VMEM((len(units) * grp, SUBLANES, blk), F32),
                        pltpu.VMEM((2, grp * blk, blk), F32),
                        pltpu.VMEM((2, 1, blk), F32),
                        pltpu.VMEM((2, grp * blk, blk), BF16),
                        pltpu.VMEM((2, 1, blk), F32),
                        pltpu.VMEM((2, 1, blk), F32),
                        pltpu.VMEM((2, 1, blk), F32),
                        pltpu.VMEM((2, HEAD_DIM, blk), F32)],
        compiler_params=_params(("arbitrary", "arbitrary")),
    )(uq, uk, uf, rel_bias, proj3, proj3, proj3)


def _ssd_kernel(xs_ref, z_ref, bc_ref, dt_ref, cwx_ref, cwbc_ref, cbx_ref, cbbc_ref,
                dtb_ref, alog_ref, dsk_ref, ng_ref, tri_ref, trit_ref, exp_ref, o_ref,
                xtail_ref, bctail_ref, state_ref):
    c = pl.program_id(1)
    t = CHUNK
    pad = SUBLANES

    @pl.when(c == 0)
    def _():
        xtail_ref[...] = jnp.zeros_like(xtail_ref)
        bctail_ref[...] = jnp.zeros_like(bctail_ref)
        state_ref[...] = jnp.zeros_like(state_ref)

    def conv_silu(tail_ref, src_ref, w_ref, b_ref):
        x = src_ref[0]
        acc = _causal_conv(tail_ref[...], x, w_ref, b_ref[...])
        tail_ref[...] = x[t - pad:t, :]
        return _silu(acc)

    xh = conv_silu(xtail_ref, xs_ref, cwx_ref, cbx_ref)
    bc = conv_silu(bctail_ref, bc_ref, cwbc_ref, cbbc_ref)

    dtr = dt_ref[0] + dtb_ref[...]
    dt = jnp.maximum(dtr, 0.0) + jnp.log1p(jnp.exp(-jnp.abs(dtr)))
    adt = dt * (-jnp.exp(alog_ref[...]))
    acs = _dot_exact_lhs(tri_ref[...], adt)
    acs_t = _dot_exact_rhs(adt.T, trit_ref[...])

    stack = jnp.concatenate([dt, jnp.exp(acs), jnp.exp(acs[t - 1:t, :] - acs)], axis=0)
    wide = _dot_exact_rhs(stack, exp_ref[...])
    dt_x, eacs_x, dst_x = wide[0:t], wide[t:2 * t], wide[2 * t:3 * t]

    xdt = xh * dt_x
    xdt_bf = xdt.astype(BF16)
    xdec_bf = (xdt * dst_x).astype(BF16)

    row = lax.broadcasted_iota(jnp.int32, (t, t), 0)
    col = lax.broadcasted_iota(jnp.int32, (t, t), 1)
    tril = row >= col
    lane = lax.broadcasted_iota(jnp.int32, (t, LANES), 1)
    heads_per_group = N_SSM_HEADS // N_GROUPS
    nt = (((1,), (1,)), ((), ()))
    tn = (((0,), (0,)), ((), ()))

    y_parts = []
    for g in range(N_GROUPS):
        bg = bc[:, g * D_STATE:(g + 1) * D_STATE].astype(BF16)
        cg = bc[:, (N_GROUPS + g) * D_STATE:(N_GROUPS + g + 1) * D_STATE].astype(BF16)
        cb = _dot(cg, bg, nt)
        for pair in range(heads_per_group // 2):
            ms = []
            for r in (g * heads_per_group + 2 * pair, g * heads_per_group + 2 * pair + 1):
                seg = acs[:, r:r + 1] - acs_t[r:r + 1, :]
                ms.append((cb * jnp.exp(jnp.where(tril, seg, NEG))).astype(BF16))
            q = g * (heads_per_group // 2) + pair
            y2 = _dot(jnp.concatenate(ms, axis=0), xdt_bf[:, q * LANES:(q + 1) * LANES])
            y_parts.append(jnp.where(lane < SSM_HEAD_DIM, y2[0:t], y2[t:2 * t]))
    y = jnp.concatenate(y_parts, axis=1)

    off_parts = []
    for g in range(N_GROUPS):
        sl = slice(g * GROUP_WIDTH, (g + 1) * GROUP_WIDTH)
        bg = bc[:, g * D_STATE:(g + 1) * D_STATE].astype(BF16)
        cg = bc[:, (N_GROUPS + g) * D_STATE:(N_GROUPS + g + 1) * D_STATE].astype(BF16)
        st = state_ref[:, sl]
        off_parts.append(_dot(cg, st.astype(BF16)))
        state_ref[:, sl] = eacs_x[t - 1:t, sl] * st + _dot(bg, xdec_bf[:, sl], tn)
    y = y + jnp.concatenate(off_parts, axis=1) * eacs_x + dsk_ref[...] * xh
    y = y * _silu(z_ref[0])

    outs = []
    for g in range(N_GROUPS):
        sl = slice(g * GROUP_WIDTH, (g + 1) * GROUP_WIDTH)
        outs.append(_rms(y[:, sl]) * ng_ref[:, sl])
    o_ref[0] = jnp.concatenate(outs, axis=1).astype(o_ref.dtype)


def _ssd(proj3, conv_w, conv_b, dt_bias, a_log, d_skip, norm_g):
    bsz, seq, _ = proj3.shape
    t = CHUNK
    gn2 = 2 * N_GROUPS * D_STATE
    pad_h = LANES - N_SSM_HEADS
    tri = jnp.tril(jnp.ones((t, t), F32)).astype(BF16)
    expand = jnp.pad(jnp.repeat(jnp.eye(N_SSM_HEADS, dtype=F32), SSM_HEAD_DIM, axis=1),
                     ((0, pad_h), (0, 0))).astype(BF16)
    const = lambda shape: pl.BlockSpec(shape, lambda b, c: (0,) * len(shape))
    return pl.pallas_call(
        _ssd_kernel,
        grid=(bsz, seq // t),
        in_specs=[pl.BlockSpec((1, t, SSM_WIDTH), lambda b, c: (b, c, COL_XS // SSM_WIDTH)),
                  pl.BlockSpec((1, t, SSM_WIDTH), lambda b, c: (b, c, COL_Z // SSM_WIDTH)),
                  pl.BlockSpec((1, t, gn2), lambda b, c: (b, c, COL_BC // gn2)),
                  pl.BlockSpec((1, t, LANES), lambda b, c: (b, c, COL_DT // LANES)),
                  const((SSM_CONV, SSM_WIDTH)), const((SSM_CONV, gn2)),
                  const((1, SSM_WIDTH)), const((1, gn2)),
                  const((1, LANES)), const((1, LANES)),
                  const((1, SSM_WIDTH)), const((1, SSM_WIDTH)),
                  const((t, t)), const((t, t)), const((LANES, SSM_WIDTH))],
        out_specs=pl.BlockSpec((1, t, SSM_WIDTH), lambda b, c: (b, c, 0)),
        out_shape=jax.ShapeDtypeStruct((bsz, seq, SSM_WIDTH), BF16),
        scratch_shapes=[pltpu.VMEM((SUBLANES, SSM_WIDTH), F32),
                        pltpu.VMEM((SUBLANES, gn2), F32),
                        pltpu.VMEM((D_STATE, SSM_WIDTH), F32)],
        compiler_params=_params(("arbitrary", "arbitrary")),
    )(proj3, proj3, proj3, proj3,
      conv_w[:, :SSM_WIDTH], conv_w[:, SSM_WIDTH:],
      conv_b[:SSM_WIDTH].reshape(1, -1), conv_b[SSM_WIDTH:].reshape(1, -1),
      jnp.pad(dt_bias, (0, pad_h)).reshape(1, LANES), jnp.pad(a_log, (0, pad_h)).reshape(1, LANES),
      jnp.repeat(d_skip, SSM_HEAD_DIM).reshape(1, SSM_WIDTH), norm_g.reshape(1, SSM_WIDTH),
      tri, tri.T, expand)


def _outproj_kernel(attn_ref, ssm_ref, x_ref, ag_ref, w_ref, gm_ref, g2_ref, sc_ref, sh_ref,
                    x1_ref, h2_ref):
    a = _rms(attn_ref[...]) * ag_ref[...]
    lhs = jnp.concatenate([a.astype(BF16), ssm_ref[...]], axis=-1)
    x1 = x_ref[...] + gm_ref[0] * _dot(lhs, w_ref[...])
    x1_ref[...] = x1
    h2_ref[...] = (_rms(x1) * (g2_ref[...] * (1.0 + sc_ref[0])) + sh_ref[0]).astype(BF16)


def _outproj(attn2, ssm2, x2, attn_g, w_out_bf, gate_m, g2, scale_f, shift_f, seq, tm):
    m, d = x2.shape
    per_b = seq // tm
    vec = pl.BlockSpec((1, 1, d), lambda i: (i // per_b, 0, 0))
    half = pl.BlockSpec((tm, ATTN_WIDTH), lambda i: (i, 0))
    full = pl.BlockSpec((tm, d), lambda i: (i, 0))
    return pl.pallas_call(
        _outproj_kernel,
        grid=(m // tm,),
        in_specs=[half, half, full,
                  pl.BlockSpec((1, ATTN_WIDTH), lambda i: (0, 0)),
                  _resident((d, d)),
                  vec, pl.BlockSpec((1, d), lambda i: (0, 0)), vec, vec],
        out_specs=[full, full],
        out_shape=[jax.ShapeDtypeStruct((m, d), F32), jax.ShapeDtypeStruct((m, d), BF16)],
        compiler_params=_params(("parallel",)),
    )(attn2, ssm2, x2, attn_g.reshape(1, -1), w_out_bf, gate_m, g2.reshape(1, d), scale_f, shift_f)


def _ffn_up_kernel(h_ref, wg_ref, wv_ref, cw_ref, cb_ref, wnext_ref, o_ref, wnext_bf_ref, wb_ref, tail_ref,
                   *, tm, seq):
    i = pl.program_id(1)
    pad = SUBLANES
    tn = o_ref.shape[1]

    @pl.when(i == 0)
    def _():
        wb_ref[:, 0:tn] = wg_ref[...].astype(BF16)
        wb_ref[:, tn:2 * tn] = wv_ref[...].astype(BF16)

    @pl.when((i * tm) % seq == 0)
    def _():
        tail_ref[...] = jnp.zeros((pad, 2 * tn), F32)

    u = _dot(h_ref[...], wb_ref[...])
    acc = _causal_conv(tail_ref[...], u, cw_ref, cb_ref[...])
    tail_ref[...] = u[tm - pad:tm, :]
    o_ref[...] = (_silu(acc[:, 0:tn]) * acc[:, tn:2 * tn]).astype(o_ref.dtype)
    wnext_bf_ref[...] = wnext_ref[...].astype(BF16)


def _ffn_up(h2, w_up, conv_w, conv_b, w_next, seq, tm, tn):
    m, d = h2.shape
    nj = FFN_DIM // tn
    ni = m // tm
    ride = pl.BlockSpec((w_next.shape[0] // (nj * ni), w_next.shape[1]), lambda j, i: (j * ni + i, 0))
    assert w_next.shape[0] % (nj * ni * 2 * SUBLANES) == 0
    pair = lambda a: jnp.concatenate([a[:, :FFN_DIM].reshape(-1, nj, tn), a[:, FFN_DIM:].reshape(-1, nj, tn)],
                                     axis=2).reshape(-1, 2 * FFN_DIM)
    return pl.pallas_call(
        functools.partial(_ffn_up_kernel, tm=tm, seq=seq),
        grid=(nj, ni),
        in_specs=[pl.BlockSpec((tm, d), lambda j, i: (i, 0)),
                  pl.BlockSpec((d, tn), lambda j, i: (0, j)),
                  pl.BlockSpec((d, tn), lambda j, i: (0, j + nj)),
                  pl.BlockSpec((FFN_CONV, 2 * tn), lambda j, i: (0, j)),
                  pl.BlockSpec((1, 2 * tn), lambda j, i: (0, j)),
                  ride],
        out_specs=[pl.BlockSpec((tm, tn), lambda j, i: (i, j)), ride],
        out_shape=[jax.ShapeDtypeStruct((m, FFN_DIM), BF16), jax.ShapeDtypeStruct(w_next.shape, BF16)],
        scratch_shapes=[pltpu.VMEM((d, 2 * tn), BF16), pltpu.VMEM((SUBLANES, 2 * tn), F32)],
        compiler_params=_params(("arbitrary", "arbitrary")),
    )(h2, w_up, w_up, pair(conv_w), pair(conv_b.reshape(1, -1)), w_next)


def _ffn_down_kernel(a_ref, w_ref, x1_ref, gf_ref, fg_ref, o_ref):
    x2 = x1_ref[...] + gf_ref[0] * _dot(a_ref[...], w_ref[...])
    o_ref[...] = _rms(x2) * fg_ref[...]


def _ffn_down(act, w_down_bf, x1, gate_f, final_g, seq, tm):
    m, d = x1.shape
    per_b = seq // tm
    row = lambda width: pl.BlockSpec((tm, width), lambda i: (i, 0))
    return pl.pallas_call(
        _ffn_down_kernel,
        grid=(m // tm,),
        in_specs=[row(FFN_DIM), _resident((FFN_DIM, d)), row(d),
                  pl.BlockSpec((1, 1, d), lambda i: (i // per_b, 0, 0)),
                  pl.BlockSpec((1, d), lambda i: (0, 0))],
        out_specs=row(d),
        out_shape=jax.ShapeDtypeStruct((m, d), F32),
        compiler_params=_params(("parallel",)),
    )(act, w_down_bf, x1, gate_f, final_g.reshape(1, d))


def kernel(x, c, w_ada, b_ada, norm_mix_g, w_in, rel_bias, attn_norm_g, conv_ssm_w, conv_ssm_b, dt_bias,
           a_log, d_skip, ssm_norm_g, w_out, norm_ffn_g, w_up, conv_ffn_w, conv_ffn_b, w_down, final_norm_g):
    bsz, seq, d = x.shape
    m = bsz * seq
    assert w_ada.shape[0] == 1, "the final RMSNorm is fused into the (single) layer's ffn_down kernel"
    l = 0
    x2 = x.reshape(m, d)
    mod = _ada(c, w_ada[l], b_ada[l])
    shift_m, scale_m, gate_m, shift_f, scale_f, gate_f = [
        mod[:, k * d:(k + 1) * d].reshape(bsz, 1, d) for k in range(6)]

    w_in_t = w_in[l].T
    proj, w_out_bf = _normproj(x2, norm_mix_g[l], scale_m, shift_m, w_in_t.astype(BF16), PROJ_COLS, w_out[l],
                               seq, PROJ_TM)
    proj3 = proj.reshape(bsz, seq, PROJ_COLS)

    attn = _moba(rel_bias, proj3)
    ssm = _ssd(proj3, conv_ssm_w[l], conv_ssm_b[l], dt_bias[l], a_log[l], d_skip[l], ssm_norm_g[l])

    x1, h2 = _outproj(attn.reshape(m, ATTN_WIDTH), ssm.reshape(m, SSM_WIDTH), x2, attn_norm_g[l],
                      w_out_bf, gate_m, norm_ffn_g[l], scale_f, shift_f, seq, OUT_TM)
    act, w_down_bf = _ffn_up(h2, w_up[l], conv_ffn_w[l], conv_ffn_b[l], w_down[l], seq, UP_TM, UP_TN)
    out = _ffn_down(act, w_down_bf, x1, gate_f, final_norm_g, seq, DOWN_TM)
    return out.reshape(bsz, seq, d)
```
